```python
import math
import numpy as np
import jax
import jax.numpy as jnp
from jax import lax

D_MODEL = 1024
BATCH = 8
SEQ = 2048
DEPTH = 4
DEC_BATCH = 128
DEC_SEQ = 4
PAST_LEN = 2048
PAGE_SIZE = 128

HEAD_DIM = 64
H_A = 8
G_A = 4
R_A = H_A // G_A
H_IDX = 4
D_IDX = 64
DSA_TOPK = 256
D_CONV = 512
CONV_W = 31
H_C = 8
G_C = 2
R_C = H_C // G_C
COMP_BLOCK = 32
COMP_STRIDE = 16
SEL_BLOCK = 64
SEL_TOPK = 8
WINDOW = 512
D_FF = 2816
FFN_CONV_W = 3
NUM_BUCKETS = 32
MAX_DISTANCE = 128
N_BRANCH = 3
Q_BLOCK = 64
FORCE_SCORE = 1.0e4
EPS = 1e-6
SPLITS = (H_A * HEAD_DIM, G_A * HEAD_DIM, G_A * HEAD_DIM, H_IDX * D_IDX, H_IDX, D_IDX,
          2 * D_CONV, H_C * HEAD_DIM, 6 * G_C * HEAD_DIM, 3 * H_C, N_BRANCH * D_MODEL)
P_IN = sum(SPLITS)

kernel_name = "hybrid_dsa_conformer_nsa_adaln_decoder_step"


def rmsnorm(x, g):
    xf = x.astype(jnp.float32)
    y = xf * lax.rsqrt(jnp.mean(xf * xf, axis=-1, keepdims=True) + EPS)
    return (y * g.astype(jnp.float32)).astype(x.dtype)


def layernorm(x, g, b):
    xf = x.astype(jnp.float32)
    mu = jnp.mean(xf, axis=-1, keepdims=True)
    var = jnp.mean(jnp.square(xf - mu), axis=-1, keepdims=True)
    return ((xf - mu) * lax.rsqrt(var + EPS) * g.astype(jnp.float32) + b.astype(jnp.float32)).astype(x.dtype)


def t5_bucket(dist):
    n = jnp.maximum(dist, 0)
    max_exact = NUM_BUCKETS // 2
    nf = jnp.maximum(n, 1).astype(jnp.float32)
    large = max_exact + (jnp.log(nf / max_exact) / math.log(MAX_DISTANCE / max_exact)
                         * (NUM_BUCKETS - max_exact)).astype(jnp.int32)
    return jnp.where(n < max_exact, n, jnp.minimum(large, NUM_BUCKETS - 1))


def masked_softmax(logits, valid):
    lf = jnp.where(valid, logits.astype(jnp.float32), -jnp.inf)
    m = jnp.max(lf, axis=-1, keepdims=True)
    e = jnp.exp(lf - jnp.where(jnp.isfinite(m), m, 0.0))
    return e / jnp.maximum(jnp.sum(e, axis=-1, keepdims=True), 1e-30)


def attend_shared(q, k, v, dist, valid, tab):
    s = jnp.einsum('nqgrd,nkgd->nqgrk', q, k).astype(jnp.float32) * HEAD_DIM ** -0.5
    s = s + jnp.transpose(tab[t5_bucket(dist)], (0, 2, 3, 1)).astype(jnp.float32)
    p = masked_softmax(s, valid[:, None, None, :])
    return jnp.einsum('nqgrk,nkgd->nqgrd', p.astype(v.dtype), v), p


def attend_gathered(q, k, v, dist, valid, tab):
    s = jnp.einsum('nqgrd,nqgmd->nqgrm', q, k).astype(jnp.float32) * HEAD_DIM ** -0.5
    b = tab[t5_bucket(dist), jnp.arange(tab.shape[1])[:, None]]
    s = s + jnp.swapaxes(b, -1, -2).astype(jnp.float32)
    p = masked_softmax(s, valid[:, :, :, None, :])
    return jnp.einsum('nqgrm,nqgmd->nqgrd', p.astype(v.dtype), v)


def dwconv_valid(x, w, b):
    y = lax.conv_general_dilated(x, w[:, None, :].astype(x.dtype), window_strides=(1,), padding='VALID',
                                 dimension_numbers=('NWC', 'WIO', 'NWC'), feature_group_count=x.shape[-1])
    return y + b


def unblock(o):
    o = jnp.swapaxes(o, 0, 1)
    return o.reshape((o.shape[0], o.shape[1] * o.shape[2]) + o.shape[3:])


def gather_rows(pool, layer, page_table, new, pos, slots, head=None):
    n = pos.shape[0]
    bidx = jnp.arange(n).reshape((n,) + (1,) * (pos.ndim - 1))
    past = jnp.minimum(pos, PAST_LEN - 1)
    page = page_table[bidx, past // PAGE_SIZE]
    off = past % PAGE_SIZE
    newp = jnp.clip(pos - PAST_LEN, 0, new.shape[1] - 1)
    if head is None:
        rp = pool[page, layer, off, slots]
        rn = new[bidx, newp]
    else:
        rp = pool[page, layer, off, slots, head]
        rn = new[bidx, newp, :, head]
    is_past = (pos < PAST_LEN).reshape(pos.shape + (1,) * (rn.ndim - pos.ndim))
    return jnp.where(is_past, rp, rn)


def dsa_select(iq, iw, ik, q_pos):
    n_keys = ik.shape[1]
    s = jax.nn.relu(jnp.einsum('nqhd,nld->nqhl', iq, ik))
    score = jnp.einsum('nqh,nqhl->nql', iw, s).astype(jnp.float32)
    adm = jnp.arange(n_keys)[None, :] <= q_pos[:, None]
    top, sel = lax.top_k(jnp.where(adm[None], score, -jnp.inf), min(DSA_TOPK, n_keys // 4))
    return sel, jnp.isfinite(top)


def dsa_attend(q, ks, vs, sel, valid, q_pos, tab):
    dist = q_pos[None, :, None] - sel
    o = attend_gathered(q, jnp.swapaxes(ks, 2, 3), jnp.swapaxes(vs, 2, 3), dist[:, :, None], valid[:, :, None], tab)
    return o.reshape(q.shape[0], q.shape[1], H_A * HEAD_DIM)


def dsa_prompt(q, k, v, iq, iw, ik, tab):
    L = q.shape[1]
    take = jax.vmap(lambda a, s: a[s])

    def block(i):
        q0 = i * Q_BLOCK
        q_pos = q0 + jnp.arange(Q_BLOCK)
        sl = lambda a: lax.dynamic_slice_in_dim(a, q0, Q_BLOCK, axis=1)
        sel, valid = dsa_select(sl(iq), sl(iw), ik, q_pos)
        return dsa_attend(sl(q), take(k, sel), take(v, sel), sel, valid, q_pos, tab)

    return unblock(lax.map(block, jnp.arange(L // Q_BLOCK)))


def dsa_sample(q, k, v, iq, iw, ik, kv_pool, idx_pool, page_table, layer, tab):
    N, T = q.shape[:2]
    L = PAST_LEN + T
    q_pos = PAST_LEN + jnp.arange(T)
    all_pos = jnp.broadcast_to(jnp.arange(L), (N, L))
    ik_all = gather_rows(idx_pool, layer, page_table, ik, all_pos, slice(None))
    sel, valid = dsa_select(iq, iw, ik_all, q_pos)
    kv = gather_rows(kv_pool, layer, page_table, jnp.stack([k, v], axis=2), sel, slice(None))
    return dsa_attend(q, kv[:, :, :, 0], kv[:, :, :, 1], sel, valid, q_pos, tab)


def compress(x, pe, w):
    n_cmp = (x.shape[1] - COMP_BLOCK) // COMP_STRIDE + 1
    idx = np.arange(n_cmp)[:, None] * COMP_STRIDE + np.arange(COMP_BLOCK)[None, :]
    blocks = x[:, idx] + pe[:, None, :]
    return jnp.einsum('nclgd,lde->ncge', blocks, w.reshape(COMP_BLOCK, HEAD_DIM, HEAD_DIM))


def nsa_select(p_cmp, q_pos, n_keys):
    n_cmp = p_cmp.shape[-1]
    n_sel = -(-n_keys // SEL_BLOCK)
    c0 = np.arange(n_cmp)[:, None] * COMP_STRIDE
    s0 = np.arange(n_sel)[None, :] * SEL_BLOCK
    cover = ((c0 < s0 + SEL_BLOCK) & (c0 + COMP_BLOCK > s0)).astype(np.float32)
    imp = jnp.einsum('nqgrc,cs->nqgs', p_cmp, jnp.asarray(cover))
    blk = jnp.arange(n_sel)[None, :]
    cur = (q_pos // SEL_BLOCK)[:, None]
    forced = (blk == 0) | (blk == cur) | (blk == cur - 1)
    adm = blk <= cur
    score = jnp.where(adm[None, :, None], imp + jnp.where(forced, FORCE_SCORE, 0.0)[None, :, None], -jnp.inf)
    top, sel = lax.top_k(score, min(SEL_TOPK, n_sel))
    return sel, jnp.isfinite(top)


def nsa_cmp_select(q, ck, cv, q_pos, w_cmp, pe_cmp, tab):
    k_c = compress(ck, pe_cmp[0], w_cmp[0])
    v_c = compress(cv, pe_cmp[1], w_cmp[1])
    end = jnp.arange(k_c.shape[1]) * COMP_STRIDE + COMP_BLOCK - 1
    dist = q_pos[:, None] - end[None, :]
    o_cmp, p_cmp = attend_shared(q, k_c, v_c, dist, dist >= 0, tab)
    blk, bvalid = nsa_select(p_cmp, q_pos, ck.shape[1])
    return o_cmp, blk, bvalid


def nsa_prompt(q, kv, w_cmp, pe_cmp, tab):
    N, L = q.shape[:2]
    o_cmp, blk, bvalid = nsa_cmp_select(q, kv[:, :, 0], kv[:, :, 1], jnp.arange(L), w_cmp, pe_cmp, tab)
    n_sel = L // SEL_BLOCK
    sblocks = kv[:, :, 2:4].reshape(N, n_sel, SEL_BLOCK, 2, G_C, HEAD_DIM).transpose(0, 4, 1, 2, 3, 5)
    wpad = jnp.pad(kv[:, :, 4:6], ((0, 0), (WINDOW, 0), (0, 0), (0, 0), (0, 0)))
    take = jax.vmap(jax.vmap(lambda a, s: a[s]))

    def block(i):
        q0 = i * Q_BLOCK
        q_pos = q0 + jnp.arange(Q_BLOCK)
        sl = lambda a, n=Q_BLOCK: lax.dynamic_slice_in_dim(a, q0, n, axis=1)
        qb, bb = sl(q), sl(blk)
        g = take(sblocks, jnp.transpose(bb, (0, 2, 1, 3)))
        g = jnp.transpose(g, (0, 2, 1, 3, 4, 5, 6)).reshape(N, Q_BLOCK, G_C, -1, 2, HEAD_DIM)
        tok = (bb[..., None] * SEL_BLOCK + jnp.arange(SEL_BLOCK)).reshape(N, Q_BLOCK, G_C, -1)
        dist = q_pos[None, :, None, None] - tok
        valid = jnp.repeat(sl(bvalid), SEL_BLOCK, axis=-1) & (dist >= 0)
        o_slc = attend_gathered(qb, g[..., 0, :], g[..., 1, :], dist, valid, tab)
        wb = sl(wpad, Q_BLOCK + WINDOW)
        k_pos = q0 - WINDOW + jnp.arange(Q_BLOCK + WINDOW)
        wd = q_pos[:, None] - k_pos[None, :]
        o_win, _ = attend_shared(qb, wb[:, :, 0], wb[:, :, 1], wd,
                                 (k_pos[None, :] >= 0) & (wd >= 0) & (wd <= WINDOW), tab)
        return o_slc, o_win

    o_slc, o_win = lax.map(block, jnp.arange(L // Q_BLOCK))
    return o_cmp, unblock(o_slc), unblock(o_win), kv[:, -min(WINDOW, L):, 4:6]


def nsa_sample(q, kv, nsa_pool, win_buf, page_table, layer, w_cmp, pe_cmp, tab):
    N, T = q.shape[:2]
    L = PAST_LEN + T
    q_pos = PAST_LEN + jnp.arange(T)
    all_pos = jnp.broadcast_to(jnp.arange(L), (N, L))
    cmp_rows = gather_rows(nsa_pool, layer, page_table, kv[:, :, 0:2], all_pos, slice(0, 2))
    o_cmp, blk, bvalid = nsa_cmp_select(q, cmp_rows[:, :, 0], cmp_rows[:, :, 1], q_pos, w_cmp, pe_cmp, tab)
    tok = (blk[..., None] * SEL_BLOCK + jnp.arange(SEL_BLOCK)).reshape(N, T, G_C, -1)
    dist = q_pos[None, :, None, None] - tok
    valid = jnp.repeat(bvalid, SEL_BLOCK, axis=-1) & (dist >= 0)
    head = jnp.arange(G_C)[None, None, :, None]
    sel = gather_rows(nsa_pool, layer, page_table, kv[:, :, 2:4], jnp.minimum(tok, L - 1),
                      slice(2, 4), head)
    o_slc = attend_gathered(q, sel[..., 0, :], sel[..., 1, :], dist, valid, tab)
    wk = jnp.concatenate([win_buf, kv[:, :, 4:6]], axis=1)
    k_pos = PAST_LEN - win_buf.shape[1] + jnp.arange(wk.shape[1])
    wd = q_pos[:, None] - k_pos[None, :]
    o_win, _ = attend_shared(q, wk[:, :, 0], wk[:, :, 1], wd, (wd >= 0) & (wd <= WINDOW), tab)
    return o_cmp, o_slc, o_win, wk[:, -min(WINDOW, L):]


def conv_module(glu, hist, p):
    a, b = jnp.split(glu, 2, axis=-1)
    z = jnp.concatenate([hist, a * jax.nn.sigmoid(b)], axis=1)
    y = jax.nn.silu(layernorm(dwconv_valid(z, p['conv_w'], p['conv_b']), p['ln_g'], p['ln_b']))
    return y @ p['w_ob'], z[:, -(CONV_W - 1):]


def layer(x, c, p, layer_idx, ctx):
    N, T, _ = x.shape
    sh1, sc1, gt1, sh2, sc2, gt2 = jnp.split((jax.nn.silu(c) @ p['w_mod'] + p['b_mod'])[:, None, :], 6, axis=-1)
    h = rmsnorm(x, p['g_mix']) * (1 + sc1) + sh1
    qa, ka, va, iq, iw, ik, glu, qc, kvc, gc, gm = jnp.split(h @ p['w_in'], np.cumsum(SPLITS)[:-1].tolist(), axis=-1)
    qa = qa.reshape(N, T, G_A, R_A, HEAD_DIM)
    ka = ka.reshape(N, T, G_A, HEAD_DIM)
    va = va.reshape(N, T, G_A, HEAD_DIM)
    iq = iq.reshape(N, T, H_IDX, D_IDX)
    qc = qc.reshape(N, T, G_C, R_C, HEAD_DIM)
    kvc = kvc.reshape(N, T, 6, G_C, HEAD_DIM)
    gc = jax.nn.sigmoid(gc).reshape(N, T, 3, G_C, R_C, 1)
    tab_a = p['rel_bias'][:, :H_A].reshape(NUM_BUCKETS, G_A, R_A)
    tab_c = p['rel_bias'][:, H_A:].reshape(NUM_BUCKETS, G_C, R_C)
    if ctx is None:
        o_a = dsa_prompt(qa, ka, va, iq, iw, ik, tab_a)
        o_cmp, o_slc, o_win, win_rows = nsa_prompt(qc, kvc, p['w_cmp'], p['pe_cmp'], tab_c)
        conv_hist = jnp.zeros((N, CONV_W - 1, D_CONV), x.dtype)
        ffn_hist = jnp.zeros((N, FFN_CONV_W - 1, D_FF), x.dtype)
    else:
        kv_pool, idx_pool, nsa_pool, win_buf, conv_hist, ffn_hist, page_table = ctx
        o_a = dsa_sample(qa, ka, va, iq, iw, ik, kv_pool, idx_pool, page_table, layer_idx, tab_a)
        o_cmp, o_slc, o_win, win_rows = nsa_sample(qc, kvc, nsa_pool, win_buf, page_table, layer_idx,
                                                   p['w_cmp'], p['pe_cmp'], tab_c)
    o_c = (gc[:, :, 0] * o_cmp + gc[:, :, 1] * o_slc + gc[:, :, 2] * o_win).reshape(N, T, H_C * HEAD_DIM)
    o_b, conv_rows = conv_module(glu, conv_hist, p)
    gm = jax.nn.sigmoid(gm).reshape(N, T, N_BRANCH, D_MODEL)
    merged = gm[:, :, 0] * (o_a @ p['w_oa']) + gm[:, :, 1] * o_b + gm[:, :, 2] * (o_c @ p['w_oc'])
    x = x + gt1 * (merged @ p['w_o'])
    h2 = rmsnorm(x, p['g_ffn']) * (1 + sc2) + sh2
    a, b = jnp.split(h2 @ p['w_up'], 2, axis=-1)
    a_h = jnp.concatenate([ffn_hist, a], axis=1)
    y = (jax.nn.gelu(dwconv_valid(a_h, p['ffn_conv_w'], p['ffn_conv_b'])) * b) @ p['w_down']
    x = x + gt2 * y
    new = (jnp.stack([ka, va], axis=2), ik, kvc[:, :, :4], win_rows, conv_rows, a_h[:, -(FFN_CONV_W - 1):])
    return x, new


def setup_inputs(seed: int = 0) -> dict:
    key = jax.random.key(seed)
    ks = iter(jax.random.split(key, 48))

    def nrm(shape, scale):
        return jax.random.normal(next(ks), shape, jnp.float32) * scale

    D = D_MODEL
    n_pages = PAST_LEN // PAGE_SIZE
    n_used = DEC_BATCH * n_pages
    n_pool = n_used + (n_used + 3) // 4
    page_table = jax.random.permutation(next(ks), n_pool)[:n_used].reshape(DEC_BATCH, n_pages).astype(jnp.int32)
    w_buf = min(WINDOW, PAST_LEN)
    return {
        "x_prompt": nrm((BATCH, SEQ, D), 1.0),
        "x_sample": nrm((DEC_BATCH, DEC_SEQ, D), 1.0),
        "c_prompt": nrm((BATCH, D), 1.0),
        "c_sample": nrm((DEC_BATCH, D), 1.0),
        "cache_dsa_kv": nrm((n_pool, DEPTH, PAGE_SIZE, 2, G_A, HEAD_DIM), 1.0),
        "cache_dsa_idx": nrm((n_pool, DEPTH, PAGE_SIZE, D_IDX), 1.0),
        "cache_nsa_kv": nrm((n_pool, DEPTH, PAGE_SIZE, 4, G_C, HEAD_DIM), 1.0),
        "state_nsa_win": nrm((DEC_BATCH, DEPTH, w_buf, 2, G_C, HEAD_DIM), 1.0),
        "state_conv": nrm((DEC_BATCH, DEPTH, CONV_W - 1, D_CONV), 0.5),
        "state_ffn": nrm((DEC_BATCH, DEPTH, FFN_CONV_W - 1, D_FF), 1.0),
        "page_table": page_table,
        "rel_bias": nrm((NUM_BUCKETS, H_A + H_C), 0.5),
        "w_mod": nrm((DEPTH, D, 6 * D), 0.5 * D ** -0.5),
        "b_mod": nrm((DEPTH, 6 * D), 0.01),
        "g_mix": 1.0 + nrm((DEPTH, D), 0.1),
        "g_ffn": 1.0 + nrm((DEPTH, D), 0.1),
        "w_in": nrm((DEPTH, D, P_IN), D ** -0.5),
        "w_cmp": nrm((DEPTH, 2, COMP_BLOCK * HEAD_DIM, HEAD_DIM), (COMP_BLOCK * HEAD_DIM) ** -0.5),
        "pe_cmp": nrm((DEPTH, 2, COMP_BLOCK, HEAD_DIM), 0.1),
        "conv_w": nrm((DEPTH, CONV_W, D_CONV), CONV_W ** -0.5),
        "conv_b": nrm((DEPTH, D_CONV), 0.01),
        "ln_g": 1.0 + nrm((DEPTH, D_CONV), 0.1),
        "ln_b": nrm((DEPTH, D_CONV), 0.01),
        "w_oa": nrm((DEPTH, H_A * HEAD_DIM, D), (H_A * HEAD_DIM) ** -0.5),
        "w_ob": nrm((DEPTH, D_CONV, D), D_CONV ** -0.5),
        "w_oc": nrm((DEPTH, H_C * HEAD_DIM, D), (H_C * HEAD_DIM) ** -0.5),
        "w_o": nrm((DEPTH, D, D), D ** -0.5),
        "w_up": nrm((DEPTH, D, 2 * D_FF), D ** -0.5),
        "ffn_conv_w": nrm((DEPTH, FFN_CONV_W, D_FF), FFN_CONV_W ** -0.5),
        "ffn_conv_b": nrm((DEPTH, D_FF), 0.01),
        "w_down": nrm((DEPTH, D_FF, D), D_FF ** -0.5),
        "g_final": 1.0 + nrm((D,), 0.1),
    }


def reference(x_prompt, x_sample, c_prompt, c_sample, cache_dsa_kv, cache_dsa_idx, cache_nsa_kv,
              state_nsa_win, state_conv, state_ffn, page_table, rel_bias, w_mod, b_mod, g_mix, g_ffn,
              w_in, w_cmp, pe_cmp, conv_w, conv_b, ln_g, ln_b, w_oa, w_ob, w_oc, w_o,
              w_up, ffn_conv_w, ffn_conv_b, w_down, g_final):
    xp, xs = x_prompt, x_sample
    new_p, new_s = [], []
    for l in range(DEPTH):
        p = dict(rel_bias=rel_bias, w_mod=w_mod[l], b_mod=b_mod[l], g_mix=g_mix[l], g_ffn=g_ffn[l],
                 w_in=w_in[l], w_cmp=w_cmp[l], pe_cmp=pe_cmp[l], conv_w=conv_w[l], conv_b=conv_b[l],
                 ln_g=ln_g[l], ln_b=ln_b[l], w_oa=w_oa[l], w_ob=w_ob[l], w_oc=w_oc[l], w_o=w_o[l],
                 w_up=w_up[l], ffn_conv_w=ffn_conv_w[l], ffn_conv_b=ffn_conv_b[l], w_down=w_down[l])
        xp, rows_p = layer(xp, c_prompt, p, l, None)
        xs, rows_s = layer(xs, c_sample, p, l, (cache_dsa_kv, cache_dsa_idx, cache_nsa_kv, state_nsa_win[:, l],
                                                 state_conv[:, l], state_ffn[:, l], page_table))
        new_p.append(rows_p)
        new_s.append(rows_s)
    y_prompt = rmsnorm(xp, g_final)
    y_sample = rmsnorm(xs, g_final)

    def stk(group, j):
        return jnp.stack([rows[j] for rows in group], axis=1)

    return (y_prompt, y_sample,
            stk(new_p, 0), stk(new_p, 1), stk(new_p, 2), stk(new_p, 3), stk(new_p, 4), stk(new_p, 5),
            stk(new_s, 0), stk(new_s, 1), stk(new_s, 2), stk(new_s, 3), stk(new_s, 4), stk(new_s, 5))
```

```python
import functools
import math

import numpy as np
import jax
import jax.numpy as jnp
from jax import lax
from jax.experimental import pallas as pl
from jax.experimental.pallas import tpu as pltpu

HEAD_DIM = 64
H_A, G_A = 8, 4
R_A = H_A // G_A
H_IDX, D_IDX = 4, 64
DSA_TOPK = 256
D_CONV, CONV_W = 512, 31
H_C, G_C = 8, 2
R_C = H_C // G_C
COMP_BLOCK, COMP_STRIDE = 32, 16
SEL_BLOCK, SEL_TOPK = 64, 8
WINDOW = 512
FFN_CONV_W = 3
NUM_BUCKETS, MAX_DISTANCE = 32, 128
FORCE_SCORE = 1.0e4
EPS = 1e-6
PAGE_SIZE = 128

CH = 128
FT_ROWS = 640
VMEM_LIMIT = 56 * 1024 * 1024

f32, bf16, i32 = jnp.float32, jnp.bfloat16, jnp.int32
NEG_INF = float("-inf")


def _cparams(n_axes):
    return pltpu.CompilerParams(dimension_semantics=("arbitrary",) * n_axes, vmem_limit_bytes=VMEM_LIMIT)


def _dot(a, b):
    return jnp.dot(a, b, preferred_element_type=f32)


def _dot_nt(a, b):
    return lax.dot_general(a, b, (((1,), (1,)), ((), ())), preferred_element_type=f32)


def _colsum(x):
    return x.reshape(CH // 8, 8, CH).sum(axis=0)


def _colmax(x):
    return x.reshape(CH // 8, 8, CH).max(axis=0)


def _mod_kernel(c_ref, w_ref, b_ref, o_ref):
    c = c_ref[...]
    a = (c * jax.nn.sigmoid(c)).astype(bf16)
    o_ref[0] = _dot(a, w_ref[0].astype(bf16)) + b_ref[0]


def _modulation(c_all, w_mod, b_mod):
    depth, d, d6 = w_mod.shape
    n = c_all.shape[0]
    tn = 1024
    return pl.pallas_call(
        _mod_kernel,
        grid=(depth, d6 // tn),
        in_specs=[pl.BlockSpec((n, d), lambda l, j: (0, 0)),
                  pl.BlockSpec((1, d, tn), lambda l, j: (l, 0, j)),
                  pl.BlockSpec((1, 1, tn), lambda l, j: (l, 0, j))],
        out_specs=pl.BlockSpec((1, n, tn), lambda l, j: (l, 0, j)),
        out_shape=jax.ShapeDtypeStruct((depth, n, d6), f32),
        compiler_params=_cparams(2),
    )(c_all, w_mod, b_mod.reshape(depth, 1, d6))


def _t5_bucket(n):
    max_exact = NUM_BUCKETS // 2
    nf = jnp.maximum(n, 1).astype(f32)
    large = max_exact + (jnp.log(nf / max_exact) / math.log(MAX_DISTANCE / max_exact)
                         * (NUM_BUCKETS - max_exact)).astype(i32)
    return jnp.where(n < max_exact, n, jnp.minimum(large, NUM_BUCKETS - 1))


def _bias_from_dist(dist, tab_ref, h):
    b = _t5_bucket(jnp.maximum(dist, 0))
    out = jnp.zeros(dist.shape, f32)
    for k in range(NUM_BUCKETS):
        out = jnp.where(b == k, tab_ref[k, h], out)
    return out


def _toep_kernel(tab_ref, o_ref):
    h = pl.program_id(0)
    row = lax.broadcasted_iota(i32, (CH, CH), 0)
    lane = lax.broadcasted_iota(i32, (CH, CH), 1)
    o_ref[0, 0] = _bias_from_dist(jnp.full((CH, CH), 2 * CH, i32), tab_ref, h)
    o_ref[0, 1] = _bias_from_dist(CH + lane - row, tab_ref, h)
    o_ref[0, 2] = _bias_from_dist(lane - row, tab_ref, h)


def _toeplitz_bias(rel_bias):
    nh = rel_bias.shape[1]
    return pl.pallas_call(
        _toep_kernel,
        grid=(nh,),
        in_specs=[pl.BlockSpec(memory_space=pltpu.SMEM)],
        out_specs=pl.BlockSpec((1, 3, CH, CH), lambda h: (h, 0, 0, 0)),
        out_shape=jax.ShapeDtypeStruct((nh, 3, CH, CH), f32),
        compiler_params=_cparams(1),
    )(rel_bias)


def _cmpbias_kernel(tab_ref, o_ref):
    h = pl.program_id(0)
    qb = pl.program_id(1)
    row = lax.broadcasted_iota(i32, (CH, CH), 0)
    lane = lax.broadcasted_iota(i32, (CH, CH), 1)
    dist = qb * CH + lane - (row * COMP_STRIDE + COMP_BLOCK - 1)
    o_ref[0, 0] = _bias_from_dist(dist, tab_ref, H_A + h)


def _cmp_bias(rel_bias, nqb):
    return pl.pallas_call(
        _cmpbias_kernel,
        grid=(H_C, nqb),
        in_specs=[pl.BlockSpec(memory_space=pltpu.SMEM)],
        out_specs=pl.BlockSpec((1, 1, CH, CH), lambda h, q: (h, q, 0, 0)),
        out_shape=jax.ShapeDtypeStruct((H_C, nqb, CH, CH), f32),
        compiler_params=_cparams(2),
    )(rel_bias)


def _inproj_kernel(x_ref, sh_ref, sc_ref, g_ref, *refs, n_w):
    w_refs = refs[:n_w]
    wft_ref = refs[n_w]
    out_refs = refs[n_w + 1:2 * n_w + 1]
    ft_ref = refs[2 * n_w + 1]
    x = x_ref[0]
    y = x * lax.rsqrt(jnp.mean(x * x, axis=-1, keepdims=True) + EPS) * g_ref[...]
    hb = (y * (1.0 + sc_ref[0]) + sh_ref[0]).astype(bf16)
    for w_ref, o_ref in zip(w_refs, out_refs):
        o_ref[0] = _dot(hb, w_ref[...])
    ft = _dot_nt(wft_ref[...], hb)
    for j in range(ft.shape[1] // CH):
        ft_ref[0, j] = ft[:, j * CH:(j + 1) * CH]


def _mod_spec(m, tm):
    if m.shape[1] == 1:
        return pl.BlockSpec((1, 1, m.shape[2]), lambda n, i: (n, 0, 0))
    return pl.BlockSpec((1, tm, m.shape[2]), lambda n, i: (n, i, 0))


def _const_spec(a):
    nd = a.ndim
    return pl.BlockSpec(a.shape, lambda *_: (0,) * nd, pipeline_mode=pl.Buffered(1))


def _inproj(x, sh, sc, g, ws, wft, tm):
    nb, t, d = x.shape
    n_w = len(ws)
    in_specs = [pl.BlockSpec((1, tm, d), lambda n, i: (n, i, 0)), _mod_spec(sh, tm), _mod_spec(sc, tm),
                _const_spec(g)] + [_const_spec(w) for w in ws] + [_const_spec(wft)]
    out_specs = [pl.BlockSpec((1, tm, w.shape[1]), lambda n, i: (n, i, 0)) for w in ws]
    out_specs.append(pl.BlockSpec((1, tm // CH, FT_ROWS, CH), lambda n, i: (n, i, 0, 0)))
    out_shape = [jax.ShapeDtypeStruct((nb, t, w.shape[1]), f32) for w in ws]
    out_shape.append(jax.ShapeDtypeStruct((nb, t // CH, FT_ROWS, CH), f32))
    return pl.pallas_call(
        functools.partial(_inproj_kernel, n_w=n_w),
        grid=(nb, t // tm), in_specs=in_specs, out_specs=out_specs, out_shape=out_shape,
        compiler_params=_cparams(2),
    )(x, sh, sc, g, *ws, wft)


def _sortable(x):
    x = jnp.where(x == 0.0, 0.0, x)
    b = lax.bitcast_convert_type(x, i32)
    return jnp.where(b < 0, b ^ 0x7FFFFFFF, b)


def _head_operand(tile, src_half, dst_half, mask_other):
    if src_half != dst_half:
        tile = pltpu.roll(tile, 64, 1)
    if mask_other:
        lane = lax.broadcasted_iota(i32, tile.shape, 1)
        keep = (lane < 64) if dst_half == 0 else (lane >= 64)
        tile = jnp.where(keep, tile, 0.0)
    return tile.astype(bf16)


def _softmax_pv(lo, hi, score_fn, vt_fn, s_scr, want_p=False):
    def p1(c, macc):
        s = score_fn(c)
        s_scr[c] = s
        return jnp.maximum(macc, _colmax(s))

    macc = lax.fori_loop(lo, hi, p1, jnp.full((8, CH), NEG_INF, f32))
    m = jnp.max(macc, axis=0, keepdims=True)
    m = jnp.where(m > NEG_INF, m, 0.0)

    def p2(c, carry):
        lacc, oacc = carry
        p = jnp.exp(s_scr[c] - m)
        return lacc + _colsum(p), oacc + _dot(vt_fn(c).astype(bf16), p.astype(bf16))

    lacc, oacc = lax.fori_loop(lo, hi, p2, (jnp.zeros((8, CH), f32), jnp.zeros((HEAD_DIM, CH), f32)))
    inv = 1.0 / jnp.maximum(jnp.sum(lacc, axis=0, keepdims=True), 1e-30)
    return oacc * inv


def _dsa_core(i_abs, nci, topk, pos_bits, q_tile, iq_tile, iw_t, get_ik, get_k, get_vt, toep_ref,
              keys_scr, mask_scr, s_scr, ot_scr):
    row = lax.broadcasted_iota(i32, (CH, CH), 0)
    lane = lax.broadcasted_iota(i32, (CH, CH), 1)
    t_pos = i_abs * CH + lane

    iqp = [_head_operand(iq_tile(h // 2), h % 2, 0, True) for h in range(H_IDX)]

    def idx_body(c, carry):
        ikc = get_ik(c).astype(bf16)
        acc = jnp.zeros((CH, CH), f32)
        for h in range(H_IDX):
            acc = acc + iw_t[h:h + 1, :] * jnp.maximum(_dot_nt(ikc, iqp[h]), 0.0)
        acc = jnp.where(c * CH + row <= t_pos, acc, NEG_INF)
        keys_scr[c] = _sortable(acc)
        return carry

    lax.fori_loop(0, nci, idx_body, 0)

    def count(pred_fn):
        def body(c, acc):
            return acc + _colsum(jnp.where(pred_fn(c, keys_scr[c]), 1.0, 0.0))
        acc = lax.fori_loop(0, nci, body, jnp.zeros((8, CH), f32))
        return jnp.sum(acc, axis=0, keepdims=True)

    kf = float(topk)
    int_min = jnp.full((1, CH), -2 ** 31, i32)
    c0 = count(lambda c, k: k >= 0)
    cand = jnp.where(c0 >= kf, jnp.zeros((1, CH), i32), int_min)

    def bit_body(j, cand):
        trial = cand | jnp.left_shift(jnp.int32(1), 30 - j)
        cnt = count(lambda c, k: k >= trial)
        return jnp.where(cnt >= kf, trial, cand)

    cand = lax.fori_loop(0, 31, bit_body, cand)
    need = kf - count(lambda c, k: k > cand)

    def pos_body(j, pcut):
        trial = pcut | jnp.left_shift(jnp.int32(1), pos_bits - 1 - j)
        cnt = count(lambda c, k: (k == cand) & (c * CH + row < trial))
        return jnp.where(cnt < need, trial, pcut)

    pcut = lax.fori_loop(0, pos_bits, pos_body, jnp.zeros((1, CH), i32))

    def mask_body(c, carry):
        k = keys_scr[c]
        s_pos = c * CH + row
        sel = ((k > cand) | ((k == cand) & (s_pos <= pcut))) & (s_pos <= t_pos)
        mask_scr[c] = jnp.where(sel, 0.0, NEG_INF)
        return carry

    lax.fori_loop(0, nci, mask_body, 0)

    scale = HEAD_DIM ** -0.5
    for h in range(H_A):
        g = h // R_A
        qh = _head_operand(q_tile(h // 2), h % 2, g % 2, True)

        def score_fn(c, qh=qh, g=g, h=h):
            s = _dot_nt(get_k(c, g // 2).astype(bf16), qh) * scale
            bidx = jnp.clip(c - i_abs + 2, 0, 2)
            return s + toep_ref[h, bidx] + mask_scr[c]

        ot_scr[h * HEAD_DIM:(h + 1) * HEAD_DIM, :] = _softmax_pv(
            0, nci, score_fn, lambda c, g=g: get_vt(c, g), s_scr)
    return ot_scr[...].T


def _dsa_prompt_kernel(q_ref, iq_ref, misc_ref, ik_ref, k_ref, vt_ref, toep_ref, o_ref,
                       keys_scr, mask_scr, s_scr, ot_scr, *, topk, pos_bits):
    i = pl.program_id(1)

    def rows(c):
        return pl.ds(pl.multiple_of(c * CH, CH), CH)

    o_ref[0] = _dsa_core(
        i, i + 1, topk, pos_bits,
        lambda j: q_ref[0, :, j * CH:(j + 1) * CH],
        lambda j: iq_ref[0, :, j * CH:(j + 1) * CH],
        misc_ref[0, 0, 0:8, :],
        lambda c: ik_ref[0, rows(c), :],
        lambda c, j: k_ref[0, rows(c), j * CH:(j + 1) * CH],
        lambda c, g: vt_ref[0, c, g * HEAD_DIM:(g + 1) * HEAD_DIM, :],
        toep_ref, keys_scr, mask_scr, s_scr, ot_scr)


def _dsa_prompt(qa, iq, ikp, kva, ft, toep):
    n, t, _ = qa.shape
    nc = t // CH
    topk = min(DSA_TOPK, t // 4)
    kern = functools.partial(_dsa_prompt_kernel, topk=topk, pos_bits=max(1, (t - 1).bit_length()))
    return pl.pallas_call(
        kern,
        grid=(n, nc),
        in_specs=[pl.BlockSpec((1, CH, 512), lambda b, i: (b, i, 0)),
                  pl.BlockSpec((1, CH, 256), lambda b, i: (b, i, 0)),
                  pl.BlockSpec((1, 1, CH, CH), lambda b, i: (b, i, 4, 0)),
                  pl.BlockSpec((1, t, CH), lambda b, i: (b, 0, 0)),
                  pl.BlockSpec((1, t, 256), lambda b, i: (b, 0, 0)),
                  pl.BlockSpec((1, nc, 256, CH), lambda b, i: (b, 0, 0, 0)),
                  pl.BlockSpec((H_A, 3, CH, CH), lambda b, i: (0, 0, 0, 0))],
        out_specs=pl.BlockSpec((1, CH, 512), lambda b, i: (b, i, 0)),
        out_shape=jax.ShapeDtypeStruct((n, t, 512), f32),
        scratch_shapes=[pltpu.VMEM((nc, CH, CH), i32), pltpu.VMEM((nc, CH, CH), f32),
                        pltpu.VMEM((nc, CH, CH), f32), pltpu.VMEM((512, CH), f32)],
        compiler_params=_cparams(2),
    )(qa, iq, ft, ikp, kva, ft, toep)


def _pad_rows(ref, x):
    ref[...] = jnp.zeros(ref.shape, f32)
    ref[0:x.shape[0], :] = x


def _dsa_sample_kernel(pt_ref, q_ref, iq_ref, ft_ref, kvn_ref, ikn_ref, *refs, n_pages, topk, pos_bits):
    idx_pages = refs[:n_pages]
    kv_pages = refs[n_pages:2 * n_pages]
    toep_ref, o_ref, q_scr, iq_scr, iw_scr, ik_scr, k_scr, vt_scr, keys_scr, mask_scr, s_scr, ot_scr = refs[2 * n_pages:]
    t_new = q_ref.shape[1]
    for p in range(n_pages):
        ik_scr[p] = jnp.zeros((CH, CH), f32)
        ik_scr[p, :, 0:D_IDX] = idx_pages[p][0, 0]
        page = kv_pages[p][0, 0]
        k_scr[p] = page[:, 0:256]
        vt_scr[p] = page[:, 256:512].T
    _pad_rows(ik_scr.at[n_pages], ikn_ref[0])
    _pad_rows(k_scr.at[n_pages], kvn_ref[0][:, 0:256])
    vt_scr[n_pages] = jnp.zeros((256, CH), f32)
    vt_scr[n_pages, :, 0:t_new] = ft_ref[0, 0:256, :]
    _pad_rows(q_scr, q_ref[0])
    _pad_rows(iq_scr, iq_ref[0])
    iw_scr[...] = jnp.zeros((8, CH), f32)
    iw_scr[0:H_IDX, 0:t_new] = ft_ref[0, 512:512 + H_IDX, :]
    o = _dsa_core(
        n_pages, n_pages + 1, topk, pos_bits,
        lambda j: q_scr[:, j * CH:(j + 1) * CH],
        lambda j: iq_scr[:, j * CH:(j + 1) * CH],
        iw_scr[...],
        lambda c: ik_scr[c],
        lambda c, j: k_scr[c, :, j * CH:(j + 1) * CH],
        lambda c, g: vt_scr[c, g * HEAD_DIM:(g + 1) * HEAD_DIM, :],
        toep_ref, keys_scr, mask_scr, s_scr, ot_scr)
    o_ref[0] = o[0:t_new]


def _page_specs(n_pages, layer, width):
    return [pl.BlockSpec((1, 1, PAGE_SIZE, width), lambda b, pt, p=p: (pt[b, p], layer, 0, 0)) for p in range(n_pages)]


def _dsa_sample(page_table, layer, qa, iq, ft, kva, ikp, idx_pool, kv_pool, toep):
    n, t_new, _ = qa.shape
    n_pages = page_table.shape[1]
    nc = n_pages + 1
    lk = n_pages * PAGE_SIZE + t_new
    topk = min(DSA_TOPK, lk // 4)
    kern = functools.partial(_dsa_sample_kernel, n_pages=n_pages, topk=topk, pos_bits=(nc * CH - 1).bit_length())
    seq = lambda w: pl.BlockSpec((1, t_new, w), lambda b, pt: (b, 0, 0))
    grid_spec = pltpu.PrefetchScalarGridSpec(
        num_scalar_prefetch=1, grid=(n,),
        in_specs=[seq(512), seq(256), pl.BlockSpec((1, FT_ROWS, t_new), lambda b, pt: (b, 0, 0)), seq(512), seq(CH)]
        + _page_specs(n_pages, layer, D_IDX) + _page_specs(n_pages, layer, 512)
        + [pl.BlockSpec((H_A, 3, CH, CH), lambda b, pt: (0, 0, 0, 0))],
        out_specs=pl.BlockSpec((1, t_new, 512), lambda b, pt: (b, 0, 0)),
        scratch_shapes=[pltpu.VMEM((CH, 512), f32), pltpu.VMEM((CH, 256), f32), pltpu.VMEM((8, CH), f32),
                        pltpu.VMEM((nc, CH, CH), f32), pltpu.VMEM((nc, CH, 256), f32), pltpu.VMEM((nc, 256, CH), f32),
                        pltpu.VMEM((nc, CH, CH), i32), pltpu.VMEM((nc, CH, CH), f32), pltpu.VMEM((nc, CH, CH), f32),
                        pltpu.VMEM((512, CH), f32)])
    return pl.pallas_call(
        kern, grid_spec=grid_spec, out_shape=jax.ShapeDtypeStruct((n, t_new, 512), f32),
        compiler_params=_cparams(1),
    )(page_table, qa, iq, ft, kva, ikp, *([idx_pool] * n_pages), *([kv_pool] * n_pages), toep)


def _compress_core(xk_ref, xv_ref, wbd_ref, pe_ref, kc_ref, kct_ref):
    nr = xk_ref.shape[0] // COMP_STRIDE
    halves = []
    for s, x_ref in enumerate((xk_ref, xv_ref)):
        a0 = jnp.zeros((nr, CH), f32)
        a1 = jnp.zeros((nr, CH), f32)
        for r in range(COMP_STRIDE):
            xr = x_ref[pl.ds(r, nr, stride=COMP_STRIDE), :]
            r2 = COMP_STRIDE + r
            a0 = a0 + _dot((xr + pe_ref[s, r:r + 1, :]).astype(bf16), wbd_ref[s, r])
            a1 = a1 + _dot((xr + pe_ref[s, r2:r2 + 1, :]).astype(bf16), wbd_ref[s, r2])
        kc = a0 + pltpu.roll(a1, nr - 1, 0)
        if nr < CH:
            kc = jnp.concatenate([kc, jnp.zeros((CH - nr, CH), f32)], axis=0)
        halves.append(kc)
    kc_ref[0, :, 0:CH] = halves[0]
    kc_ref[0, :, CH:2 * CH] = halves[1]
    kct_ref[0] = halves[1].T


def _compress_prompt_kernel(xk_ref, xv_ref, wbd_ref, pe_ref, kc_ref, kct_ref):
    _compress_core(xk_ref.at[0], xv_ref.at[0], wbd_ref, pe_ref, kc_ref, kct_ref)


def _compress_sample_kernel(pt_ref, *refs, n_pages):
    pages = refs[:n_pages]
    wbd_ref, pe_ref, kc_ref, kct_ref, xk_scr, xv_scr = refs[n_pages:]
    for p in range(n_pages):
        page = pages[p][0, 0]
        xk_scr[p * PAGE_SIZE:(p + 1) * PAGE_SIZE, :] = page[:, 0:CH]
        xv_scr[p * PAGE_SIZE:(p + 1) * PAGE_SIZE, :] = page[:, CH:2 * CH]
    _compress_core(xk_scr, xv_scr, wbd_ref, pe_ref, kc_ref, kct_ref)


def _compress_out(n):
    return ([pl.BlockSpec((1, CH, 256), lambda b, *_: (b, 0, 0)), pl.BlockSpec((1, CH, CH), lambda b, *_: (b, 0, 0))],
            [jax.ShapeDtypeStruct((n, CH, 256), f32), jax.ShapeDtypeStruct((n, CH, CH), f32)])


def _compress_prompt(kvc4, wbd, pe256):
    n, t, _ = kvc4.shape
    assert t % (8 * COMP_STRIDE) == 0 and t <= COMP_STRIDE * CH
    out_specs, out_shape = _compress_out(n)
    return pl.pallas_call(
        _compress_prompt_kernel, grid=(n,),
        in_specs=[pl.BlockSpec((1, t, CH), lambda b: (b, 0, 0)), pl.BlockSpec((1, t, CH), lambda b: (b, 0, 1)),
                  _const_spec(wbd), _const_spec(pe256)],
        out_specs=out_specs, out_shape=out_shape, compiler_params=_cparams(1),
    )(kvc4, kvc4, wbd, pe256)


def _compress_sample(page_table, layer, nsa_pool, wbd, pe256):
    n, n_pages = page_table.shape
    assert n_pages * PAGE_SIZE <= COMP_STRIDE * CH
    out_specs, out_shape = _compress_out(n)
    grid_spec = pltpu.PrefetchScalarGridSpec(
        num_scalar_prefetch=1, grid=(n,),
        in_specs=_page_specs(n_pages, layer, 512) + [_const_spec(wbd), _const_spec(pe256)],
        out_specs=out_specs,
        scratch_shapes=[pltpu.VMEM((n_pages * PAGE_SIZE, CH), f32), pltpu.VMEM((n_pages * PAGE_SIZE, CH), f32)])
    return pl.pallas_call(
        functools.partial(_compress_sample_kernel, n_pages=n_pages), grid_spec=grid_spec, out_shape=out_shape,
        compiler_params=_cparams(1),
    )(page_table, *([nsa_pool] * n_pages), wbd, pe256)


def _split3(x):
    hi = x.astype(bf16)
    r1 = x - hi.astype(f32)
    mid = r1.astype(bf16)
    lo = (r1 - mid.astype(f32)).astype(bf16)
    return hi, mid, lo


def _nsa_core(i_abs, n_sel, win_c0, q_tile, gate_t, kc_ref, kct_ref, cmpb_ref, cover_ref, get_selk, get_selvt,
              get_wink, get_winvt, toep_ref, selm_scr, mask_scr, s_scr, ot_scr):
    row = lax.broadcasted_iota(i32, (CH, CH), 0)
    lane = lax.broadcasted_iota(i32, (CH, CH), 1)
    t_pos = i_abs * CH + lane
    scale = HEAD_DIM ** -0.5
    nci = i_abs + 1
    n_chunks_static = mask_scr.shape[1]

    qh = []
    for h in range(H_C):
        g = h // R_C
        qh.append(_head_operand(q_tile(h // 2), h % 2, g, True))

    kc_b = kc_ref[0, :, 0:CH].astype(bf16)
    cmp_valid = t_pos >= row * COMP_STRIDE + COMP_BLOCK - 1
    o_cmp = []
    p_sum = [jnp.zeros((CH, CH), f32) for _ in range(G_C)]
    for h in range(H_C):
        g = h // R_C
        s = _dot_nt(kc_b, qh[h]) * scale + cmpb_ref[h, 0]
        s = jnp.where(cmp_valid, s, NEG_INF)
        m = jnp.max(s, axis=0, keepdims=True)
        m = jnp.where(m > NEG_INF, m, 0.0)
        e = jnp.exp(s - m)
        p = e / jnp.maximum(jnp.sum(e, axis=0, keepdims=True), 1e-30)
        p_sum[g] = p_sum[g] + p
        o_cmp.append(_dot(kct_ref[0, g * HEAD_DIM:(g + 1) * HEAD_DIM, :].astype(bf16), p.astype(bf16)))

    srow = lax.broadcasted_iota(i32, (CH, CH), 0)
    cur = jnp.right_shift(t_pos, SEL_BLOCK.bit_length() - 1)
    adm = srow <= cur
    forced = (srow == 0) | (srow == cur) | (srow == cur - 1)
    cov = cover_ref[...]
    for g in range(G_C):
        hi, mid, lo = _split3(p_sum[g])
        imp = _dot(cov, hi) + _dot(cov, mid) + _dot(cov, lo)
        score = jnp.where(adm, imp + jnp.where(forced, FORCE_SCORE, 0.0), NEG_INF)
        rank = jnp.zeros((CH, CH), f32)
        for s2 in range(n_sel):
            other = score[s2:s2 + 1, :]
            ahead = (other > score) | ((other == score) & (s2 < srow))
            rank = rank + jnp.where(ahead, 1.0, 0.0)
        selm_scr[g] = jnp.where(adm & (rank < float(SEL_TOPK)) & (srow < n_sel), 1.0, 0.0)
        for c in range(n_chunks_static):
            blk0 = selm_scr[g, 2 * c:2 * c + 1, :]
            blk1 = selm_scr[g, 2 * c + 1:2 * c + 2, :]
            on = jnp.where(row < SEL_BLOCK, blk0, blk1) > 0.5
            mask_scr[g, c] = jnp.where(on & (c * CH + row <= t_pos), 0.0, NEG_INF)

    win_lo = jnp.maximum(i_abs - WINDOW // CH, win_c0)
    for h in range(H_C):
        g = h // R_C

        def slc_score(c, h=h, g=g):
            s = _dot_nt(get_selk(c).astype(bf16), qh[h]) * scale
            bidx = jnp.clip(c - i_abs + 2, 0, 2)
            return s + toep_ref[h, bidx] + mask_scr[g, c]

        o_slc = _softmax_pv(0, nci, slc_score, lambda c, g=g: get_selvt(c, g), s_scr)

        def win_score(c, h=h):
            s = _dot_nt(get_wink(c).astype(bf16), qh[h]) * scale
            bidx = jnp.clip(c - i_abs + 2, 0, 2)
            wd = t_pos - (c * CH + row)
            return jnp.where((wd >= 0) & (wd <= WINDOW), s + toep_ref[h, bidx], NEG_INF)

        o_win = _softmax_pv(win_lo, nci, win_score, lambda c, g=g: get_winvt(c, g), s_scr)
        g0 = jax.nn.sigmoid(gate_t[8 + h:9 + h, :])
        g1 = jax.nn.sigmoid(gate_t[16 + h:17 + h, :])
        g2 = jax.nn.sigmoid(gate_t[24 + h:25 + h, :])
        ot_scr[h * HEAD_DIM:(h + 1) * HEAD_DIM, :] = g0 * o_cmp[h] + g1 * o_slc + g2 * o_win
    return ot_scr[...].T


def _nsa_prompt_kernel(q_ref, misc_ref, kc_ref, kct_ref, cmpb_ref, cover_ref, selk_ref, selvt_ref, wink_ref, winvt_ref,
                       toep_ref, o_ref, selm_scr, mask_scr, s_scr, ot_scr, *, n_sel):
    i = pl.program_id(1)

    def rows(c):
        return pl.ds(pl.multiple_of(c * CH, CH), CH)

    o_ref[0] = _nsa_core(
        i, n_sel, 0,
        lambda j: q_ref[0, :, j * CH:(j + 1) * CH],
        misc_ref[0, 0, 0:32, :],
        kc_ref, kct_ref, cmpb_ref, cover_ref,
        lambda c: selk_ref[0, rows(c), :],
        lambda c, g: selvt_ref[0, c, g * HEAD_DIM:(g + 1) * HEAD_DIM, :],
        lambda c: wink_ref[0, rows(c), :],
        lambda c, g: winvt_ref[0, c, g * HEAD_DIM:(g + 1) * HEAD_DIM, :],
        toep_ref, selm_scr, mask_scr, s_scr, ot_scr)


def _cover_matrix(n_keys):
    n_cmp = (n_keys - COMP_BLOCK) // COMP_STRIDE + 1
    n_sel = -(-n_keys // SEL_BLOCK)
    c0 = np.arange(n_cmp)[:, None] * COMP_STRIDE
    s0 = np.arange(n_sel)[None, :] * SEL_BLOCK
    cover = ((c0 < s0 + SEL_BLOCK) & (c0 + COMP_BLOCK > s0)).astype(np.float32)
    out = np.zeros((CH, CH), np.float32)
    out[:n_sel, :n_cmp] = cover.T
    return jnp.asarray(out, dtype=bf16), n_sel


def _nsa_prompt(qc, ft, kc, kct, cmpb, kvc4, kvw, toep):
    n, t, _ = qc.shape
    nc = t // CH
    cover, n_sel = _cover_matrix(t)
    return pl.pallas_call(
        functools.partial(_nsa_prompt_kernel, n_sel=n_sel),
        grid=(n, nc),
        in_specs=[pl.BlockSpec((1, CH, 512), lambda b, i: (b, i, 0)),
                  pl.BlockSpec((1, 1, CH, CH), lambda b, i: (b, i, 4, 0)),
                  pl.BlockSpec((1, CH, 256), lambda b, i: (b, 0, 0)),
                  pl.BlockSpec((1, CH, CH), lambda b, i: (b, 0, 0)),
                  pl.BlockSpec((H_C, 1, CH, CH), lambda b, i: (0, i, 0, 0)),
                  pl.BlockSpec((CH, CH), lambda b, i: (0, 0)),
                  pl.BlockSpec((1, t, CH), lambda b, i: (b, 0, 2)),
                  pl.BlockSpec((1, nc, CH, CH), lambda b, i: (b, 0, 2, 0)),
                  pl.BlockSpec((1, t, CH), lambda b, i: (b, 0, 0)),
                  pl.BlockSpec((1, nc, CH, CH), lambda b, i: (b, 0, 3, 0)),
                  pl.BlockSpec((H_C, 3, CH, CH), lambda b, i: (1, 0, 0, 0))],
        out_specs=pl.BlockSpec((1, CH, 512), lambda b, i: (b, i, 0)),
        out_shape=jax.ShapeDtypeStruct((n, t, 512), f32),
        scratch_shapes=[pltpu.VMEM((G_C, CH, CH), f32), pltpu.VMEM((G_C, nc, CH, CH), f32),
                        pltpu.VMEM((nc, CH, CH), f32), pltpu.VMEM((512, CH), f32)],
        compiler_params=_cparams(2),
    )(qc, ft, kc, kct, cmpb, cover, kvc4, ft, kvw, ft, toep)


def _nsa_sample_kernel(pt_ref, q_ref, ft_ref, kvn_ref, kvwn_ref, win_ref, kc_ref, kct_ref, cmpb_ref, cover_ref, *refs,
                       n_pages, n_sel, n_win):
    pages = refs[:n_pages]
    (toep_ref, o_ref, q_scr, gate_scr, selk_scr, selvt_scr, wink_scr, winvt_scr,
     selm_scr, mask_scr, s_scr, ot_scr) = refs[n_pages:]
    t_new = q_ref.shape[1]
    for p in range(n_pages):
        page = pages[p][0, 0]
        selk_scr[p] = page[:, 256:384]
        selvt_scr[p] = page[:, 384:512].T
    _pad_rows(selk_scr.at[n_pages], kvn_ref[0][:, 256:384])
    selvt_scr[n_pages] = jnp.zeros((CH, CH), f32)
    selvt_scr[n_pages, :, 0:t_new] = ft_ref[0, 256:384, :]
    for w in range(n_win):
        blk = win_ref[0, w * CH:(w + 1) * CH, :]
        wink_scr[w] = blk[:, 0:CH]
        winvt_scr[w] = blk[:, CH:2 * CH].T
    _pad_rows(wink_scr.at[n_win], kvwn_ref[0][:, 0:CH])
    winvt_scr[n_win] = jnp.zeros((CH, CH), f32)
    winvt_scr[n_win, :, 0:t_new] = ft_ref[0, 384:512, :]
    _pad_rows(q_scr, q_ref[0])
    gate_scr[...] = jnp.zeros((32, CH), f32)
    gate_scr[:, 0:t_new] = ft_ref[0, 512:544, :]
    win_c0 = n_pages - n_win
    o = _nsa_core(
        n_pages, n_sel, win_c0,
        lambda j: q_scr[:, j * CH:(j + 1) * CH],
        gate_scr[...],
        kc_ref, kct_ref, cmpb_ref, cover_ref,
        lambda c: selk_scr[c],
        lambda c, g: selvt_scr[c, g * HEAD_DIM:(g + 1) * HEAD_DIM, :],
        lambda c: wink_scr[c - win_c0],
        lambda c, g: winvt_scr[c - win_c0, g * HEAD_DIM:(g + 1) * HEAD_DIM, :],
        toep_ref, selm_scr, mask_scr, s_scr, ot_scr)
    o_ref[0] = o[0:t_new]


def _nsa_sample(page_table, layer, qc, ft, kvc4, kvw, win_buf, kc, kct, cmpb, nsa_pool, toep):
    n, t_new, _ = qc.shape
    n_pages = page_table.shape[1]
    nc = n_pages + 1
    w_buf = win_buf.shape[1]
    assert w_buf % CH == 0 and w_buf <= n_pages * PAGE_SIZE
    n_win = w_buf // CH
    cover, n_sel = _cover_matrix(n_pages * PAGE_SIZE + t_new)
    nqb = cmpb.shape[1]
    seq = lambda w: pl.BlockSpec((1, t_new, w), lambda b, pt: (b, 0, 0))
    grid_spec = pltpu.PrefetchScalarGridSpec(
        num_scalar_prefetch=1, grid=(n,),
        in_specs=[seq(512), pl.BlockSpec((1, FT_ROWS, t_new), lambda b, pt: (b, 0, 0)), seq(512), seq(256),
                  pl.BlockSpec((1, w_buf, 256), lambda b, pt: (b, 0, 0)),
                  pl.BlockSpec((1, CH, 256), lambda b, pt: (b, 0, 0)),
                  pl.BlockSpec((1, CH, CH), lambda b, pt: (b, 0, 0)),
                  pl.BlockSpec((H_C, 1, CH, CH), lambda b, pt: (0, nqb - 1, 0, 0)),
                  pl.BlockSpec((CH, CH), lambda b, pt: (0, 0))]
        + _page_specs(n_pages, layer, 512)
        + [pl.BlockSpec((H_C, 3, CH, CH), lambda b, pt: (1, 0, 0, 0))],
        out_specs=pl.BlockSpec((1, t_new, 512), lambda b, pt: (b, 0, 0)),
        scratch_shapes=[pltpu.VMEM((CH, 512), f32), pltpu.VMEM((32, CH), f32),
                        pltpu.VMEM((nc, CH, CH), f32), pltpu.VMEM((nc, CH, CH), f32),
                        pltpu.VMEM((n_win + 1, CH, CH), f32), pltpu.VMEM((n_win + 1, CH, CH), f32),
                        pltpu.VMEM((G_C, CH, CH), f32), pltpu.VMEM((G_C, nc, CH, CH), f32),
                        pltpu.VMEM((nc, CH, CH), f32), pltpu.VMEM((512, CH), f32)])
    return pl.pallas_call(
        functools.partial(_nsa_sample_kernel, n_pages=n_pages, n_sel=n_sel, n_win=n_win),
        grid_spec=grid_spec, out_shape=jax.ShapeDtypeStruct((n, t_new, 512), f32),
        compiler_params=_cparams(1),
    )(page_table, qc, ft, kvc4, kvw, win_buf, kc, kct, cmpb, cover, *([nsa_pool] * n_pages), toep)


def _conv_kernel(glu_ref, hist_ref, cw_ref, cb_ref, lg_ref, lb_ref, wob_ref, ob_ref, rows_ref, ext_scr, *, tm, stride):
    hp = ext_scr.shape[0] - tm
    keep = (CONV_W - 1) * stride
    i = pl.program_id(1)

    @pl.when(i == 0)
    def _():
        ext_scr[0:hp, :] = hist_ref[0]

    glu = glu_ref[0]
    ext_scr[hp:hp + tm, :] = glu[:, 0:D_CONV] * jax.nn.sigmoid(glu[:, D_CONV:2 * D_CONV])
    y = jnp.zeros((tm, D_CONV), f32) + cb_ref[...]
    for w in range(CONV_W):
        y = y + ext_scr[hp - keep + w * stride:hp - keep + w * stride + tm, :] * cw_ref[w:w + 1, :]
    mu = jnp.mean(y, axis=-1, keepdims=True)
    var = jnp.mean(jnp.square(y - mu), axis=-1, keepdims=True)
    yn = (y - mu) * lax.rsqrt(var + EPS) * lg_ref[...] + lb_ref[...]
    act = (yn * jax.nn.sigmoid(yn)).astype(bf16)
    ob_ref[0] = _dot(act, wob_ref[...])
    tail = ext_scr[hp + tm - keep:hp + tm, :]
    rows_ref[0] = tail
    ext_scr[hp - keep:hp, :] = tail


def _conv_module(glu, hist, cw, cb, lg, lb, wob, tm, stride):
    nb, t, _ = glu.shape
    hp = hist.shape[1]
    keep = (CONV_W - 1) * stride
    d = wob.shape[1]
    return pl.pallas_call(
        functools.partial(_conv_kernel, tm=tm, stride=stride),
        grid=(nb, t // tm),
        in_specs=[pl.BlockSpec((1, tm, 2 * D_CONV), lambda n, i: (n, i, 0)),
                  pl.BlockSpec((1, hp, D_CONV), lambda n, i: (n, 0, 0)),
                  _const_spec(cw), _const_spec(cb), _const_spec(lg), _const_spec(lb), _const_spec(wob)],
        out_specs=[pl.BlockSpec((1, tm, d), lambda n, i: (n, i, 0)),
                   pl.BlockSpec((1, keep, D_CONV), lambda n, i: (n, 0, 0))],
        out_shape=[jax.ShapeDtypeStruct((nb, t, d), f32), jax.ShapeDtypeStruct((nb, keep, D_CONV), f32)],
        scratch_shapes=[pltpu.VMEM((hp + tm, D_CONV), f32)],
        compiler_params=_cparams(2),
    )(glu, hist, cw, cb, lg, lb, wob)


def _merge_kernel(x_ref, gt_ref, oa_ref, ob_ref, oc_ref, gm_ref, woa_ref, woc_ref, wo_ref, o_ref):
    d = x_ref.shape[2]
    ya = _dot(oa_ref[0].astype(bf16), woa_ref[...])
    yc = _dot(oc_ref[0].astype(bf16), woc_ref[...])
    gm = gm_ref[0]
    merged = (jax.nn.sigmoid(gm[:, 0:d]) * ya + jax.nn.sigmoid(gm[:, d:2 * d]) * ob_ref[0]
              + jax.nn.sigmoid(gm[:, 2 * d:3 * d]) * yc)
    o_ref[0] = x_ref[0] + gt_ref[0] * _dot(merged.astype(bf16), wo_ref[...])


def _merge(x, gt, oa, ob, oc, gm, woa, woc, wo, tm):
    nb, t, d = x.shape
    tok = lambda w: pl.BlockSpec((1, tm, w), lambda n, i: (n, i, 0))
    return pl.pallas_call(
        _merge_kernel, grid=(nb, t // tm),
        in_specs=[tok(d), _mod_spec(gt, tm), tok(512), tok(d), tok(512), tok(3 * d),
                  _const_spec(woa), _const_spec(woc), _const_spec(wo)],
        out_specs=tok(d), out_shape=jax.ShapeDtypeStruct((nb, t, d), f32),
        compiler_params=_cparams(2),
    )(x, gt, oa, ob, oc, gm, woa, woc, wo)


def _ffn_kernel(x_ref, sh_ref, sc_ref, gt_ref, g_ref, hist_ref, wa_ref, wb_ref, cw_ref, cb_ref, wd_ref,
                o_ref, rows_ref, h_scr, acc_scr, ext_scr, carry_scr, *, tm, stride):
    hp = ext_scr.shape[0] - tm
    i = pl.program_id(1)
    j = pl.program_id(2)

    @pl.when(j == 0)
    def _():
        x = x_ref[0]
        y = x * lax.rsqrt(jnp.mean(x * x, axis=-1, keepdims=True) + EPS) * g_ref[...]
        h_scr[...] = (y * (1.0 + sc_ref[0]) + sh_ref[0]).astype(bf16)
        acc_scr[...] = jnp.zeros(acc_scr.shape, f32)

    @pl.when(i == 0)
    def _():
        ext_scr[0:hp, :] = hist_ref[...]

    @pl.when(i > 0)
    def _():
        ext_scr[0:hp, :] = carry_scr[j]

    hb = h_scr[...]
    a = _dot(hb, wa_ref[...])
    b = _dot(hb, wb_ref[...])
    ext_scr[hp:hp + tm, :] = a
    conv = (ext_scr[hp - 2 * stride:hp - 2 * stride + tm, :] * cw_ref[0:1, :]
            + ext_scr[hp - stride:hp - stride + tm, :] * cw_ref[1:2, :]
            + a * cw_ref[2:3, :] + cb_ref[...])
    act = (jax.nn.gelu(conv, approximate=True) * b).astype(bf16)
    acc_scr[...] += _dot(act, wd_ref[...])
    tail = ext_scr[tm:tm + hp, :]
    carry_scr[j] = tail
    rows_ref[0, 0] = tail

    @pl.when(j == pl.num_programs(2) - 1)
    def _():
        o_ref[0] = x_ref[0] + gt_ref[0] * acc_scr[...]


def _ffn(x, sh, sc, gt, g, hist, wa, wb, cw, cb, wd, tm, tf, stride):
    nb, t, d = x.shape
    dff = wa.shape[1]
    hp = hist.shape[0]
    nj = dff // tf
    tok = pl.BlockSpec((1, tm, d), lambda n, i, j: (n, i, 0))

    def mod_spec(m):
        if m.shape[1] == 1:
            return pl.BlockSpec((1, 1, d), lambda n, i, j: (n, 0, 0))
        return pl.BlockSpec((1, tm, d), lambda n, i, j: (n, i, 0))

    return pl.pallas_call(
        functools.partial(_ffn_kernel, tm=tm, stride=stride),
        grid=(nb, t // tm, nj),
        in_specs=[tok, mod_spec(sh), mod_spec(sc), mod_spec(gt),
                  pl.BlockSpec((1, d), lambda n, i, j: (0, 0)),
                  pl.BlockSpec((hp, tf), lambda n, i, j: (0, j)),
                  pl.BlockSpec((d, tf), lambda n, i, j: (0, j)),
                  pl.BlockSpec((d, tf), lambda n, i, j: (0, j)),
                  pl.BlockSpec((FFN_CONV_W, tf), lambda n, i, j: (0, j)),
                  pl.BlockSpec((1, tf), lambda n, i, j: (0, j)),
                  pl.BlockSpec((tf, d), lambda n, i, j: (j, 0))],
        out_specs=[tok, pl.BlockSpec((1, 1, hp, tf), lambda n, i, j: (n, i, 0, j))],
        out_shape=[jax.ShapeDtypeStruct((nb, t, d), f32), jax.ShapeDtypeStruct((nb, t // tm, hp, dff), f32)],
        scratch_shapes=[pltpu.VMEM((tm, d), bf16), pltpu.VMEM((tm, d), f32), pltpu.VMEM((hp + tm, tf), f32),
                        pltpu.VMEM((nj, hp, tf), f32)],
        compiler_params=_cparams(3),
    )(x, sh, sc, gt, g, hist, wa, wb, cw, cb, wd)


def _final_kernel(x_ref, g_ref, o_ref):
    x = x_ref[0]
    o_ref[0] = x * lax.rsqrt(jnp.mean(x * x, axis=-1, keepdims=True) + EPS) * g_ref[...]


def _final_norm(x, g, tm):
    nb, t, d = x.shape
    tok = pl.BlockSpec((1, tm, d), lambda n, i: (n, i, 0))
    return pl.pallas_call(
        _final_kernel, grid=(nb, t // tm), in_specs=[tok, pl.BlockSpec((1, d), lambda n, i: (0, 0))],
        out_specs=tok, out_shape=jax.ShapeDtypeStruct((nb, t, d), f32), compiler_params=_cparams(2),
    )(x, g)


def _layer_weights(w_in, w_cmp, pe_cmp):
    d = w_in.shape[0]
    offs = np.cumsum([0, H_A * HEAD_DIM, G_A * HEAD_DIM, G_A * HEAD_DIM, H_IDX * D_IDX, H_IDX, D_IDX, 2 * D_CONV,
                      H_C * HEAD_DIM, 6 * G_C * HEAD_DIM, 3 * H_C, 3 * d])
    qa, ka, va, iq, iw, ik, glu, qc, kvc, gc, gm = [w_in[:, offs[k]:offs[k + 1]] for k in range(11)]
    ikp = jnp.pad(ik, ((0, 0), (0, CH - D_IDX)))
    ws = [qa, jnp.concatenate([ka, va], axis=1), iq, ikp, glu, qc, kvc[:, 0:512], kvc[:, 512:768], gm]
    ws = [w.astype(bf16) for w in ws]
    misc = jnp.concatenate([iw, jnp.zeros((d, 8 - H_IDX), f32), gc, jnp.zeros((d, CH - 8 - 3 * H_C), f32)], axis=1)
    wft = jnp.concatenate([va, kvc[:, 384:512], kvc[:, 640:768], misc], axis=1).T.astype(bf16)
    wbd = jnp.zeros((2, COMP_BLOCK, CH, CH), f32)
    for s in range(2):
        blk = w_cmp[s].reshape(COMP_BLOCK, HEAD_DIM, HEAD_DIM)
        for g in range(G_C):
            wbd = wbd.at[s, :, g * HEAD_DIM:(g + 1) * HEAD_DIM, g * HEAD_DIM:(g + 1) * HEAD_DIM].set(blk)
    pe256 = jnp.concatenate([pe_cmp] * G_C, axis=2)
    return ws, wft, wbd.astype(bf16), pe256


def kernel(x_prompt, x_sample, c_prompt, c_sample, cache_dsa_kv, cache_dsa_idx, cache_nsa_kv, state_nsa_win,
           state_conv, state_ffn, page_table, rel_bias, w_mod, b_mod, g_mix, g_ffn, w_in, w_cmp, pe_cmp, conv_w,
           conv_b, ln_g, ln_b, w_oa, w_ob, w_oc, w_o, w_up, ffn_conv_w, ffn_conv_b, w_down, g_final):
    n_p, t_p, d = x_prompt.shape
    n_s, t_s, _ = x_sample.shape
    depth = w_mod.shape[0]
    dff = w_down.shape[1]
    n_pool = cache_dsa_kv.shape[0]
    n_pages = page_table.shape[1]
    past = n_pages * PAGE_SIZE
    assert t_p % CH == 0 and t_s < COMP_STRIDE and (n_s * t_s) % CH == 0 and n_s % 8 == 0
    tm_p = 256
    rows_s = n_s * t_s

    mod = _modulation(jnp.concatenate([c_prompt, c_sample], axis=0), w_mod, b_mod)
    toep = _toeplitz_bias(rel_bias)
    nqb = past // CH + 1
    cmpb = _cmp_bias(rel_bias, max(nqb, t_p // CH))

    idx_pool = cache_dsa_idx
    kv_pool = cache_dsa_kv.reshape(n_pool, depth, PAGE_SIZE, 512)
    nsa_pool = cache_nsa_kv.reshape(n_pool, depth, PAGE_SIZE, 512)

    xp = x_prompt
    xs = jnp.transpose(x_sample, (1, 0, 2)).reshape(1, rows_s, d)
    zeros_conv = jnp.zeros((n_p, 32, D_CONV), f32)
    zeros_ffn = jnp.zeros((8, dff), f32)
    tf = dff // 2

    def to_seq(a):
        return jnp.transpose(a.reshape(t_s, n_s, a.shape[-1]), (1, 0, 2))

    def to_rows(a):
        return jnp.transpose(a, (1, 0, 2)).reshape(1, t_s * n_s, a.shape[-1])

    outs_p = [[] for _ in range(6)]
    outs_s = [[] for _ in range(6)]
    for l in range(depth):
        ws, wft, wbd, pe256 = _layer_weights(w_in[l], w_cmp[l], pe_cmp[l])
        m = [mod[l][:, k * d:(k + 1) * d] for k in range(6)]
        mp = [a[:n_p].reshape(n_p, 1, d) for a in m]
        ms = [jnp.tile(a[n_p:], (t_s, 1)).reshape(1, rows_s, d) for a in m]
        g1 = g_mix[l].reshape(1, d)
        g2 = g_ffn[l].reshape(1, d)
        cw, cb = conv_w[l], conv_b[l].reshape(1, D_CONV)
        lg, lb = ln_g[l].reshape(1, D_CONV), ln_b[l].reshape(1, D_CONV)
        woa, wob, woc, wo = [w[l].astype(bf16) for w in (w_oa, w_ob, w_oc, w_o)]
        wa, wb = w_up[l][:, :dff].astype(bf16), w_up[l][:, dff:].astype(bf16)
        wd = w_down[l].astype(bf16)
        fcw, fcb = ffn_conv_w[l], ffn_conv_b[l].reshape(1, dff)

        qa, kva, iq, ikp, glu, qc, kvc4, kvw, gm, ft = _inproj(xp, mp[0], mp[1], g1, ws, wft, tm_p)
        o_a = _dsa_prompt(qa, iq, ikp, kva, ft, toep)
        kc, kct = _compress_prompt(kvc4, wbd, pe256)
        o_c = _nsa_prompt(qc, ft, kc, kct, cmpb, kvc4, kvw, toep)
        o_b, conv_rows = _conv_module(glu, zeros_conv, cw, cb, lg, lb, wob, tm_p, 1)
        xp = _merge(xp, mp[2], o_a, o_b, o_c, gm, woa, woc, wo, tm_p)
        xp, ffn_rows = _ffn(xp, mp[3], mp[4], mp[5], g2, zeros_ffn, wa, wb, fcw, fcb, wd, 512, tf, 1)
        outs_p[0].append(kva.reshape(n_p, t_p, 2, G_A, HEAD_DIM))
        outs_p[1].append(ikp[:, :, :D_IDX])
        outs_p[2].append(kvc4.reshape(n_p, t_p, 4, G_C, HEAD_DIM))
        outs_p[3].append(kvw[:, t_p - min(WINDOW, t_p):].reshape(n_p, min(WINDOW, t_p), 2, G_C, HEAD_DIM))
        outs_p[4].append(conv_rows)
        outs_p[5].append(ffn_rows[:, -1, 8 - (FFN_CONV_W - 1):])

        qa, kva, iq, ikp, glu, qc, kvc4, kvw, gm, ft = _inproj(xs, ms[0], ms[1], g1, ws, wft, rows_s)
        ft_s = jnp.transpose(jnp.transpose(ft[0], (1, 0, 2)).reshape(FT_ROWS, t_s, n_s), (2, 0, 1))
        qa_s, kva_s, iq_s, ikp_s, qc_s, kvc4_s, kvw_s = [to_seq(a) for a in (qa, kva, iq, ikp, qc, kvc4, kvw)]
        o_a = _dsa_sample(page_table, l, qa_s, iq_s, ft_s, kva_s, ikp_s, idx_pool, kv_pool, toep)
        kc, kct = _compress_sample(page_table, l, nsa_pool, wbd, pe256)
        win_buf = state_nsa_win[:, l].reshape(n_s, -1, 256)
        o_c = _nsa_sample(page_table, l, qc_s, ft_s, kvc4_s, kvw_s, win_buf, kc, kct, cmpb, nsa_pool, toep)
        hist_c = jnp.transpose(state_conv[:, l], (1, 0, 2)).reshape(1, (CONV_W - 1) * n_s, D_CONV)
        o_b, conv_rows = _conv_module(glu, hist_c, cw, cb, lg, lb, wob, rows_s, n_s)
        xs = _merge(xs, ms[2], to_rows(o_a), o_b, to_rows(o_c), gm, woa, woc, wo, rows_s)
        hist_f = jnp.transpose(state_ffn[:, l], (1, 0, 2)).reshape((FFN_CONV_W - 1) * n_s, dff)
        xs, ffn_rows = _ffn(xs, ms[3], ms[4], ms[5], g2, hist_f, wa, wb, fcw, fcb, wd, rows_s, tf, n_s)
        outs_s[0].append(kva_s.reshape(n_s, t_s, 2, G_A, HEAD_DIM))
        outs_s[1].append(ikp_s[:, :, :D_IDX])
        outs_s[2].append(kvc4_s.reshape(n_s, t_s, 4, G_C, HEAD_DIM))
        win_all = jnp.concatenate([win_buf, kvw_s], axis=1)
        keep_w = min(WINDOW, past + t_s)
        outs_s[3].append(win_all[:, win_all.shape[1] - keep_w:].reshape(n_s, keep_w, 2, G_C, HEAD_DIM))
        outs_s[4].append(jnp.transpose(conv_rows[0].reshape(CONV_W - 1, n_s, D_CONV), (1, 0, 2)))
        outs_s[5].append(jnp.transpose(ffn_rows[0, 0].reshape(FFN_CONV_W - 1, n_s, dff), (1, 0, 2)))

    y_p = _final_norm(xp, g_final.reshape(1, d), 512)
    y_s = to_seq(_final_norm(xs, g_final.reshape(1, d), rows_s))
    stk = lambda group: [jnp.stack(rows, axis=1) for rows in group]
    return tuple([y_p, y_s] + stk(outs_p) + stk(outs_s))
```

```python
import functools
import math

import numpy as np
import jax
import jax.numpy as jnp
from jax import lax
from jax.experimental import pallas as pl
from jax.experimental.pallas import tpu as pltpu

HEAD_DIM = 64
H_A, G_A = 8, 4
R_A = H_A // G_A
H_IDX, D_IDX = 4, 64
DSA_TOPK = 256
D_CONV, CONV_W = 512, 31
H_C, G_C = 8, 2
R_C = H_C // G_C
COMP_BLOCK, COMP_STRIDE = 32, 16
SEL_BLOCK, SEL_TOPK = 64, 8
WINDOW = 512
FFN_CONV_W = 3
NUM_BUCKETS, MAX_DISTANCE = 32, 128
FORCE_SCORE = 1.0e4
EPS = 1e-6
PAGE_SIZE = 128

CH = 128
FT_ROWS = 640
VMEM_LIMIT = 56 * 1024 * 1024

f32, bf16, i32 = jnp.float32, jnp.bfloat16, jnp.int32
NEG_INF = float("-inf")


def _cparams(n_axes):
    return pltpu.CompilerParams(dimension_semantics=("arbitrary",) * n_axes, vmem_limit_bytes=VMEM_LIMIT)


def _dot(a, b):
    return jnp.dot(a, b, preferred_element_type=f32)


def _dot_nt(a, b):
    return lax.dot_general(a, b, (((1,), (1,)), ((), ())), preferred_element_type=f32)


def _colsum(x):
    return x.reshape(CH // 8, 8, CH).sum(axis=0)


def _colmax(x):
    return x.reshape(CH // 8, 8, CH).max(axis=0)


def _mod_kernel(c_ref, w_ref, b_ref, o_ref):
    c = c_ref[...]
    a = (c * jax.nn.sigmoid(c)).astype(bf16)
    o_ref[0] = _dot(a, w_ref[0].astype(bf16)) + b_ref[0]


def _modulation(c_all, w_mod, b_mod):
    depth, d, d6 = w_mod.shape
    n = c_all.shape[0]
    tn = 1024
    return pl.pallas_call(
        _mod_kernel,
        grid=(depth, d6 // tn),
        in_specs=[pl.BlockSpec((n, d), lambda l, j: (0, 0)),
                  pl.BlockSpec((1, d, tn), lambda l, j: (l, 0, j)),
                  pl.BlockSpec((1, 1, tn), lambda l, j: (l, 0, j))],
        out_specs=pl.BlockSpec((1, n, tn), lambda l, j: (l, 0, j)),
        out_shape=jax.ShapeDtypeStruct((depth, n, d6), f32),
        compiler_params=_cparams(2),
    )(c_all, w_mod, b_mod.reshape(depth, 1, d6))


def _t5_bucket(n):
    max_exact = NUM_BUCKETS // 2
    nf = jnp.maximum(n, 1).astype(f32)
    large = max_exact + (jnp.log(nf / max_exact) / math.log(MAX_DISTANCE / max_exact)
                         * (NUM_BUCKETS - max_exact)).astype(i32)
    return jnp.where(n < max_exact, n, jnp.minimum(large, NUM_BUCKETS - 1))


def _bias_from_dist(dist, tab_ref, h):
    b = _t5_bucket(jnp.maximum(dist, 0))
    out = jnp.zeros(dist.shape, f32)
    for k in range(NUM_BUCKETS):
        out = jnp.where(b == k, tab_ref[k, h], out)
    return out


def _toep_kernel(tab_ref, o_ref):
    h = pl.program_id(0)
    row = lax.broadcasted_iota(i32, (CH, CH), 0)
    lane = lax.broadcasted_iota(i32, (CH, CH), 1)
    o_ref[0, 0] = _bias_from_dist(jnp.full((CH, CH), 2 * CH, i32), tab_ref, h)
    o_ref[0, 1] = _bias_from_dist(CH + lane - row, tab_ref, h)
    o_ref[0, 2] = _bias_from_dist(lane - row, tab_ref, h)


def _toeplitz_bias(rel_bias):
    nh = rel_bias.shape[1]
    return pl.pallas_call(
        _toep_kernel,
        grid=(nh,),
        in_specs=[pl.BlockSpec(memory_space=pltpu.SMEM)],
        out_specs=pl.BlockSpec((1, 3, CH, CH), lambda h: (h, 0, 0, 0)),
        out_shape=jax.ShapeDtypeStruct((nh, 3, CH, CH), f32),
        compiler_params=_cparams(1),
    )(rel_bias)


def _cmpbias_kernel(tab_ref, o_ref):
    h = pl.program_id(0)
    qb = pl.program_id(1)
    row = lax.broadcasted_iota(i32, (CH, CH), 0)
    lane = lax.broadcasted_iota(i32, (CH, CH), 1)
    dist = qb * CH + lane - (row * COMP_STRIDE + COMP_BLOCK - 1)
    o_ref[0, 0] = _bias_from_dist(dist, tab_ref, H_A + h)


def _cmp_bias(rel_bias, nqb):
    return pl.pallas_call(
        _cmpbias_kernel,
        grid=(H_C, nqb),
        in_specs=[pl.BlockSpec(memory_space=pltpu.SMEM)],
        out_specs=pl.BlockSpec((1, 1, CH, CH), lambda h, q: (h, q, 0, 0)),
        out_shape=jax.ShapeDtypeStruct((H_C, nqb, CH, CH), f32),
        compiler_params=_cparams(2),
    )(rel_bias)


def _inproj_kernel(x_ref, sh_ref, sc_ref, g_ref, *refs, n_w):
    w_refs = refs[:n_w]
    wft_ref = refs[n_w]
    out_refs = refs[n_w + 1:2 * n_w + 1]
    ft_ref = refs[2 * n_w + 1]
    x = x_ref[0]
    y = x * lax.rsqrt(jnp.mean(x * x, axis=-1, keepdims=True) + EPS) * g_ref[...]
    hb = (y * (1.0 + sc_ref[0]) + sh_ref[0]).astype(bf16)
    for w_ref, o_ref in zip(w_refs, out_refs):
        o_ref[0] = _dot(hb, w_ref[...])
    ft = _dot_nt(wft_ref[...], hb)
    for j in range(ft.shape[1] // CH):
        ft_ref[0, j] = ft[:, j * CH:(j + 1) * CH]


def _mod_spec(m, tm):
    if m.shape[1] == 1:
        return pl.BlockSpec((1, 1, m.shape[2]), lambda n, i: (n, 0, 0))
    return pl.BlockSpec((1, tm, m.shape[2]), lambda n, i: (n, i, 0))


def _const_spec(a):
    nd = a.ndim
    return pl.BlockSpec(a.shape, lambda *_: (0,) * nd, pipeline_mode=pl.Buffered(1))


def _inproj(x, sh, sc, g, ws, wft, tm):
    nb, t, d = x.shape
    n_w = len(ws)
    in_specs = [pl.BlockSpec((1, tm, d), lambda n, i: (n, i, 0)), _mod_spec(sh, tm), _mod_spec(sc, tm),
                _const_spec(g)] + [_const_spec(w) for w in ws] + [_const_spec(wft)]
    out_specs = [pl.BlockSpec((1, tm, w.shape[1]), lambda n, i: (n, i, 0)) for w in ws]
    out_specs.append(pl.BlockSpec((1, tm // CH, FT_ROWS, CH), lambda n, i: (n, i, 0, 0)))
    out_shape = [jax.ShapeDtypeStruct((nb, t, w.shape[1]), f32) for w in ws]
    out_shape.append(jax.ShapeDtypeStruct((nb, t // CH, FT_ROWS, CH), f32))
    return pl.pallas_call(
        functools.partial(_inproj_kernel, n_w=n_w),
        grid=(nb, t // tm), in_specs=in_specs, out_specs=out_specs, out_shape=out_shape,
        compiler_params=_cparams(2),
    )(x, sh, sc, g, *ws, wft)


def _sortable(x):
    x = jnp.where(x == 0.0, 0.0, x)
    b = lax.bitcast_convert_type(x, i32)
    return jnp.where(b < 0, b ^ 0x7FFFFFFF, b)


def _head_operand(tile, src_half, dst_half, mask_other):
    if src_half != dst_half:
        tile = pltpu.roll(tile, 64, 1)
    if mask_other:
        lane = lax.broadcasted_iota(i32, tile.shape, 1)
        keep = (lane < 64) if dst_half == 0 else (lane >= 64)
        tile = jnp.where(keep, tile, 0.0)
    return tile.astype(bf16)


def _shr_pow2(x, g):
    return x // g if isinstance(x, int) else jnp.right_shift(x, g.bit_length() - 1)


def _chunk_loop(lo, hi, body, carry, group):
    if isinstance(lo, int) and isinstance(hi, int):
        for c in range(lo, hi):
            carry = body(c, carry)
        return carry

    def gbody(gi, carry):
        for k in range(group):
            carry = body(gi * group + k, carry)
        return carry

    return lax.fori_loop(_shr_pow2(lo, group), _shr_pow2(hi + group - 1, group), gbody, carry)


def _attend(lo, hi, q_stack, get_k, extra_fn, get_vt, s_scr, acc_scr):
    n_groups = len(q_stack)
    rep = q_stack[0].shape[0] // CH
    nh = n_groups * rep

    def p1(c, macc):
        add = extra_fn(c)
        macc = list(macc)
        for g in range(n_groups):
            s_all = _dot_nt(get_k(c, g).astype(bf16), q_stack[g])
            for r in range(rep):
                h = g * rep + r
                s = s_all[:, r * CH:(r + 1) * CH] + add(h, g)
                s_scr[h, c] = s
                macc[h] = jnp.maximum(macc[h], _colmax(s))
        return tuple(macc)

    macc = _chunk_loop(lo, hi, p1, tuple(jnp.full((8, CH), NEG_INF, f32) for _ in range(nh)), 2)
    m = []
    for h in range(nh):
        mh = jnp.max(macc[h], axis=0, keepdims=True)
        m.append(jnp.where(mh > NEG_INF, mh, 0.0))
    for g in range(n_groups):
        acc_scr[g] = jnp.zeros(acc_scr.shape[1:], f32)

    def p2(c, lacc):
        lacc = list(lacc)
        for g in range(n_groups):
            ps = []
            for r in range(rep):
                h = g * rep + r
                p = jnp.exp(s_scr[h, c] - m[h])
                lacc[h] = lacc[h] + _colsum(p)
                ps.append(p.astype(bf16))
            acc_scr[g] += _dot(get_vt(c, g).astype(bf16), jnp.concatenate(ps, axis=1))
        return tuple(lacc)

    lacc = _chunk_loop(lo, hi, p2, tuple(jnp.zeros((8, CH), f32) for _ in range(nh)), 2)
    outs = []
    for h in range(nh):
        g, r = divmod(h, rep)
        inv = 1.0 / jnp.maximum(jnp.sum(lacc[h], axis=0, keepdims=True), 1e-30)
        outs.append(acc_scr[g, :, r * CH:(r + 1) * CH] * inv)
    return outs


def _dsa_core(i_abs, nci, topk, pos_bits, n_valid, q_tile, iq_tile, iw_t, get_ik, get_k, get_vt, toep_ref,
              keys_scr, mask_scr, s_scr, acc_scr, ot_scr):
    row = lax.broadcasted_iota(i32, (CH, CH), 0)
    lane = lax.broadcasted_iota(i32, (CH, CH), 1)
    t_pos = i_abs * CH + lane

    iq_stack = jnp.concatenate([_head_operand(iq_tile(h // 2), h % 2, 0, True) for h in range(H_IDX)], axis=0)

    def idx_body(c, carry):
        s_all = _dot_nt(get_ik(c).astype(bf16), iq_stack)
        acc = jnp.zeros((CH, CH), f32)
        for h in range(H_IDX):
            acc = acc + iw_t[h:h + 1, :] * jnp.maximum(s_all[:, h * CH:(h + 1) * CH], 0.0)
        acc = jnp.where(c * CH + row <= t_pos, acc, NEG_INF)
        keys_scr[c] = _sortable(acc)
        return carry

    _chunk_loop(0, nci, idx_body, 0, 4)

    def count(pred_fn):
        def body(c, acc):
            return acc + _colsum(jnp.where(pred_fn(c, keys_scr[c]), 1.0, 0.0))
        acc = _chunk_loop(0, nci, body, jnp.zeros((8, CH), f32), 4)
        return jnp.sum(acc, axis=0, keepdims=True)

    kf = float(topk)
    int_min = jnp.full((1, CH), -2 ** 31, i32)
    c0 = count(lambda c, k: k >= 0)
    cand = jnp.where(c0 >= kf, jnp.zeros((1, CH), i32), int_min)

    def bit_body(j, cand):
        trial = cand | jnp.left_shift(jnp.int32(1), 30 - j)
        cnt = count(lambda c, k: k >= trial)
        return jnp.where(cnt >= kf, trial, cand)

    cand = lax.fori_loop(0, 31, bit_body, cand)
    cnt_ge = count(lambda c, k: k >= cand)
    tied = (cnt_ge > kf) & (lane[0:1, :] < n_valid)
    has_tie = jnp.max(jnp.where(tied, 1.0, 0.0)) > 0.5

    @pl.when(jnp.logical_not(has_tie))
    def _():
        def mask_body(c, carry):
            sel = (keys_scr[c] >= cand) & (c * CH + row <= t_pos)
            mask_scr[c] = jnp.where(sel, 0.0, NEG_INF)
            return carry
        _chunk_loop(0, nci, mask_body, 0, 4)

    @pl.when(has_tie)
    def _():
        need = kf - count(lambda c, k: k > cand)

        def pos_body(j, pcut):
            trial = pcut | jnp.left_shift(jnp.int32(1), pos_bits - 1 - j)
            cnt = count(lambda c, k: (k == cand) & (c * CH + row < trial))
            return jnp.where(cnt < need, trial, pcut)

        pcut = lax.fori_loop(0, pos_bits, pos_body, jnp.zeros((1, CH), i32))

        def mask_body(c, carry):
            k = keys_scr[c]
            s_pos = c * CH + row
            sel = ((k > cand) | ((k == cand) & (s_pos <= pcut))) & (s_pos <= t_pos)
            mask_scr[c] = jnp.where(sel, 0.0, NEG_INF)
            return carry
        _chunk_loop(0, nci, mask_body, 0, 4)

    scale = HEAD_DIM ** -0.5
    q_stack = []
    for g in range(G_A):
        heads = [_head_operand(q_tile((g * R_A + r) // 2) * scale, (g * R_A + r) % 2, g % 2, True) for r in range(R_A)]
        q_stack.append(jnp.concatenate(heads, axis=0))

    def extra_fn(c):
        bidx = jnp.clip(c - i_abs + 2, 0, 2)
        mk = mask_scr[c]
        return lambda h, g: toep_ref[h, bidx] + mk

    outs = _attend(0, nci, q_stack, lambda c, g: get_k(c, g // 2), extra_fn, get_vt, s_scr, acc_scr)
    for h in range(H_A):
        ot_scr[h * HEAD_DIM:(h + 1) * HEAD_DIM, :] = outs[h]
    return ot_scr[...].T


def _dsa_prompt_kernel(q_ref, iq_ref, misc_ref, ik_ref, k_ref, vt_ref, toep_ref, o_ref,
                       keys_scr, mask_scr, s_scr, acc_scr, ot_scr, *, topk, pos_bits):
    i = pl.program_id(1)

    def rows(c):
        return pl.ds(pl.multiple_of(c * CH, CH), CH)

    o_ref[0] = _dsa_core(
        i, i + 1, topk, pos_bits, CH,
        lambda j: q_ref[0, :, j * CH:(j + 1) * CH],
        lambda j: iq_ref[0, :, j * CH:(j + 1) * CH],
        misc_ref[0, 0, 0:8, :],
        lambda c: ik_ref[0, rows(c), :],
        lambda c, j: k_ref[0, rows(c), j * CH:(j + 1) * CH],
        lambda c, g: vt_ref[0, c, g * HEAD_DIM:(g + 1) * HEAD_DIM, :],
        toep_ref, keys_scr, mask_scr, s_scr, acc_scr, ot_scr)


def _dsa_prompt(qa, iq, ikp, kva, ft, toep):
    n, t, _ = qa.shape
    nc = t // CH
    topk = min(DSA_TOPK, t // 4)
    assert nc % 4 == 0, "traced chunk loops walk aligned groups of up to 4 chunks"
    kern = functools.partial(_dsa_prompt_kernel, topk=topk, pos_bits=max(1, (t - 1).bit_length()))
    return pl.pallas_call(
        kern,
        grid=(n, nc),
        in_specs=[pl.BlockSpec((1, CH, 512), lambda b, i: (b, i, 0)),
                  pl.BlockSpec((1, CH, 256), lambda b, i: (b, i, 0)),
                  pl.BlockSpec((1, 1, CH, CH), lambda b, i: (b, i, 4, 0)),
                  pl.BlockSpec((1, t, CH), lambda b, i: (b, 0, 0)),
                  pl.BlockSpec((1, t, 256), lambda b, i: (b, 0, 0)),
                  pl.BlockSpec((1, nc, 256, CH), lambda b, i: (b, 0, 0, 0)),
                  pl.BlockSpec((H_A, 3, CH, CH), lambda b, i: (0, 0, 0, 0))],
        out_specs=pl.BlockSpec((1, CH, 512), lambda b, i: (b, i, 0)),
        out_shape=jax.ShapeDtypeStruct((n, t, 512), f32),
        scratch_shapes=[pltpu.VMEM((nc, CH, CH), i32), pltpu.VMEM((nc, CH, CH), f32),
                        pltpu.VMEM((H_A, nc, CH, CH), f32), pltpu.VMEM((G_A, HEAD_DIM, R_A * CH), f32),
                        pltpu.VMEM((512, CH), f32)],
        compiler_params=_cparams(2),
    )(qa, iq, ft, ikp, kva, ft, toep)


def _pad_rows(ref, x):
    ref[...] = jnp.zeros(ref.shape, f32)
    ref[0:x.shape[0], :] = x


def _dsa_sample_kernel(pt_ref, q_ref, iq_ref, ft_ref, kvn_ref, ikn_ref, *refs, n_pages, topk, pos_bits):
    idx_pages = refs[:n_pages]
    kv_pages = refs[n_pages:2 * n_pages]
    (toep_ref, o_ref, q_scr, iq_scr, iw_scr, ik_scr, k_scr, vt_scr, keys_scr, mask_scr, s_scr, acc_scr,
     ot_scr) = refs[2 * n_pages:]
    t_new = q_ref.shape[1]
    for p in range(n_pages):
        ik_scr[p] = jnp.zeros((CH, CH), f32)
        ik_scr[p, :, 0:D_IDX] = idx_pages[p][0, 0]
        page = kv_pages[p][0, 0]
        k_scr[p] = page[:, 0:256]
        vt_scr[p] = page[:, 256:512].T
    _pad_rows(ik_scr.at[n_pages], ikn_ref[0])
    _pad_rows(k_scr.at[n_pages], kvn_ref[0][:, 0:256])
    vt_scr[n_pages] = jnp.zeros((256, CH), f32)
    vt_scr[n_pages, :, 0:t_new] = ft_ref[0, 0:256, :]
    _pad_rows(q_scr, q_ref[0])
    _pad_rows(iq_scr, iq_ref[0])
    iw_scr[...] = jnp.zeros((8, CH), f32)
    iw_scr[0:H_IDX, 0:t_new] = ft_ref[0, 512:512 + H_IDX, :]
    o = _dsa_core(
        n_pages, n_pages + 1, topk, pos_bits, t_new,
        lambda j: q_scr[:, j * CH:(j + 1) * CH],
        lambda j: iq_scr[:, j * CH:(j + 1) * CH],
        iw_scr[...],
        lambda c: ik_scr[c],
        lambda c, j: k_scr[c, :, j * CH:(j + 1) * CH],
        lambda c, g: vt_scr[c, g * HEAD_DIM:(g + 1) * HEAD_DIM, :],
        toep_ref, keys_scr, mask_scr, s_scr, acc_scr, ot_scr)
    o_ref[0] = o[0:t_new]


def _page_specs(n_pages, layer, width):
    return [pl.BlockSpec((1, 1, PAGE_SIZE, width), lambda b, pt, p=p: (pt[b, p], layer, 0, 0)) for p in range(n_pages)]


def _dsa_sample(page_table, layer, qa, iq, ft, kva, ikp, idx_pool, kv_pool, toep):
    n, t_new, _ = qa.shape
    n_pages = page_table.shape[1]
    nc = n_pages + 1
    lk = n_pages * PAGE_SIZE + t_new
    topk = min(DSA_TOPK, lk // 4)
    kern = functools.partial(_dsa_sample_kernel, n_pages=n_pages, topk=topk, pos_bits=(nc * CH - 1).bit_length())
    seq = lambda w: pl.BlockSpec((1, t_new, w), lambda b, pt: (b, 0, 0))
    grid_spec = pltpu.PrefetchScalarGridSpec(
        num_scalar_prefetch=1, grid=(n,),
        in_specs=[seq(512), seq(256), pl.BlockSpec((1, FT_ROWS, t_new), lambda b, pt: (b, 0, 0)), seq(512), seq(CH)]
        + _page_specs(n_pages, layer, D_IDX) + _page_specs(n_pages, layer, 512)
        + [pl.BlockSpec((H_A, 3, CH, CH), lambda b, pt: (0, 0, 0, 0))],
        out_specs=pl.BlockSpec((1, t_new, 512), lambda b, pt: (b, 0, 0)),
        scratch_shapes=[pltpu.VMEM((CH, 512), f32), pltpu.VMEM((CH, 256), f32), pltpu.VMEM((8, CH), f32),
                        pltpu.VMEM((nc, CH, CH), f32), pltpu.VMEM((nc, CH, 256), f32), pltpu.VMEM((nc, 256, CH), f32),
                        pltpu.VMEM((nc, CH, CH), i32), pltpu.VMEM((nc, CH, CH), f32),
                        pltpu.VMEM((H_A, nc, CH, CH), f32), pltpu.VMEM((G_A, HEAD_DIM, R_A * CH), f32),
                        pltpu.VMEM((512, CH), f32)])
    return pl.pallas_call(
        kern, grid_spec=grid_spec, out_shape=jax.ShapeDtypeStruct((n, t_new, 512), f32),
        compiler_params=_cparams(1),
    )(page_table, qa, iq, ft, kva, ikp, *([idx_pool] * n_pages), *([kv_pool] * n_pages), toep)


def _compress_core(xk_ref, xv_ref, wbd_ref, pe_ref, kc_ref, kct_ref):
    nr = xk_ref.shape[0] // COMP_STRIDE
    halves = []
    for s, x_ref in enumerate((xk_ref, xv_ref)):
        a0 = jnp.zeros((nr, CH), f32)
        a1 = jnp.zeros((nr, CH), f32)
        for r in range(COMP_STRIDE):
            xr = x_ref[pl.ds(r, nr, stride=COMP_STRIDE), :]
            r2 = COMP_STRIDE + r
            a0 = a0 + _dot((xr + pe_ref[s, r:r + 1, :]).astype(bf16), wbd_ref[s, r])
            a1 = a1 + _dot((xr + pe_ref[s, r2:r2 + 1, :]).astype(bf16), wbd_ref[s, r2])
        kc = a0 + pltpu.roll(a1, nr - 1, 0)
        if nr < CH:
            kc = jnp.concatenate([kc, jnp.zeros((CH - nr, CH), f32)], axis=0)
        halves.append(kc)
    kc_ref[0, :, 0:CH] = halves[0]
    kc_ref[0, :, CH:2 * CH] = halves[1]
    kct_ref[0] = halves[1].T


def _compress_prompt_kernel(xk_ref, xv_ref, wbd_ref, pe_ref, kc_ref, kct_ref):
    _compress_core(xk_ref.at[0], xv_ref.at[0], wbd_ref, pe_ref, kc_ref, kct_ref)


def _compress_sample_kernel(pt_ref, *refs, n_pages):
    pages = refs[:n_pages]
    wbd_ref, pe_ref, kc_ref, kct_ref, xk_scr, xv_scr = refs[n_pages:]
    for p in range(n_pages):
        page = pages[p][0, 0]
        xk_scr[p * PAGE_SIZE:(p + 1) * PAGE_SIZE, :] = page[:, 0:CH]
        xv_scr[p * PAGE_SIZE:(p + 1) * PAGE_SIZE, :] = page[:, CH:2 * CH]
    _compress_core(xk_scr, xv_scr, wbd_ref, pe_ref, kc_ref, kct_ref)


def _compress_out(n):
    return ([pl.BlockSpec((1, CH, 256), lambda b, *_: (b, 0, 0)), pl.BlockSpec((1, CH, CH), lambda b, *_: (b, 0, 0))],
            [jax.ShapeDtypeStruct((n, CH, 256), f32), jax.ShapeDtypeStruct((n, CH, CH), f32)])


def _compress_prompt(kvc4, wbd, pe256):
    n, t, _ = kvc4.shape
    assert t % (8 * COMP_STRIDE) == 0 and t <= COMP_STRIDE * CH
    out_specs, out_shape = _compress_out(n)
    return pl.pallas_call(
        _compress_prompt_kernel, grid=(n,),
        in_specs=[pl.BlockSpec((1, t, CH), lambda b: (b, 0, 0)), pl.BlockSpec((1, t, CH), lambda b: (b, 0, 1)),
                  _const_spec(wbd), _const_spec(pe256)],
        out_specs=out_specs, out_shape=out_shape, compiler_params=_cparams(1),
    )(kvc4, kvc4, wbd, pe256)


def _compress_sample(page_table, layer, nsa_pool, wbd, pe256):
    n, n_pages = page_table.shape
    assert n_pages * PAGE_SIZE <= COMP_STRIDE * CH
    out_specs, out_shape = _compress_out(n)
    grid_spec = pltpu.PrefetchScalarGridSpec(
        num_scalar_prefetch=1, grid=(n,),
        in_specs=_page_specs(n_pages, layer, 512) + [_const_spec(wbd), _const_spec(pe256)],
        out_specs=out_specs,
        scratch_shapes=[pltpu.VMEM((n_pages * PAGE_SIZE, CH), f32), pltpu.VMEM((n_pages * PAGE_SIZE, CH), f32)])
    return pl.pallas_call(
        functools.partial(_compress_sample_kernel, n_pages=n_pages), grid_spec=grid_spec, out_shape=out_shape,
        compiler_params=_cparams(1),
    )(page_table, *([nsa_pool] * n_pages), wbd, pe256)


def _split3(x):
    hi = x.astype(bf16)
    r1 = x - hi.astype(f32)
    mid = r1.astype(bf16)
    lo = (r1 - mid.astype(f32)).astype(bf16)
    return hi, mid, lo


def _nsa_core(i_abs, n_sel, win_c0, q_tile, gate_t, kc_ref, kct_ref, cmpb_ref, cover_ref, get_selk, get_selvt,
              get_wink, get_winvt, toep_ref, selm_scr, mask_scr, s_scr, acc_scr, ot_scr):
    row = lax.broadcasted_iota(i32, (CH, CH), 0)
    lane = lax.broadcasted_iota(i32, (CH, CH), 1)
    t_pos = i_abs * CH + lane
    scale = HEAD_DIM ** -0.5
    nci = i_abs + 1
    n_chunks_static = mask_scr.shape[1]
    ns_pad = -(-n_sel // 8) * 8

    q_stack = []
    for g in range(G_C):
        heads = [_head_operand(q_tile((g * R_C + r) // 2) * scale, (g * R_C + r) % 2, g, True) for r in range(R_C)]
        q_stack.append(jnp.concatenate(heads, axis=0))

    kc_b = kc_ref[0, :, 0:CH].astype(bf16)
    cmp_valid = t_pos >= row * COMP_STRIDE + COMP_BLOCK - 1
    o_cmp = []
    p_sum = []
    for g in range(G_C):
        s_all = _dot_nt(kc_b, q_stack[g])
        ps = []
        for r in range(R_C):
            h = g * R_C + r
            s = jnp.where(cmp_valid, s_all[:, r * CH:(r + 1) * CH] + cmpb_ref[h, 0], NEG_INF)
            m = jnp.max(s, axis=0, keepdims=True)
            e = jnp.exp(s - jnp.where(m > NEG_INF, m, 0.0))
            ps.append(e / jnp.maximum(jnp.sum(e, axis=0, keepdims=True), 1e-30))
        p_sum.append(ps[0] + ps[1] + ps[2] + ps[3])
        o_all = _dot(kct_ref[0, g * HEAD_DIM:(g + 1) * HEAD_DIM, :].astype(bf16),
                     jnp.concatenate([p.astype(bf16) for p in ps], axis=1))
        o_cmp += [o_all[:, r * CH:(r + 1) * CH] for r in range(R_C)]

    srow = lax.broadcasted_iota(i32, (ns_pad, CH), 0)
    cur = jnp.right_shift(i_abs * CH + lax.broadcasted_iota(i32, (ns_pad, CH), 1), SEL_BLOCK.bit_length() - 1)
    adm = srow <= cur
    forced = (srow == 0) | (srow == cur) | (srow == cur - 1)
    cov = cover_ref[...]
    for g in range(G_C):
        hi, mid, lo = _split3(p_sum[g])
        imp = (_dot(cov, hi) + _dot(cov, mid) + _dot(cov, lo))[0:ns_pad]
        score = jnp.where(adm, imp + jnp.where(forced, FORCE_SCORE, 0.0), NEG_INF)
        rank = jnp.zeros((ns_pad, CH), f32)
        for s2 in range(n_sel):
            other = score[s2:s2 + 1, :]
            ahead = (other > score) | ((other == score) & (s2 < srow))
            rank = rank + jnp.where(ahead, 1.0, 0.0)
        selm_scr[g, 0:ns_pad, :] = jnp.where(adm & (rank < float(SEL_TOPK)) & (srow < n_sel), 1.0, 0.0)
        for c in range(n_chunks_static):
            blk0 = selm_scr[g, 2 * c:2 * c + 1, :]
            blk1 = selm_scr[g, 2 * c + 1:2 * c + 2, :]
            on = jnp.where(row < SEL_BLOCK, blk0, blk1) > 0.5
            mask_scr[g, c] = jnp.where(on & (c * CH + row <= t_pos), 0.0, NEG_INF)

    def slc_extra(c):
        bidx = jnp.clip(c - i_abs + 2, 0, 2)
        return lambda h, g: toep_ref[h, bidx] + mask_scr[g, c]

    o_slc = _attend(0, nci, q_stack, lambda c, g: get_selk(c), slc_extra, get_selvt, s_scr, acc_scr)

    def win_extra(c):
        bidx = jnp.clip(c - i_abs + 2, 0, 2)
        wd = t_pos - (c * CH + row)
        wmask = jnp.where((wd >= 0) & (wd <= WINDOW), 0.0, NEG_INF)
        return lambda h, g: toep_ref[h, bidx] + wmask

    if isinstance(i_abs, int):
        win_lo = max(i_abs - WINDOW // CH, win_c0)
    else:
        win_lo = jnp.maximum(i_abs - WINDOW // CH, win_c0)
    o_win = _attend(win_lo, nci, q_stack, lambda c, g: get_wink(c), win_extra, get_winvt, s_scr, acc_scr)

    for h in range(H_C):
        g0 = jax.nn.sigmoid(gate_t[8 + h:9 + h, :])
        g1 = jax.nn.sigmoid(gate_t[16 + h:17 + h, :])
        g2 = jax.nn.sigmoid(gate_t[24 + h:25 + h, :])
        ot_scr[h * HEAD_DIM:(h + 1) * HEAD_DIM, :] = g0 * o_cmp[h] + g1 * o_slc[h] + g2 * o_win[h]
    return ot_scr[...].T


def _nsa_prompt_kernel(q_ref, misc_ref, kc_ref, kct_ref, cmpb_ref, cover_ref, selk_ref, selvt_ref, wink_ref, winvt_ref,
                       toep_ref, o_ref, selm_scr, mask_scr, s_scr, acc_scr, ot_scr, *, n_sel):
    i = pl.program_id(1)

    def rows(c):
        return pl.ds(pl.multiple_of(c * CH, CH), CH)

    o_ref[0] = _nsa_core(
        i, n_sel, 0,
        lambda j: q_ref[0, :, j * CH:(j + 1) * CH],
        misc_ref[0, 0, 0:32, :],
        kc_ref, kct_ref, cmpb_ref, cover_ref,
        lambda c: selk_ref[0, rows(c), :],
        lambda c, g: selvt_ref[0, c, g * HEAD_DIM:(g + 1) * HEAD_DIM, :],
        lambda c: wink_ref[0, rows(c), :],
        lambda c, g: winvt_ref[0, c, g * HEAD_DIM:(g + 1) * HEAD_DIM, :],
        toep_ref, selm_scr, mask_scr, s_scr, acc_scr, ot_scr)


def _cover_matrix(n_keys):
    n_cmp = (n_keys - COMP_BLOCK) // COMP_STRIDE + 1
    n_sel = -(-n_keys // SEL_BLOCK)
    c0 = np.arange(n_cmp)[:, None] * COMP_STRIDE
    s0 = np.arange(n_sel)[None, :] * SEL_BLOCK
    cover = ((c0 < s0 + SEL_BLOCK) & (c0 + COMP_BLOCK > s0)).astype(np.float32)
    out = np.zeros((CH, CH), np.float32)
    out[:n_sel, :n_cmp] = cover.T
    return jnp.asarray(out, dtype=bf16), n_sel


def _nsa_prompt(qc, ft, kc, kct, cmpb, kvc4, kvw, toep):
    n, t, _ = qc.shape
    nc = t // CH
    cover, n_sel = _cover_matrix(t)
    return pl.pallas_call(
        functools.partial(_nsa_prompt_kernel, n_sel=n_sel),
        grid=(n, nc),
        in_specs=[pl.BlockSpec((1, CH, 512), lambda b, i: (b, i, 0)),
                  pl.BlockSpec((1, 1, CH, CH), lambda b, i: (b, i, 4, 0)),
                  pl.BlockSpec((1, CH, 256), lambda b, i: (b, 0, 0)),
                  pl.BlockSpec((1, CH, CH), lambda b, i: (b, 0, 0)),
                  pl.BlockSpec((H_C, 1, CH, CH), lambda b, i: (0, i, 0, 0)),
                  pl.BlockSpec((CH, CH), lambda b, i: (0, 0)),
                  pl.BlockSpec((1, t, CH), lambda b, i: (b, 0, 2)),
                  pl.BlockSpec((1, nc, CH, CH), lambda b, i: (b, 0, 2, 0)),
                  pl.BlockSpec((1, t, CH), lambda b, i: (b, 0, 0)),
                  pl.BlockSpec((1, nc, CH, CH), lambda b, i: (b, 0, 3, 0)),
                  pl.BlockSpec((H_C, 3, CH, CH), lambda b, i: (1, 0, 0, 0))],
        out_specs=pl.BlockSpec((1, CH, 512), lambda b, i: (b, i, 0)),
        out_shape=jax.ShapeDtypeStruct((n, t, 512), f32),
        scratch_shapes=[pltpu.VMEM((G_C, CH, CH), f32), pltpu.VMEM((G_C, nc, CH, CH), f32),
                        pltpu.VMEM((H_C, nc, CH, CH), f32), pltpu.VMEM((G_C, HEAD_DIM, R_C * CH), f32),
                        pltpu.VMEM((512, CH), f32)],
        compiler_params=_cparams(2),
    )(qc, ft, kc, kct, cmpb, cover, kvc4, ft, kvw, ft, toep)


def _nsa_sample_kernel(pt_ref, q_ref, ft_ref, kvn_ref, kvwn_ref, win_ref, kc_ref, kct_ref, cmpb_ref, cover_ref, *refs,
                       n_pages, n_sel, n_win):
    pages = refs[:n_pages]
    (toep_ref, o_ref, q_scr, gate_scr, selk_scr, selvt_scr, wink_scr, winvt_scr,
     selm_scr, mask_scr, s_scr, acc_scr, ot_scr) = refs[n_pages:]
    t_new = q_ref.shape[1]
    for p in range(n_pages):
        page = pages[p][0, 0]
        selk_scr[p] = page[:, 256:384]
        selvt_scr[p] = page[:, 384:512].T
    _pad_rows(selk_scr.at[n_pages], kvn_ref[0][:, 256:384])
    selvt_scr[n_pages] = jnp.zeros((CH, CH), f32)
    selvt_scr[n_pages, :, 0:t_new] = ft_ref[0, 256:384, :]
    for w in range(n_win):
        blk = win_ref[0, w * CH:(w + 1) * CH, :]
        wink_scr[w] = blk[:, 0:CH]
        winvt_scr[w] = blk[:, CH:2 * CH].T
    _pad_rows(wink_scr.at[n_win], kvwn_ref[0][:, 0:CH])
    winvt_scr[n_win] = jnp.zeros((CH, CH), f32)
    winvt_scr[n_win, :, 0:t_new] = ft_ref[0, 384:512, :]
    _pad_rows(q_scr, q_ref[0])
    gate_scr[...] = jnp.zeros((32, CH), f32)
    gate_scr[:, 0:t_new] = ft_ref[0, 512:544, :]
    win_c0 = n_pages - n_win
    o = _nsa_core(
        n_pages, n_sel, win_c0,
        lambda j: q_scr[:, j * CH:(j + 1) * CH],
        gate_scr[...],
        kc_ref, kct_ref, cmpb_ref, cover_ref,
        lambda c: selk_scr[c],
        lambda c, g: selvt_scr[c, g * HEAD_DIM:(g + 1) * HEAD_DIM, :],
        lambda c: wink_scr[c - win_c0],
        lambda c, g: winvt_scr[c - win_c0, g * HEAD_DIM:(g + 1) * HEAD_DIM, :],
        toep_ref, selm_scr, mask_scr, s_scr, acc_scr, ot_scr)
    o_ref[0] = o[0:t_new]


def _nsa_sample(page_table, layer, qc, ft, kvc4, kvw, win_buf, kc, kct, cmpb, nsa_pool, toep):
    n, t_new, _ = qc.shape
    n_pages = page_table.shape[1]
    nc = n_pages + 1
    w_buf = win_buf.shape[1]
    assert w_buf % CH == 0 and w_buf <= n_pages * PAGE_SIZE
    n_win = w_buf // CH
    cover, n_sel = _cover_matrix(n_pages * PAGE_SIZE + t_new)
    assert cmpb.shape[1] > n_pages
    seq = lambda w: pl.BlockSpec((1, t_new, w), lambda b, pt: (b, 0, 0))
    grid_spec = pltpu.PrefetchScalarGridSpec(
        num_scalar_prefetch=1, grid=(n,),
        in_specs=[seq(512), pl.BlockSpec((1, FT_ROWS, t_new), lambda b, pt: (b, 0, 0)), seq(512), seq(256),
                  pl.BlockSpec((1, w_buf, 256), lambda b, pt: (b, 0, 0)),
                  pl.BlockSpec((1, CH, 256), lambda b, pt: (b, 0, 0)),
                  pl.BlockSpec((1, CH, CH), lambda b, pt: (b, 0, 0)),
                  pl.BlockSpec((H_C, 1, CH, CH), lambda b, pt: (0, n_pages, 0, 0)),
                  pl.BlockSpec((CH, CH), lambda b, pt: (0, 0))]
        + _page_specs(n_pages, layer, 512)
        + [pl.BlockSpec((H_C, 3, CH, CH), lambda b, pt: (1, 0, 0, 0))],
        out_specs=pl.BlockSpec((1, t_new, 512), lambda b, pt: (b, 0, 0)),
        scratch_shapes=[pltpu.VMEM((CH, 512), f32), pltpu.VMEM((32, CH), f32),
                        pltpu.VMEM((nc, CH, CH), f32), pltpu.VMEM((nc, CH, CH), f32),
                        pltpu.VMEM((n_win + 1, CH, CH), f32), pltpu.VMEM((n_win + 1, CH, CH), f32),
                        pltpu.VMEM((G_C, CH, CH), f32), pltpu.VMEM((G_C, nc, CH, CH), f32),
                        pltpu.VMEM((H_C, nc, CH, CH), f32), pltpu.VMEM((G_C, HEAD_DIM, R_C * CH), f32),
                        pltpu.VMEM((512, CH), f32)])
    return pl.pallas_call(
        functools.partial(_nsa_sample_kernel, n_pages=n_pages, n_sel=n_sel, n_win=n_win),
        grid_spec=grid_spec, out_shape=jax.ShapeDtypeStruct((n, t_new, 512), f32),
        compiler_params=_cparams(1),
    )(page_table, qc, ft, kvc4, kvw, win_buf, kc, kct, cmpb, cover, *([nsa_pool] * n_pages), toep)


def _conv_kernel(glu_ref, hist_ref, cw_ref, cb_ref, lg_ref, lb_ref, wob_ref, ob_ref, rows_ref, ext_scr, *, tm, stride):
    hp = ext_scr.shape[0] - tm
    keep = (CONV_W - 1) * stride
    i = pl.program_id(1)

    @pl.when(i == 0)
    def _():
        ext_scr[0:hp, :] = hist_ref[0]

    glu = glu_ref[0]
    ext_scr[hp:hp + tm, :] = glu[:, 0:D_CONV] * jax.nn.sigmoid(glu[:, D_CONV:2 * D_CONV])
    y = jnp.zeros((tm, D_CONV), f32) + cb_ref[...]
    for w in range(CONV_W):
        y = y + ext_scr[hp - keep + w * stride:hp - keep + w * stride + tm, :] * cw_ref[w:w + 1, :]
    mu = jnp.mean(y, axis=-1, keepdims=True)
    var = jnp.mean(jnp.square(y - mu), axis=-1, keepdims=True)
    yn = (y - mu) * lax.rsqrt(var + EPS) * lg_ref[...] + lb_ref[...]
    act = (yn * jax.nn.sigmoid(yn)).astype(bf16)
    ob_ref[0] = _dot(act, wob_ref[...])
    tail = ext_scr[hp + tm - keep:hp + tm, :]
    rows_ref[0] = tail
    ext_scr[hp - keep:hp, :] = tail


def _conv_module(glu, hist, cw, cb, lg, lb, wob, tm, stride):
    nb, t, _ = glu.shape
    hp = hist.shape[1]
    keep = (CONV_W - 1) * stride
    d = wob.shape[1]
    return pl.pallas_call(
        functools.partial(_conv_kernel, tm=tm, stride=stride),
        grid=(nb, t // tm),
        in_specs=[pl.BlockSpec((1, tm, 2 * D_CONV), lambda n, i: (n, i, 0)),
                  pl.BlockSpec((1, hp, D_CONV), lambda n, i: (n, 0, 0)),
                  _const_spec(cw), _const_spec(cb), _const_spec(lg), _const_spec(lb), _const_spec(wob)],
        out_specs=[pl.BlockSpec((1, tm, d), lambda n, i: (n, i, 0)),
                   pl.BlockSpec((1, keep, D_CONV), lambda n, i: (n, 0, 0))],
        out_shape=[jax.ShapeDtypeStruct((nb, t, d), f32), jax.ShapeDtypeStruct((nb, keep, D_CONV), f32)],
        scratch_shapes=[pltpu.VMEM((hp + tm, D_CONV), f32)],
        compiler_params=_cparams(2),
    )(glu, hist, cw, cb, lg, lb, wob)


def _merge_kernel(x_ref, gt_ref, oa_ref, ob_ref, oc_ref, gm_ref, woa_ref, woc_ref, wo_ref, o_ref):
    d = x_ref.shape[2]
    ya = _dot(oa_ref[0].astype(bf16), woa_ref[...])
    yc = _dot(oc_ref[0].astype(bf16), woc_ref[...])
    gm = gm_ref[0]
    merged = (jax.nn.sigmoid(gm[:, 0:d]) * ya + jax.nn.sigmoid(gm[:, d:2 * d]) * ob_ref[0]
              + jax.nn.sigmoid(gm[:, 2 * d:3 * d]) * yc)
    o_ref[0] = x_ref[0] + gt_ref[0] * _dot(merged.astype(bf16), wo_ref[...])


def _merge(x, gt, oa, ob, oc, gm, woa, woc, wo, tm):
    nb, t, d = x.shape
    tok = lambda w: pl.BlockSpec((1, tm, w), lambda n, i: (n, i, 0))
    return pl.pallas_call(
        _merge_kernel, grid=(nb, t // tm),
        in_specs=[tok(d), _mod_spec(gt, tm), tok(512), tok(d), tok(512), tok(3 * d),
                  _const_spec(woa), _const_spec(woc), _const_spec(wo)],
        out_specs=tok(d), out_shape=jax.ShapeDtypeStruct((nb, t, d), f32),
        compiler_params=_cparams(2),
    )(x, gt, oa, ob, oc, gm, woa, woc, wo)


def _ffn_kernel(x_ref, sh_ref, sc_ref, gt_ref, g_ref, hist_ref, wa_ref, wb_ref, cw_ref, cb_ref, wd_ref,
                o_ref, rows_ref, h_scr, acc_scr, ext_scr, carry_scr, *, tm, stride):
    hp = ext_scr.shape[0] - tm
    i = pl.program_id(1)
    j = pl.program_id(2)

    @pl.when(j == 0)
    def _():
        x = x_ref[0]
        y = x * lax.rsqrt(jnp.mean(x * x, axis=-1, keepdims=True) + EPS) * g_ref[...]
        h_scr[...] = (y * (1.0 + sc_ref[0]) + sh_ref[0]).astype(bf16)
        acc_scr[...] = jnp.zeros(acc_scr.shape, f32)

    @pl.when(i == 0)
    def _():
        ext_scr[0:hp, :] = hist_ref[...]

    @pl.when(i > 0)
    def _():
        ext_scr[0:hp, :] = carry_scr[j]

    hb = h_scr[...]
    a = _dot(hb, wa_ref[...])
    b = _dot(hb, wb_ref[...])
    ext_scr[hp:hp + tm, :] = a
    conv = (ext_scr[hp - 2 * stride:hp - 2 * stride + tm, :] * cw_ref[0:1, :]
            + ext_scr[hp - stride:hp - stride + tm, :] * cw_ref[1:2, :]
            + a * cw_ref[2:3, :] + cb_ref[...])
    act = (jax.nn.gelu(conv, approximate=True) * b).astype(bf16)
    acc_scr[...] += _dot(act, wd_ref[...])
    tail = ext_scr[tm:tm + hp, :]
    carry_scr[j] = tail
    rows_ref[0, 0] = tail

    @pl.when(j == pl.num_programs(2) - 1)
    def _():
        o_ref[0] = x_ref[0] + gt_ref[0] * acc_scr[...]


def _ffn(x, sh, sc, gt, g, hist, wa, wb, cw, cb, wd, tm, tf, stride):
    nb, t, d = x.shape
    dff = wa.shape[1]
    hp = hist.shape[0]
    nj = dff // tf
    tok = pl.BlockSpec((1, tm, d), lambda n, i, j: (n, i, 0))

    def mod_spec(m):
        if m.shape[1] == 1:
            return pl.BlockSpec((1, 1, d), lambda n, i, j: (n, 0, 0))
        return pl.BlockSpec((1, tm, d), lambda n, i, j: (n, i, 0))

    return pl.pallas_call(
        functools.partial(_ffn_kernel, tm=tm, stride=stride),
        grid=(nb, t // tm, nj),
        in_specs=[tok, mod_spec(sh), mod_spec(sc), mod_spec(gt),
                  pl.BlockSpec((1, d), lambda n, i, j: (0, 0)),
                  pl.BlockSpec((hp, tf), lambda n, i, j: (0, j)),
                  pl.BlockSpec((d, tf), lambda n, i, j: (0, j)),
                  pl.BlockSpec((d, tf), lambda n, i, j: (0, j)),
                  pl.BlockSpec((FFN_CONV_W, tf), lambda n, i, j: (0, j)),
                  pl.BlockSpec((1, tf), lambda n, i, j: (0, j)),
                  pl.BlockSpec((tf, d), lambda n, i, j: (j, 0))],
        out_specs=[tok, pl.BlockSpec((1, 1, hp, tf), lambda n, i, j: (n, i, 0, j))],
        out_shape=[jax.ShapeDtypeStruct((nb, t, d), f32), jax.ShapeDtypeStruct((nb, t // tm, hp, dff), f32)],
        scratch_shapes=[pltpu.VMEM((tm, d), bf16), pltpu.VMEM((tm, d), f32), pltpu.VMEM((hp + tm, tf), f32),
                        pltpu.VMEM((nj, hp, tf), f32)],
        compiler_params=_cparams(3),
    )(x, sh, sc, gt, g, hist, wa, wb, cw, cb, wd)


def _final_kernel(x_ref, g_ref, o_ref):
    x = x_ref[0]
    o_ref[0] = x * lax.rsqrt(jnp.mean(x * x, axis=-1, keepdims=True) + EPS) * g_ref[...]


def _final_norm(x, g, tm):
    nb, t, d = x.shape
    tok = pl.BlockSpec((1, tm, d), lambda n, i: (n, i, 0))
    return pl.pallas_call(
        _final_kernel, grid=(nb, t // tm), in_specs=[tok, pl.BlockSpec((1, d), lambda n, i: (0, 0))],
        out_specs=tok, out_shape=jax.ShapeDtypeStruct((nb, t, d), f32), compiler_params=_cparams(2),
    )(x, g)


def _layer_weights(w_in, w_cmp, pe_cmp):
    d = w_in.shape[0]
    offs = np.cumsum([0, H_A * HEAD_DIM, G_A * HEAD_DIM, G_A * HEAD_DIM, H_IDX * D_IDX, H_IDX, D_IDX, 2 * D_CONV,
                      H_C * HEAD_DIM, 6 * G_C * HEAD_DIM, 3 * H_C, 3 * d])
    qa, ka, va, iq, iw, ik, glu, qc, kvc, gc, gm = [w_in[:, offs[k]:offs[k + 1]] for k in range(11)]
    ikp = jnp.pad(ik, ((0, 0), (0, CH - D_IDX)))
    ws = [qa, jnp.concatenate([ka, va], axis=1), iq, ikp, glu, qc, kvc[:, 0:512], kvc[:, 512:768], gm]
    ws = [w.astype(bf16) for w in ws]
    misc = jnp.concatenate([iw, jnp.zeros((d, 8 - H_IDX), f32), gc, jnp.zeros((d, CH - 8 - 3 * H_C), f32)], axis=1)
    wft = jnp.concatenate([va, kvc[:, 384:512], kvc[:, 640:768], misc], axis=1).T.astype(bf16)
    wbd = jnp.zeros((2, COMP_BLOCK, CH, CH), f32)
    for s in range(2):
        blk = w_cmp[s].reshape(COMP_BLOCK, HEAD_DIM, HEAD_DIM)
        for g in range(G_C):
            wbd = wbd.at[s, :, g * HEAD_DIM:(g + 1) * HEAD_DIM, g * HEAD_DIM:(g + 1) * HEAD_DIM].set(blk)
    pe256 = jnp.concatenate([pe_cmp] * G_C, axis=2)
    return ws, wft, wbd.astype(bf16), pe256


def kernel(x_prompt, x_sample, c_prompt, c_sample, cache_dsa_kv, cache_dsa_idx, cache_nsa_kv, state_nsa_win,
           state_conv, state_ffn, page_table, rel_bias, w_mod, b_mod, g_mix, g_ffn, w_in, w_cmp, pe_cmp, conv_w,
           conv_b, ln_g, ln_b, w_oa, w_ob, w_oc, w_o, w_up, ffn_conv_w, ffn_conv_b, w_down, g_final):
    n_p, t_p, d = x_prompt.shape
    n_s, t_s, _ = x_sample.shape
    depth = w_mod.shape[0]
    dff = w_down.shape[1]
    n_pool = cache_dsa_kv.shape[0]
    n_pages = page_table.shape[1]
    past = n_pages * PAGE_SIZE
    assert t_p % CH == 0 and t_s < COMP_STRIDE and (n_s * t_s) % CH == 0 and n_s % 8 == 0
    tm_p = 256
    rows_s = n_s * t_s

    mod = _modulation(jnp.concatenate([c_prompt, c_sample], axis=0), w_mod, b_mod)
    toep = _toeplitz_bias(rel_bias)
    nqb = past // CH + 1
    cmpb = _cmp_bias(rel_bias, max(nqb, t_p // CH))

    idx_pool = cache_dsa_idx
    kv_pool = cache_dsa_kv.reshape(n_pool, depth, PAGE_SIZE, 512)
    nsa_pool = cache_nsa_kv.reshape(n_pool, depth, PAGE_SIZE, 512)

    xp = x_prompt
    xs = jnp.transpose(x_sample, (1, 0, 2)).reshape(1, rows_s, d)
    zeros_conv = jnp.zeros((n_p, 32, D_CONV), f32)
    zeros_ffn = jnp.zeros((8, dff), f32)
    tf = dff // 2

    def to_seq(a):
        return jnp.transpose(a.reshape(t_s, n_s, a.shape[-1]), (1, 0, 2))

    def to_rows(a):
        return jnp.transpose(a, (1, 0, 2)).reshape(1, t_s * n_s, a.shape[-1])

    outs_p = [[] for _ in range(6)]
    outs_s = [[] for _ in range(6)]
    for l in range(depth):
        ws, wft, wbd, pe256 = _layer_weights(w_in[l], w_cmp[l], pe_cmp[l])
        m = [mod[l][:, k * d:(k + 1) * d] for k in range(6)]
        mp = [a[:n_p].reshape(n_p, 1, d) for a in m]
        ms = [jnp.tile(a[n_p:], (t_s, 1)).reshape(1, rows_s, d) for a in m]
        g1 = g_mix[l].reshape(1, d)
        g2 = g_ffn[l].reshape(1, d)
        cw, cb = conv_w[l], conv_b[l].reshape(1, D_CONV)
        lg, lb = ln_g[l].reshape(1, D_CONV), ln_b[l].reshape(1, D_CONV)
        woa, wob, woc, wo = [w[l].astype(bf16) for w in (w_oa, w_ob, w_oc, w_o)]
        wa, wb = w_up[l][:, :dff].astype(bf16), w_up[l][:, dff:].astype(bf16)
        wd = w_down[l].astype(bf16)
        fcw, fcb = ffn_conv_w[l], ffn_conv_b[l].reshape(1, dff)

        qa, kva, iq, ikp, glu, qc, kvc4, kvw, gm, ft = _inproj(xp, mp[0], mp[1], g1, ws, wft, tm_p)
        o_a = _dsa_prompt(qa, iq, ikp, kva, ft, toep)
        kc, kct = _compress_prompt(kvc4, wbd, pe256)
        o_c = _nsa_prompt(qc, ft, kc, kct, cmpb, kvc4, kvw, toep)
        o_b, conv_rows = _conv_module(glu, zeros_conv, cw, cb, lg, lb, wob, tm_p, 1)
        xp = _merge(xp, mp[2], o_a, o_b, o_c, gm, woa, woc, wo, tm_p)
        xp, ffn_rows = _ffn(xp, mp[3], mp[4], mp[5], g2, zeros_ffn, wa, wb, fcw, fcb, wd, 512, tf, 1)
        outs_p[0].append(kva.reshape(n_p, t_p, 2, G_A, HEAD_DIM))
        outs_p[1].append(ikp[:, :, :D_IDX])
        outs_p[2].append(kvc4.reshape(n_p, t_p, 4, G_C, HEAD_DIM))
        outs_p[3].append(kvw[:, t_p - min(WINDOW, t_p):].reshape(n_p, min(WINDOW, t_p), 2, G_C, HEAD_DIM))
        outs_p[4].append(conv_rows)
        outs_p[5].append(ffn_rows[:, -1, 8 - (FFN_CONV_W - 1):])

        qa, kva, iq, ikp, glu, qc, kvc4, kvw, gm, ft = _inproj(xs, ms[0], ms[1], g1, ws, wft, rows_s)
        ft_s = jnp.transpose(jnp.transpose(ft[0], (1, 0, 2)).reshape(FT_ROWS, t_s, n_s), (2, 0, 1))
        qa_s, kva_s, iq_s, ikp_s, qc_s, kvc4_s, kvw_s = [to_seq(a) for a in (qa, kva, iq, ikp, qc, kvc4, kvw)]
        o_a = _dsa_sample(page_table, l, qa_s, iq_s, ft_s, kva_s, ikp_s, idx_pool, kv_pool, toep)
        kc, kct = _compress_sample(page_table, l, nsa_pool, wbd, pe256)
        win_buf = state_nsa_win[:, l].reshape(n_s, -1, 256)
        o_c = _nsa_sample(page_table, l, qc_s, ft_s, kvc4_s, kvw_s, win_buf, kc, kct, cmpb, nsa_pool, toep)
        hist_c = jnp.transpose(state_conv[:, l], (1, 0, 2)).reshape(1, (CONV_W - 1) * n_s, D_CONV)
        o_b, conv_rows = _conv_module(glu, hist_c, cw, cb, lg, lb, wob, rows_s, n_s)
        xs = _merge(xs, ms[2], to_rows(o_a), o_b, to_rows(o_c), gm, woa, woc, wo, rows_s)
        hist_f = jnp.transpose(state_ffn[:, l], (1, 0, 2)).reshape((FFN_CONV_W - 1) * n_s, dff)
        xs, ffn_rows = _ffn(xs, ms[3], ms[4], ms[5], g2, hist_f, wa, wb, fcw, fcb, wd, rows_s, tf, n_s)
        outs_s[0].append(kva_s.reshape(n_s, t_s, 2, G_A, HEAD_DIM))
        outs_s[1].append(ikp_s[:, :, :D_IDX])
        outs_s[2].append(kvc4_s.reshape(n_s, t_s, 4, G_C, HEAD_DIM))
        win_all = jnp.concatenate([win_buf, kvw_s], axis=1)
        keep_w = min(WINDOW, past + t_s)
        outs_s[3].append(win_all[:, win_all.shape[1] - keep_w:].reshape(n_s, keep_w, 2, G_C, HEAD_DIM))
        outs_s[4].append(jnp.transpose(conv_rows[0].reshape(CONV_W - 1, n_s, D_CONV), (1, 0, 2)))
        outs_s[5].append(jnp.transpose(ffn_rows[0, 0].reshape(FFN_CONV_W - 1, n_s, dff), (1, 0, 2)))

    y_p = _final_norm(xp, g_final.reshape(1, d), 512)
    y_s = to_seq(_final_norm(xs, g_final.reshape(1, d), rows_s))
    stk = lambda group: [jnp.stack(rows, axis=1) for rows in group]
    return tuple([y_p, y_s] + stk(outs_p) + stk(outs_s))
```

```python
import functools
import math

import numpy as np
import jax
import jax.numpy as jnp
from jax import lax
from jax.experimental import pallas as pl
from jax.experimental.pallas import tpu as pltpu

HEAD_DIM = 64
H_A, G_A = 8, 4
R_A = H_A // G_A
H_IDX, D_IDX = 4, 64
DSA_TOPK = 256
D_CONV, CONV_W = 512, 31
H_C, G_C = 8, 2
R_C = H_C // G_C
COMP_BLOCK, COMP_STRIDE = 32, 16
SEL_BLOCK, SEL_TOPK = 64, 8
WINDOW = 512
FFN_CONV_W = 3
NUM_BUCKETS, MAX_DISTANCE = 32, 128
FORCE_SCORE = 1.0e4
EPS = 1e-6
PAGE_SIZE = 128

CH = 128
FT_ROWS = 640
VMEM_LIMIT = 56 * 1024 * 1024

f32, bf16, i32 = jnp.float32, jnp.bfloat16, jnp.int32
NEG_INF = float("-inf")


def _cparams(n_axes):
    return pltpu.CompilerParams(dimension_semantics=("arbitrary",) * n_axes, vmem_limit_bytes=VMEM_LIMIT)


def _dot(a, b):
    return jnp.dot(a, b, preferred_element_type=f32)


def _dot_nt(a, b):
    return lax.dot_general(a, b, (((1,), (1,)), ((), ())), preferred_element_type=f32)


def _colsum(x):
    return x.reshape(CH // 8, 8, CH).sum(axis=0)


def _colmax(x):
    return x.reshape(CH // 8, 8, CH).max(axis=0)


def _mod_kernel(c_ref, w_ref, b_ref, o_ref):
    c = c_ref[...]
    a = (c * jax.nn.sigmoid(c)).astype(bf16)
    o_ref[0] = _dot(a, w_ref[0].astype(bf16)) + b_ref[0]


def _modulation(c_all, w_mod, b_mod):
    depth, d, d6 = w_mod.shape
    n = c_all.shape[0]
    tn = 1024
    return pl.pallas_call(
        _mod_kernel,
        grid=(depth, d6 // tn),
        in_specs=[pl.BlockSpec((n, d), lambda l, j: (0, 0)),
                  pl.BlockSpec((1, d, tn), lambda l, j: (l, 0, j)),
                  pl.BlockSpec((1, 1, tn), lambda l, j: (l, 0, j))],
        out_specs=pl.BlockSpec((1, n, tn), lambda l, j: (l, 0, j)),
        out_shape=jax.ShapeDtypeStruct((depth, n, d6), f32),
        compiler_params=_cparams(2),
    )(c_all, w_mod, b_mod.reshape(depth, 1, d6))


def _t5_bucket(n):
    max_exact = NUM_BUCKETS // 2
    nf = jnp.maximum(n, 1).astype(f32)
    large = max_exact + (jnp.log(nf / max_exact) / math.log(MAX_DISTANCE / max_exact)
                         * (NUM_BUCKETS - max_exact)).astype(i32)
    return jnp.where(n < max_exact, n, jnp.minimum(large, NUM_BUCKETS - 1))


def _bias_from_dist(dist, tab_ref, h):
    b = _t5_bucket(jnp.maximum(dist, 0))
    out = jnp.zeros(dist.shape, f32)
    for k in range(NUM_BUCKETS):
        out = jnp.where(b == k, tab_ref[k, h], out)
    return out


def _toep_kernel(tab_ref, o_ref):
    h = pl.program_id(0)
    row = lax.broadcasted_iota(i32, (CH, CH), 0)
    lane = lax.broadcasted_iota(i32, (CH, CH), 1)
    o_ref[0, 0] = _bias_from_dist(jnp.full((CH, CH), 2 * CH, i32), tab_ref, h)
    o_ref[0, 1] = _bias_from_dist(CH + lane - row, tab_ref, h)
    o_ref[0, 2] = _bias_from_dist(lane - row, tab_ref, h)


def _toeplitz_bias(rel_bias):
    nh = rel_bias.shape[1]
    return pl.pallas_call(
        _toep_kernel,
        grid=(nh,),
        in_specs=[pl.BlockSpec(memory_space=pltpu.SMEM)],
        out_specs=pl.BlockSpec((1, 3, CH, CH), lambda h: (h, 0, 0, 0)),
        out_shape=jax.ShapeDtypeStruct((nh, 3, CH, CH), f32),
        compiler_params=_cparams(1),
    )(rel_bias)


def _cmpbias_kernel(tab_ref, o_ref):
    h = pl.program_id(0)
    qb = pl.program_id(1)
    row = lax.broadcasted_iota(i32, (CH, CH), 0)
    lane = lax.broadcasted_iota(i32, (CH, CH), 1)
    dist = qb * CH + lane - (row * COMP_STRIDE + COMP_BLOCK - 1)
    o_ref[0, 0] = _bias_from_dist(dist, tab_ref, H_A + h)


def _cmp_bias(rel_bias, nqb):
    return pl.pallas_call(
        _cmpbias_kernel,
        grid=(H_C, nqb),
        in_specs=[pl.BlockSpec(memory_space=pltpu.SMEM)],
        out_specs=pl.BlockSpec((1, 1, CH, CH), lambda h, q: (h, q, 0, 0)),
        out_shape=jax.ShapeDtypeStruct((H_C, nqb, CH, CH), f32),
        compiler_params=_cparams(2),
    )(rel_bias)


def _inproj_kernel(x_ref, sh_ref, sc_ref, g_ref, *refs, n_w):
    w_refs = refs[:n_w]
    wft_ref = refs[n_w]
    out_refs = refs[n_w + 1:2 * n_w + 1]
    ft_ref = refs[2 * n_w + 1]
    x = x_ref[0]
    y = x * lax.rsqrt(jnp.mean(x * x, axis=-1, keepdims=True) + EPS) * g_ref[...]
    hb = (y * (1.0 + sc_ref[0]) + sh_ref[0]).astype(bf16)
    for w_ref, o_ref in zip(w_refs, out_refs):
        o_ref[0] = _dot(hb, w_ref[...])
    ft = _dot_nt(wft_ref[...], hb)
    for j in range(ft.shape[1] // CH):
        ft_ref[0, j] = ft[:, j * CH:(j + 1) * CH]


def _mod_spec(m, tm):
    if m.shape[1] == 1:
        return pl.BlockSpec((1, 1, m.shape[2]), lambda n, i: (n, 0, 0))
    return pl.BlockSpec((1, tm, m.shape[2]), lambda n, i: (n, i, 0))


def _const_spec(a):
    nd = a.ndim
    return pl.BlockSpec(a.shape, lambda *_: (0,) * nd, pipeline_mode=pl.Buffered(1))


def _inproj(x, sh, sc, g, ws, wft, tm):
    nb, t, d = x.shape
    n_w = len(ws)
    in_specs = [pl.BlockSpec((1, tm, d), lambda n, i: (n, i, 0)), _mod_spec(sh, tm), _mod_spec(sc, tm),
                _const_spec(g)] + [_const_spec(w) for w in ws] + [_const_spec(wft)]
    out_specs = [pl.BlockSpec((1, tm, w.shape[1]), lambda n, i: (n, i, 0)) for w in ws]
    out_specs.append(pl.BlockSpec((1, tm // CH, FT_ROWS, CH), lambda n, i: (n, i, 0, 0)))
    out_shape = [jax.ShapeDtypeStruct((nb, t, w.shape[1]), f32) for w in ws]
    out_shape.append(jax.ShapeDtypeStruct((nb, t // CH, FT_ROWS, CH), f32))
    return pl.pallas_call(
        functools.partial(_inproj_kernel, n_w=n_w),
        grid=(nb, t // tm), in_specs=in_specs, out_specs=out_specs, out_shape=out_shape,
        compiler_params=_cparams(2),
    )(x, sh, sc, g, *ws, wft)


def _sortable(x):
    x = jnp.where(x == 0.0, 0.0, x)
    b = lax.bitcast_convert_type(x, i32)
    return jnp.where(b < 0, b ^ 0x7FFFFFFF, b)


def _head_operand(tile, src_half, dst_half, mask_other):
    if src_half != dst_half:
        tile = pltpu.roll(tile, 64, 1)
    if mask_other:
        lane = lax.broadcasted_iota(i32, tile.shape, 1)
        keep = (lane < 64) if dst_half == 0 else (lane >= 64)
        tile = jnp.where(keep, tile, 0.0)
    return tile.astype(bf16)


def _shr_pow2(x, g):
    return x // g if isinstance(x, int) else jnp.right_shift(x, g.bit_length() - 1)


def _chunk_loop(lo, hi, body, carry, group):
    if isinstance(lo, int) and isinstance(hi, int):
        for c in range(lo, hi):
            carry = body(c, carry)
        return carry

    def gbody(gi, carry):
        for k in range(group):
            carry = body(gi * group + k, carry)
        return carry

    return lax.fori_loop(_shr_pow2(lo, group), _shr_pow2(hi + group - 1, group), gbody, carry)


def _super_loop(lo, hi, body, carry, group):
    if isinstance(lo, int) and isinstance(hi, int):
        c = lo
        while c < hi:
            n = min(group, hi - c)
            carry = body(c, n, carry)
            c += n
        return carry
    return lax.fori_loop(_shr_pow2(lo, group), _shr_pow2(hi + group - 1, group),
                         lambda gi, carry: body(gi * group, group, carry), carry)


ATT_RUN = 4


def _attend(lo, hi, q_stack, get_k, extra_fn, get_vt, s_scr, acc_scr):
    n_groups = len(q_stack)
    rep = q_stack[0].shape[0] // CH
    nh = n_groups * rep

    def p1(c0, n, macc):
        adds = [extra_fn(c0 + k) for k in range(n)]
        macc = list(macc)
        for g in range(n_groups):
            s_all = _dot_nt(get_k(c0, g, n).astype(bf16), q_stack[g])
            for k in range(n):
                for r in range(rep):
                    h = g * rep + r
                    s = s_all[k * CH:(k + 1) * CH, r * CH:(r + 1) * CH] + adds[k](h, g)
                    s_scr[h, c0 + k] = s
                    macc[h] = jnp.maximum(macc[h], _colmax(s))
        return tuple(macc)

    macc = _super_loop(lo, hi, p1, tuple(jnp.full((8, CH), NEG_INF, f32) for _ in range(nh)), ATT_RUN)
    m = []
    for h in range(nh):
        mh = jnp.max(macc[h], axis=0, keepdims=True)
        m.append(jnp.where(mh > NEG_INF, mh, 0.0))
    for g in range(n_groups):
        acc_scr[g] = jnp.zeros(acc_scr.shape[1:], f32)

    def p2(c0, n, lacc):
        lacc = list(lacc)
        for g in range(n_groups):
            rows = []
            for k in range(n):
                ps = []
                for r in range(rep):
                    h = g * rep + r
                    p = jnp.exp(s_scr[h, c0 + k] - m[h])
                    lacc[h] = lacc[h] + _colsum(p)
                    ps.append(p.astype(bf16))
                rows.append(jnp.concatenate(ps, axis=1))
            p_run = rows[0] if n == 1 else jnp.concatenate(rows, axis=0)
            acc_scr[g] += _dot(get_vt(c0, g, n).astype(bf16), p_run)
        return tuple(lacc)

    lacc = _super_loop(lo, hi, p2, tuple(jnp.zeros((8, CH), f32) for _ in range(nh)), ATT_RUN)
    outs = []
    for h in range(nh):
        g, r = divmod(h, rep)
        inv = 1.0 / jnp.maximum(jnp.sum(lacc[h], axis=0, keepdims=True), 1e-30)
        outs.append(acc_scr[g, :, r * CH:(r + 1) * CH] * inv)
    return outs


def _rows(c0, n):
    start = c0 * CH if isinstance(c0, int) else pl.multiple_of(c0 * CH, CH)
    return pl.ds(start, n * CH)


def _vt_run(ref, c0, n, lo, hi):
    tiles = [ref[c0 + k, lo:hi, :] for k in range(n)]
    return tiles[0] if n == 1 else jnp.concatenate(tiles, axis=1)


def _dsa_core(i_abs, nci, topk, pos_bits, n_valid, q_tile, iq_tile, iw_t, get_ik, get_k, get_vt, toep_ref,
              keys_scr, mask_scr, s_scr, acc_scr, ot_scr):
    row = lax.broadcasted_iota(i32, (CH, CH), 0)
    lane = lax.broadcasted_iota(i32, (CH, CH), 1)
    t_pos = i_abs * CH + lane

    iq_stack = jnp.concatenate([_head_operand(iq_tile(h // 2), h % 2, 0, True) for h in range(H_IDX)], axis=0)

    def idx_body(c, carry):
        s_all = _dot_nt(get_ik(c).astype(bf16), iq_stack)
        acc = jnp.zeros((CH, CH), f32)
        for h in range(H_IDX):
            acc = acc + iw_t[h:h + 1, :] * jnp.maximum(s_all[:, h * CH:(h + 1) * CH], 0.0)
        acc = jnp.where(c * CH + row <= t_pos, acc, NEG_INF)
        keys_scr[c] = _sortable(acc)
        return carry

    _chunk_loop(0, nci, idx_body, 0, 4)

    def count(pred_fn):
        def body(c, acc):
            return acc + _colsum(jnp.where(pred_fn(c, keys_scr[c]), 1.0, 0.0))
        acc = _chunk_loop(0, nci, body, jnp.zeros((8, CH), f32), 4)
        return jnp.sum(acc, axis=0, keepdims=True)

    kf = float(topk)
    int_min = jnp.full((1, CH), -2 ** 31, i32)
    c0 = count(lambda c, k: k >= 0)
    cand = jnp.where(c0 >= kf, jnp.zeros((1, CH), i32), int_min)

    def bit_body(j, cand):
        trial = cand | jnp.left_shift(jnp.int32(1), 30 - j)
        cnt = count(lambda c, k: k >= trial)
        return jnp.where(cnt >= kf, trial, cand)

    cand = lax.fori_loop(0, 31, bit_body, cand)
    cnt_ge = count(lambda c, k: k >= cand)
    tied = (cnt_ge > kf) & (lane[0:1, :] < n_valid)
    has_tie = jnp.max(jnp.where(tied, 1.0, 0.0)) > 0.5

    @pl.when(jnp.logical_not(has_tie))
    def _():
        def mask_body(c, carry):
            sel = (keys_scr[c] >= cand) & (c * CH + row <= t_pos)
            mask_scr[c] = jnp.where(sel, 0.0, NEG_INF)
            return carry
        _chunk_loop(0, nci, mask_body, 0, 4)

    @pl.when(has_tie)
    def _():
        need = kf - count(lambda c, k: k > cand)

        def pos_body(j, pcut):
            trial = pcut | jnp.left_shift(jnp.int32(1), pos_bits - 1 - j)
            cnt = count(lambda c, k: (k == cand) & (c * CH + row < trial))
            return jnp.where(cnt < need, trial, pcut)

        pcut = lax.fori_loop(0, pos_bits, pos_body, jnp.zeros((1, CH), i32))

        def mask_body(c, carry):
            k = keys_scr[c]
            s_pos = c * CH + row
            sel = ((k > cand) | ((k == cand) & (s_pos <= pcut))) & (s_pos <= t_pos)
            mask_scr[c] = jnp.where(sel, 0.0, NEG_INF)
            return carry
        _chunk_loop(0, nci, mask_body, 0, 4)

    scale = HEAD_DIM ** -0.5
    q_stack = []
    for g in range(G_A):
        heads = [_head_operand(q_tile((g * R_A + r) // 2) * scale, (g * R_A + r) % 2, g % 2, True) for r in range(R_A)]
        q_stack.append(jnp.concatenate(heads, axis=0))

    def extra_fn(c):
        bidx = jnp.clip(c - i_abs + 2, 0, 2)
        mk = mask_scr[c]
        return lambda h, g: toep_ref[h, bidx] + mk

    outs = _attend(0, nci, q_stack, lambda c0, g, n: get_k(c0, g // 2, n), extra_fn, get_vt, s_scr, acc_scr)
    for h in range(H_A):
        ot_scr[h * HEAD_DIM:(h + 1) * HEAD_DIM, :] = outs[h]
    return ot_scr[...].T


def _dsa_prompt_kernel(q_ref, iq_ref, misc_ref, ik_ref, k_ref, vt_ref, toep_ref, o_ref,
                       keys_scr, mask_scr, s_scr, acc_scr, ot_scr, *, topk, pos_bits):
    i = pl.program_id(1)

    def rows(c):
        return pl.ds(pl.multiple_of(c * CH, CH), CH)

    o_ref[0] = _dsa_core(
        i, i + 1, topk, pos_bits, CH,
        lambda j: q_ref[0, :, j * CH:(j + 1) * CH],
        lambda j: iq_ref[0, :, j * CH:(j + 1) * CH],
        misc_ref[0, 0, 0:8, :],
        lambda c: ik_ref[0, rows(c), :],
        lambda c0, j, n: k_ref[0, _rows(c0, n), j * CH:(j + 1) * CH],
        lambda c0, g, n: _vt_run(vt_ref.at[0], c0, n, g * HEAD_DIM, (g + 1) * HEAD_DIM),
        toep_ref, keys_scr, mask_scr, s_scr, acc_scr, ot_scr)


def _dsa_prompt(qa, iq, ikp, kva, ft, toep):
    n, t, _ = qa.shape
    nc = t // CH
    topk = min(DSA_TOPK, t // 4)
    assert nc % 4 == 0, "traced chunk loops walk aligned groups of up to 4 chunks"
    kern = functools.partial(_dsa_prompt_kernel, topk=topk, pos_bits=max(1, (t - 1).bit_length()))
    return pl.pallas_call(
        kern,
        grid=(n, nc),
        in_specs=[pl.BlockSpec((1, CH, 512), lambda b, i: (b, i, 0)),
                  pl.BlockSpec((1, CH, 256), lambda b, i: (b, i, 0)),
                  pl.BlockSpec((1, 1, CH, CH), lambda b, i: (b, i, 4, 0)),
                  pl.BlockSpec((1, t, CH), lambda b, i: (b, 0, 0)),
                  pl.BlockSpec((1, t, 256), lambda b, i: (b, 0, 0)),
                  pl.BlockSpec((1, nc, 256, CH), lambda b, i: (b, 0, 0, 0)),
                  pl.BlockSpec((H_A, 3, CH, CH), lambda b, i: (0, 0, 0, 0))],
        out_specs=pl.BlockSpec((1, CH, 512), lambda b, i: (b, i, 0)),
        out_shape=jax.ShapeDtypeStruct((n, t, 512), f32),
        scratch_shapes=[pltpu.VMEM((nc, CH, CH), i32), pltpu.VMEM((nc, CH, CH), f32),
                        pltpu.VMEM((H_A, nc, CH, CH), f32), pltpu.VMEM((G_A, HEAD_DIM, R_A * CH), f32),
                        pltpu.VMEM((512, CH), f32)],
        compiler_params=_cparams(2),
    )(qa, iq, ft, ikp, kva, ft, toep)


def _pad_rows(ref, x):
    ref[...] = jnp.zeros(ref.shape, f32)
    ref[0:x.shape[0], :] = x


def _dsa_sample_kernel(pt_ref, q_ref, iq_ref, ft_ref, kvn_ref, ikn_ref, *refs, n_pages, topk, pos_bits):
    idx_pages = refs[:n_pages]
    kv_pages = refs[n_pages:2 * n_pages]
    (toep_ref, o_ref, q_scr, iq_scr, iw_scr, ik_scr, k_scr, vt_scr, keys_scr, mask_scr, s_scr, acc_scr,
     ot_scr) = refs[2 * n_pages:]
    t_new = q_ref.shape[1]
    for p in range(n_pages):
        ikt = idx_pages[p][0, 0]
        ik_scr[p] = jnp.concatenate([ikt, jnp.zeros((CH - D_IDX, CH), f32)], axis=0).T
        page = kv_pages[p][0, 0]
        k_scr[p * CH:(p + 1) * CH, :] = page[0:256, :].T
        vt_scr[p] = page[256:512, :]
    _pad_rows(ik_scr.at[n_pages], ikn_ref[0])
    _pad_rows(k_scr.at[n_pages * CH:(n_pages + 1) * CH], kvn_ref[0][:, 0:256])
    vt_scr[n_pages] = jnp.zeros((256, CH), f32)
    vt_scr[n_pages, :, 0:t_new] = ft_ref[0, 0:256, :]
    _pad_rows(q_scr, q_ref[0])
    _pad_rows(iq_scr, iq_ref[0])
    iw_scr[...] = jnp.zeros((8, CH), f32)
    iw_scr[0:H_IDX, 0:t_new] = ft_ref[0, 512:512 + H_IDX, :]
    o = _dsa_core(
        n_pages, n_pages + 1, topk, pos_bits, t_new,
        lambda j: q_scr[:, j * CH:(j + 1) * CH],
        lambda j: iq_scr[:, j * CH:(j + 1) * CH],
        iw_scr[...],
        lambda c: ik_scr[c],
        lambda c0, j, n: k_scr[_rows(c0, n), j * CH:(j + 1) * CH],
        lambda c0, g, n: _vt_run(vt_scr, c0, n, g * HEAD_DIM, (g + 1) * HEAD_DIM),
        toep_ref, keys_scr, mask_scr, s_scr, acc_scr, ot_scr)
    o_ref[0] = o[0:t_new]


def _page_specs(n_pages, layer, rows):
    return [pl.BlockSpec((1, 1, rows, PAGE_SIZE), lambda b, pt, p=p: (pt[b, p], layer, 0, 0)) for p in range(n_pages)]


def _dsa_sample(page_table, layer, qa, iq, ft, kva, ikp, idx_pool, kv_pool, toep):
    n, t_new, _ = qa.shape
    n_pages = page_table.shape[1]
    nc = n_pages + 1
    lk = n_pages * PAGE_SIZE + t_new
    topk = min(DSA_TOPK, lk // 4)
    kern = functools.partial(_dsa_sample_kernel, n_pages=n_pages, topk=topk, pos_bits=(nc * CH - 1).bit_length())
    seq = lambda w: pl.BlockSpec((1, t_new, w), lambda b, pt: (b, 0, 0))
    grid_spec = pltpu.PrefetchScalarGridSpec(
        num_scalar_prefetch=1, grid=(n,),
        in_specs=[seq(512), seq(256), pl.BlockSpec((1, FT_ROWS, t_new), lambda b, pt: (b, 0, 0)), seq(512), seq(CH)]
        + _page_specs(n_pages, layer, D_IDX) + _page_specs(n_pages, layer, 512)
        + [pl.BlockSpec((H_A, 3, CH, CH), lambda b, pt: (0, 0, 0, 0))],
        out_specs=pl.BlockSpec((1, t_new, 512), lambda b, pt: (b, 0, 0)),
        scratch_shapes=[pltpu.VMEM((CH, 512), f32), pltpu.VMEM((CH, 256), f32), pltpu.VMEM((8, CH), f32),
                        pltpu.VMEM((nc, CH, CH), f32), pltpu.VMEM((nc * CH, 256), f32), pltpu.VMEM((nc, 256, CH), f32),
                        pltpu.VMEM((nc, CH, CH), i32), pltpu.VMEM((nc, CH, CH), f32),
                        pltpu.VMEM((H_A, nc, CH, CH), f32), pltpu.VMEM((G_A, HEAD_DIM, R_A * CH), f32),
                        pltpu.VMEM((512, CH), f32)])
    return pl.pallas_call(
        kern, grid_spec=grid_spec, out_shape=jax.ShapeDtypeStruct((n, t_new, 512), f32),
        compiler_params=_cparams(1),
    )(page_table, qa, iq, ft, kva, ikp, *([idx_pool] * n_pages), *([kv_pool] * n_pages), toep)


def _compress_core(xk_ref, xv_ref, wbd_ref, pe_ref, kc_ref, kct_ref):
    nr = xk_ref.shape[0] // COMP_STRIDE
    halves = []
    for s, x_ref in enumerate((xk_ref, xv_ref)):
        a0 = jnp.zeros((nr, CH), f32)
        a1 = jnp.zeros((nr, CH), f32)
        for r in range(COMP_STRIDE):
            xr = x_ref[pl.ds(r, nr, stride=COMP_STRIDE), :]
            r2 = COMP_STRIDE + r
            a0 = a0 + _dot((xr + pe_ref[s, r:r + 1, :]).astype(bf16), wbd_ref[s, r])
            a1 = a1 + _dot((xr + pe_ref[s, r2:r2 + 1, :]).astype(bf16), wbd_ref[s, r2])
        kc = a0 + pltpu.roll(a1, nr - 1, 0)
        if nr < CH:
            kc = jnp.concatenate([kc, jnp.zeros((CH - nr, CH), f32)], axis=0)
        halves.append(kc)
    kc_ref[0, :, 0:CH] = halves[0]
    kc_ref[0, :, CH:2 * CH] = halves[1]
    kct_ref[0] = halves[1].T


def _compress_prompt_kernel(xk_ref, xv_ref, wbd_ref, pe_ref, kc_ref, kct_ref):
    _compress_core(xk_ref.at[0], xv_ref.at[0], wbd_ref, pe_ref, kc_ref, kct_ref)


def _compress_sample_kernel(pt_ref, *refs, n_pages):
    pages = refs[:n_pages]
    wbd_ref, pe_ref, kc_ref, kct_ref, xk_scr, xv_scr = refs[n_pages:]
    for p in range(n_pages):
        page = pages[p][0, 0]
        xk_scr[p * PAGE_SIZE:(p + 1) * PAGE_SIZE, :] = page[0:CH, :].T
        xv_scr[p * PAGE_SIZE:(p + 1) * PAGE_SIZE, :] = page[CH:2 * CH, :].T
    _compress_core(xk_scr, xv_scr, wbd_ref, pe_ref, kc_ref, kct_ref)


def _compress_out(n):
    return ([pl.BlockSpec((1, CH, 256), lambda b, *_: (b, 0, 0)), pl.BlockSpec((1, CH, CH), lambda b, *_: (b, 0, 0))],
            [jax.ShapeDtypeStruct((n, CH, 256), f32), jax.ShapeDtypeStruct((n, CH, CH), f32)])


def _compress_prompt(kvc4, wbd, pe256):
    n, t, _ = kvc4.shape
    assert t % (8 * COMP_STRIDE) == 0 and t <= COMP_STRIDE * CH
    out_specs, out_shape = _compress_out(n)
    return pl.pallas_call(
        _compress_prompt_kernel, grid=(n,),
        in_specs=[pl.BlockSpec((1, t, CH), lambda b: (b, 0, 0)), pl.BlockSpec((1, t, CH), lambda b: (b, 0, 1)),
                  _const_spec(wbd), _const_spec(pe256)],
        out_specs=out_specs, out_shape=out_shape, compiler_params=_cparams(1),
    )(kvc4, kvc4, wbd, pe256)


def _compress_sample(page_table, layer, nsa_pool, wbd, pe256):
    n, n_pages = page_table.shape
    assert n_pages * PAGE_SIZE <= COMP_STRIDE * CH
    out_specs, out_shape = _compress_out(n)
    grid_spec = pltpu.PrefetchScalarGridSpec(
        num_scalar_prefetch=1, grid=(n,),
        in_specs=_page_specs(n_pages, layer, 512) + [_const_spec(wbd), _const_spec(pe256)],
        out_specs=out_specs,
        scratch_shapes=[pltpu.VMEM((n_pages * PAGE_SIZE, CH), f32), pltpu.VMEM((n_pages * PAGE_SIZE, CH), f32)])
    return pl.pallas_call(
        functools.partial(_compress_sample_kernel, n_pages=n_pages), grid_spec=grid_spec, out_shape=out_shape,
        compiler_params=_cparams(1),
    )(page_table, *([nsa_pool] * n_pages), wbd, pe256)


def _split3(x):
    hi = x.astype(bf16)
    r1 = x - hi.astype(f32)
    mid = r1.astype(bf16)
    lo = (r1 - mid.astype(f32)).astype(bf16)
    return hi, mid, lo


def _nsa_core(i_abs, n_sel, win_c0, q_tile, gate_t, kc_ref, kct_ref, cmpb_ref, cover_ref, get_selk, get_selvt,
              get_wink, get_winvt, toep_ref, selm_scr, mask_scr, s_scr, acc_scr, ot_scr):
    row = lax.broadcasted_iota(i32, (CH, CH), 0)
    lane = lax.broadcasted_iota(i32, (CH, CH), 1)
    t_pos = i_abs * CH + lane
    scale = HEAD_DIM ** -0.5
    nci = i_abs + 1
    n_chunks_static = mask_scr.shape[1]
    ns_pad = -(-n_sel // 8) * 8

    q_stack = []
    for g in range(G_C):
        heads = [_head_operand(q_tile((g * R_C + r) // 2) * scale, (g * R_C + r) % 2, g, True) for r in range(R_C)]
        q_stack.append(jnp.concatenate(heads, axis=0))

    kc_b = kc_ref[0, :, 0:CH].astype(bf16)
    cmp_valid = t_pos >= row * COMP_STRIDE + COMP_BLOCK - 1
    o_cmp = []
    p_sum = []
    for g in range(G_C):
        s_all = _dot_nt(kc_b, q_stack[g])
        ps = []
        for r in range(R_C):
            h = g * R_C + r
            s = jnp.where(cmp_valid, s_all[:, r * CH:(r + 1) * CH] + cmpb_ref[h, 0], NEG_INF)
            m = jnp.max(s, axis=0, keepdims=True)
            e = jnp.exp(s - jnp.where(m > NEG_INF, m, 0.0))
            ps.append(e / jnp.maximum(jnp.sum(e, axis=0, keepdims=True), 1e-30))
        p_sum.append(ps[0] + ps[1] + ps[2] + ps[3])
        o_all = _dot(kct_ref[0, g * HEAD_DIM:(g + 1) * HEAD_DIM, :].astype(bf16),
                     jnp.concatenate([p.astype(bf16) for p in ps], axis=1))
        o_cmp += [o_all[:, r * CH:(r + 1) * CH] for r in range(R_C)]

    srow = lax.broadcasted_iota(i32, (ns_pad, CH), 0)
    cur = jnp.right_shift(i_abs * CH + lax.broadcasted_iota(i32, (ns_pad, CH), 1), SEL_BLOCK.bit_length() - 1)
    adm = srow <= cur
    forced = (srow == 0) | (srow == cur) | (srow == cur - 1)
    cov = cover_ref[...]
    for g in range(G_C):
        hi, mid, lo = _split3(p_sum[g])
        imp = (_dot(cov, hi) + _dot(cov, mid) + _dot(cov, lo))[0:ns_pad]
        score = jnp.where(adm, imp + jnp.where(forced, FORCE_SCORE, 0.0), NEG_INF)
        rank = jnp.zeros((ns_pad, CH), f32)
        for s2 in range(n_sel):
            other = score[s2:s2 + 1, :]
            ahead = (other > score) | ((other == score) & (s2 < srow))
            rank = rank + jnp.where(ahead, 1.0, 0.0)
        selm_scr[g, 0:ns_pad, :] = jnp.where(adm & (rank < float(SEL_TOPK)) & (srow < n_sel), 1.0, 0.0)
        for c in range(n_chunks_static):
            blk0 = selm_scr[g, 2 * c:2 * c + 1, :]
            blk1 = selm_scr[g, 2 * c + 1:2 * c + 2, :]
            on = jnp.where(row < SEL_BLOCK, blk0, blk1) > 0.5
            mask_scr[g, c] = jnp.where(on & (c * CH + row <= t_pos), 0.0, NEG_INF)

    def slc_extra(c):
        bidx = jnp.clip(c - i_abs + 2, 0, 2)
        return lambda h, g: toep_ref[h, bidx] + mask_scr[g, c]

    o_slc = _attend(0, nci, q_stack, lambda c0, g, n: get_selk(c0, n), slc_extra, get_selvt, s_scr, acc_scr)

    def win_extra(c):
        bidx = jnp.clip(c - i_abs + 2, 0, 2)
        wd = t_pos - (c * CH + row)
        wmask = jnp.where((wd >= 0) & (wd <= WINDOW), 0.0, NEG_INF)
        return lambda h, g: toep_ref[h, bidx] + wmask

    if isinstance(i_abs, int):
        win_lo = max(i_abs - WINDOW // CH, win_c0)
    else:
        win_lo = jnp.maximum(i_abs - WINDOW // CH, win_c0)
    o_win = _attend(win_lo, nci, q_stack, lambda c0, g, n: get_wink(c0, n), win_extra, get_winvt, s_scr, acc_scr)

    for h in range(H_C):
        g0 = jax.nn.sigmoid(gate_t[8 + h:9 + h, :])
        g1 = jax.nn.sigmoid(gate_t[16 + h:17 + h, :])
        g2 = jax.nn.sigmoid(gate_t[24 + h:25 + h, :])
        ot_scr[h * HEAD_DIM:(h + 1) * HEAD_DIM, :] = g0 * o_cmp[h] + g1 * o_slc[h] + g2 * o_win[h]
    return ot_scr[...].T


def _nsa_prompt_kernel(q_ref, misc_ref, kc_ref, kct_ref, cmpb_ref, cover_ref, selk_ref, selvt_ref, wink_ref, winvt_ref,
                       toep_ref, o_ref, selm_scr, mask_scr, s_scr, acc_scr, ot_scr, *, n_sel):
    i = pl.program_id(1)

    def rows(c):
        return pl.ds(pl.multiple_of(c * CH, CH), CH)

    o_ref[0] = _nsa_core(
        i, n_sel, 0,
        lambda j: q_ref[0, :, j * CH:(j + 1) * CH],
        misc_ref[0, 0, 0:32, :],
        kc_ref, kct_ref, cmpb_ref, cover_ref,
        lambda c0, n: selk_ref[0, _rows(c0, n), :],
        lambda c0, g, n: _vt_run(selvt_ref.at[0], c0, n, g * HEAD_DIM, (g + 1) * HEAD_DIM),
        lambda c0, n: wink_ref[0, _rows(c0, n), :],
        lambda c0, g, n: _vt_run(winvt_ref.at[0], c0, n, g * HEAD_DIM, (g + 1) * HEAD_DIM),
        toep_ref, selm_scr, mask_scr, s_scr, acc_scr, ot_scr)


def _cover_matrix(n_keys):
    n_cmp = (n_keys - COMP_BLOCK) // COMP_STRIDE + 1
    n_sel = -(-n_keys // SEL_BLOCK)
    c0 = np.arange(n_cmp)[:, None] * COMP_STRIDE
    s0 = np.arange(n_sel)[None, :] * SEL_BLOCK
    cover = ((c0 < s0 + SEL_BLOCK) & (c0 + COMP_BLOCK > s0)).astype(np.float32)
    out = np.zeros((CH, CH), np.float32)
    out[:n_sel, :n_cmp] = cover.T
    return jnp.asarray(out, dtype=bf16), n_sel


def _nsa_prompt(qc, ft, kc, kct, cmpb, kvc4, kvw, toep):
    n, t, _ = qc.shape
    nc = t // CH
    cover, n_sel = _cover_matrix(t)
    return pl.pallas_call(
        functools.partial(_nsa_prompt_kernel, n_sel=n_sel),
        grid=(n, nc),
        in_specs=[pl.BlockSpec((1, CH, 512), lambda b, i: (b, i, 0)),
                  pl.BlockSpec((1, 1, CH, CH), lambda b, i: (b, i, 4, 0)),
                  pl.BlockSpec((1, CH, 256), lambda b, i: (b, 0, 0)),
                  pl.BlockSpec((1, CH, CH), lambda b, i: (b, 0, 0)),
                  pl.BlockSpec((H_C, 1, CH, CH), lambda b, i: (0, i, 0, 0)),
                  pl.BlockSpec((CH, CH), lambda b, i: (0, 0)),
                  pl.BlockSpec((1, t, CH), lambda b, i: (b, 0, 2)),
                  pl.BlockSpec((1, nc, CH, CH), lambda b, i: (b, 0, 2, 0)),
                  pl.BlockSpec((1, t, CH), lambda b, i: (b, 0, 0)),
                  pl.BlockSpec((1, nc, CH, CH), lambda b, i: (b, 0, 3, 0)),
                  pl.BlockSpec((H_C, 3, CH, CH), lambda b, i: (1, 0, 0, 0))],
        out_specs=pl.BlockSpec((1, CH, 512), lambda b, i: (b, i, 0)),
        out_shape=jax.ShapeDtypeStruct((n, t, 512), f32),
        scratch_shapes=[pltpu.VMEM((G_C, CH, CH), f32), pltpu.VMEM((G_C, nc, CH, CH), f32),
                        pltpu.VMEM((H_C, nc, CH, CH), f32), pltpu.VMEM((G_C, HEAD_DIM, R_C * CH), f32),
                        pltpu.VMEM((512, CH), f32)],
        compiler_params=_cparams(2),
    )(qc, ft, kc, kct, cmpb, cover, kvc4, ft, kvw, ft, toep)


def _nsa_sample_kernel(pt_ref, q_ref, ft_ref, kvn_ref, kvwn_ref, win_ref, kc_ref, kct_ref, cmpb_ref, cover_ref, *refs,
                       n_pages, n_sel, n_win):
    pages = refs[:n_pages]
    (toep_ref, o_ref, q_scr, gate_scr, selk_scr, selvt_scr, wink_scr, winvt_scr,
     selm_scr, mask_scr, s_scr, acc_scr, ot_scr) = refs[n_pages:]
    t_new = q_ref.shape[1]
    for p in range(n_pages):
        page = pages[p][0, 0]
        selk_scr[p * CH:(p + 1) * CH, :] = page[256:384, :].T
        selvt_scr[p] = page[384:512, :]
    _pad_rows(selk_scr.at[n_pages * CH:(n_pages + 1) * CH], kvn_ref[0][:, 256:384])
    selvt_scr[n_pages] = jnp.zeros((CH, CH), f32)
    selvt_scr[n_pages, :, 0:t_new] = ft_ref[0, 256:384, :]
    for w in range(n_win):
        wink_scr[w * CH:(w + 1) * CH, :] = win_ref[0, 0, 0:CH, w * CH:(w + 1) * CH].T
        winvt_scr[w] = win_ref[0, 0, CH:2 * CH, w * CH:(w + 1) * CH]
    _pad_rows(wink_scr.at[n_win * CH:(n_win + 1) * CH], kvwn_ref[0][:, 0:CH])
    winvt_scr[n_win] = jnp.zeros((CH, CH), f32)
    winvt_scr[n_win, :, 0:t_new] = ft_ref[0, 384:512, :]
    _pad_rows(q_scr, q_ref[0])
    gate_scr[...] = jnp.zeros((32, CH), f32)
    gate_scr[:, 0:t_new] = ft_ref[0, 512:544, :]
    win_c0 = n_pages - n_win
    o = _nsa_core(
        n_pages, n_sel, win_c0,
        lambda j: q_scr[:, j * CH:(j + 1) * CH],
        gate_scr[...],
        kc_ref, kct_ref, cmpb_ref, cover_ref,
        lambda c0, n: selk_scr[_rows(c0, n), :],
        lambda c0, g, n: _vt_run(selvt_scr, c0, n, g * HEAD_DIM, (g + 1) * HEAD_DIM),
        lambda c0, n: wink_scr[_rows(c0 - win_c0, n), :],
        lambda c0, g, n: _vt_run(winvt_scr, c0 - win_c0, n, g * HEAD_DIM, (g + 1) * HEAD_DIM),
        toep_ref, selm_scr, mask_scr, s_scr, acc_scr, ot_scr)
    o_ref[0] = o[0:t_new]


def _nsa_sample(page_table, layer, qc, ft, kvc4, kvw, win_buf, kc, kct, cmpb, nsa_pool, toep):
    n, t_new, _ = qc.shape
    n_pages = page_table.shape[1]
    nc = n_pages + 1
    w_buf = win_buf.shape[3]
    assert w_buf % CH == 0 and w_buf <= n_pages * PAGE_SIZE
    n_win = w_buf // CH
    cover, n_sel = _cover_matrix(n_pages * PAGE_SIZE + t_new)
    assert cmpb.shape[1] > n_pages
    seq = lambda w: pl.BlockSpec((1, t_new, w), lambda b, pt: (b, 0, 0))
    grid_spec = pltpu.PrefetchScalarGridSpec(
        num_scalar_prefetch=1, grid=(n,),
        in_specs=[seq(512), pl.BlockSpec((1, FT_ROWS, t_new), lambda b, pt: (b, 0, 0)), seq(512), seq(256),
                  pl.BlockSpec((1, 1, 256, w_buf), lambda b, pt: (b, layer, 0, 0)),
                  pl.BlockSpec((1, CH, 256), lambda b, pt: (b, 0, 0)),
                  pl.BlockSpec((1, CH, CH), lambda b, pt: (b, 0, 0)),
                  pl.BlockSpec((H_C, 1, CH, CH), lambda b, pt: (0, n_pages, 0, 0)),
                  pl.BlockSpec((CH, CH), lambda b, pt: (0, 0))]
        + _page_specs(n_pages, layer, 512)
        + [pl.BlockSpec((H_C, 3, CH, CH), lambda b, pt: (1, 0, 0, 0))],
        out_specs=pl.BlockSpec((1, t_new, 512), lambda b, pt: (b, 0, 0)),
        scratch_shapes=[pltpu.VMEM((CH, 512), f32), pltpu.VMEM((32, CH), f32),
                        pltpu.VMEM((nc * CH, CH), f32), pltpu.VMEM((nc, CH, CH), f32),
                        pltpu.VMEM(((n_win + 1) * CH, CH), f32), pltpu.VMEM((n_win + 1, CH, CH), f32),
                        pltpu.VMEM((G_C, CH, CH), f32), pltpu.VMEM((G_C, nc, CH, CH), f32),
                        pltpu.VMEM((H_C, nc, CH, CH), f32), pltpu.VMEM((G_C, HEAD_DIM, R_C * CH), f32),
                        pltpu.VMEM((512, CH), f32)])
    return pl.pallas_call(
        functools.partial(_nsa_sample_kernel, n_pages=n_pages, n_sel=n_sel, n_win=n_win),
        grid_spec=grid_spec, out_shape=jax.ShapeDtypeStruct((n, t_new, 512), f32),
        compiler_params=_cparams(1),
    )(page_table, qc, ft, kvc4, kvw, win_buf, kc, kct, cmpb, cover, *([nsa_pool] * n_pages), toep)


def _conv_kernel(glu_ref, hist_ref, cw_ref, cb_ref, lg_ref, lb_ref, wob_ref, ob_ref, rows_ref, ext_scr, *, tm, stride):
    hp = ext_scr.shape[0] - tm
    keep = (CONV_W - 1) * stride
    i = pl.program_id(1)

    @pl.when(i == 0)
    def _():
        ext_scr[0:hp, :] = hist_ref[0]

    glu = glu_ref[0]
    ext_scr[hp:hp + tm, :] = glu[:, 0:D_CONV] * jax.nn.sigmoid(glu[:, D_CONV:2 * D_CONV])
    y = jnp.zeros((tm, D_CONV), f32) + cb_ref[...]
    for w in range(CONV_W):
        y = y + ext_scr[hp - keep + w * stride:hp - keep + w * stride + tm, :] * cw_ref[w:w + 1, :]
    mu = jnp.mean(y, axis=-1, keepdims=True)
    var = jnp.mean(jnp.square(y - mu), axis=-1, keepdims=True)
    yn = (y - mu) * lax.rsqrt(var + EPS) * lg_ref[...] + lb_ref[...]
    act = (yn * jax.nn.sigmoid(yn)).astype(bf16)
    ob_ref[0] = _dot(act, wob_ref[...])
    tail = ext_scr[hp + tm - keep:hp + tm, :]
    rows_ref[0] = tail
    ext_scr[hp - keep:hp, :] = tail


def _conv_module(glu, hist, cw, cb, lg, lb, wob, tm, stride):
    nb, t, _ = glu.shape
    hp = hist.shape[1]
    keep = (CONV_W - 1) * stride
    d = wob.shape[1]
    return pl.pallas_call(
        functools.partial(_conv_kernel, tm=tm, stride=stride),
        grid=(nb, t // tm),
        in_specs=[pl.BlockSpec((1, tm, 2 * D_CONV), lambda n, i: (n, i, 0)),
                  pl.BlockSpec((1, hp, D_CONV), lambda n, i: (n, 0, 0)),
                  _const_spec(cw), _const_spec(cb), _const_spec(lg), _const_spec(lb), _const_spec(wob)],
        out_specs=[pl.BlockSpec((1, tm, d), lambda n, i: (n, i, 0)),
                   pl.BlockSpec((1, keep, D_CONV), lambda n, i: (n, 0, 0))],
        out_shape=[jax.ShapeDtypeStruct((nb, t, d), f32), jax.ShapeDtypeStruct((nb, keep, D_CONV), f32)],
        scratch_shapes=[pltpu.VMEM((hp + tm, D_CONV), f32)],
        compiler_params=_cparams(2),
    )(glu, hist, cw, cb, lg, lb, wob)


def _merge_kernel(x_ref, gt_ref, oa_ref, ob_ref, oc_ref, gm_ref, woa_ref, woc_ref, wo_ref, o_ref):
    d = x_ref.shape[2]
    ya = _dot(oa_ref[0].astype(bf16), woa_ref[...])
    yc = _dot(oc_ref[0].astype(bf16), woc_ref[...])
    gm = gm_ref[0]
    merged = (jax.nn.sigmoid(gm[:, 0:d]) * ya + jax.nn.sigmoid(gm[:, d:2 * d]) * ob_ref[0]
              + jax.nn.sigmoid(gm[:, 2 * d:3 * d]) * yc)
    o_ref[0] = x_ref[0] + gt_ref[0] * _dot(merged.astype(bf16), wo_ref[...])


def _merge(x, gt, oa, ob, oc, gm, woa, woc, wo, tm):
    nb, t, d = x.shape
    tok = lambda w: pl.BlockSpec((1, tm, w), lambda n, i: (n, i, 0))
    return pl.pallas_call(
        _merge_kernel, grid=(nb, t // tm),
        in_specs=[tok(d), _mod_spec(gt, tm), tok(512), tok(d), tok(512), tok(3 * d),
                  _const_spec(woa), _const_spec(woc), _const_spec(wo)],
        out_specs=tok(d), out_shape=jax.ShapeDtypeStruct((nb, t, d), f32),
        compiler_params=_cparams(2),
    )(x, gt, oa, ob, oc, gm, woa, woc, wo)


def _ffn_kernel(x_ref, sh_ref, sc_ref, gt_ref, g_ref, hist_ref, wa_ref, wb_ref, cw_ref, cb_ref, wd_ref,
                o_ref, rows_ref, h_scr, acc_scr, ext_scr, carry_scr, *, tm, stride):
    hp = ext_scr.shape[0] - tm
    i = pl.program_id(1)
    j = pl.program_id(2)

    @pl.when(j == 0)
    def _():
        x = x_ref[0]
        y = x * lax.rsqrt(jnp.mean(x * x, axis=-1, keepdims=True) + EPS) * g_ref[...]
        h_scr[...] = (y * (1.0 + sc_ref[0]) + sh_ref[0]).astype(bf16)
        acc_scr[...] = jnp.zeros(acc_scr.shape, f32)

    @pl.when(i == 0)
    def _():
        ext_scr[0:hp, :] = hist_ref[...]

    @pl.when(i > 0)
    def _():
        ext_scr[0:hp, :] = carry_scr[j]

    hb = h_scr[...]
    a = _dot(hb, wa_ref[...])
    b = _dot(hb, wb_ref[...])
    ext_scr[hp:hp + tm, :] = a
    conv = (ext_scr[hp - 2 * stride:hp - 2 * stride + tm, :] * cw_ref[0:1, :]
            + ext_scr[hp - stride:hp - stride + tm, :] * cw_ref[1:2, :]
            + a * cw_ref[2:3, :] + cb_ref[...])
    act = (jax.nn.gelu(conv, approximate=True) * b).astype(bf16)
    acc_scr[...] += _dot(act, wd_ref[...])
    tail = ext_scr[tm:tm + hp, :]
    carry_scr[j] = tail
    rows_ref[0, 0] = tail

    @pl.when(j == pl.num_programs(2) - 1)
    def _():
        o_ref[0] = x_ref[0] + gt_ref[0] * acc_scr[...]


def _ffn(x, sh, sc, gt, g, hist, wa, wb, cw, cb, wd, tm, tf, stride):
    nb, t, d = x.shape
    dff = wa.shape[1]
    hp = hist.shape[0]
    nj = dff // tf
    tok = pl.BlockSpec((1, tm, d), lambda n, i, j: (n, i, 0))

    def mod_spec(m):
        if m.shape[1] == 1:
            return pl.BlockSpec((1, 1, d), lambda n, i, j: (n, 0, 0))
        return pl.BlockSpec((1, tm, d), lambda n, i, j: (n, i, 0))

    return pl.pallas_call(
        functools.partial(_ffn_kernel, tm=tm, stride=stride),
        grid=(nb, t // tm, nj),
        in_specs=[tok, mod_spec(sh), mod_spec(sc), mod_spec(gt),
                  pl.BlockSpec((1, d), lambda n, i, j: (0, 0)),
                  pl.BlockSpec((hp, tf), lambda n, i, j: (0, j)),
                  pl.BlockSpec((d, tf), lambda n, i, j: (0, j)),
                  pl.BlockSpec((d, tf), lambda n, i, j: (0, j)),
                  pl.BlockSpec((FFN_CONV_W, tf), lambda n, i, j: (0, j)),
                  pl.BlockSpec((1, tf), lambda n, i, j: (0, j)),
                  pl.BlockSpec((tf, d), lambda n, i, j: (j, 0))],
        out_specs=[tok, pl.BlockSpec((1, 1, hp, tf), lambda n, i, j: (n, i, 0, j))],
        out_shape=[jax.ShapeDtypeStruct((nb, t, d), f32), jax.ShapeDtypeStruct((nb, t // tm, hp, dff), f32)],
        scratch_shapes=[pltpu.VMEM((tm, d), bf16), pltpu.VMEM((tm, d), f32), pltpu.VMEM((hp + tm, tf), f32),
                        pltpu.VMEM((nj, hp, tf), f32)],
        compiler_params=_cparams(3),
    )(x, sh, sc, gt, g, hist, wa, wb, cw, cb, wd)


def _final_kernel(x_ref, g_ref, o_ref):
    x = x_ref[0]
    o_ref[0] = x * lax.rsqrt(jnp.mean(x * x, axis=-1, keepdims=True) + EPS) * g_ref[...]


def _final_norm(x, g, tm):
    nb, t, d = x.shape
    tok = pl.BlockSpec((1, tm, d), lambda n, i: (n, i, 0))
    return pl.pallas_call(
        _final_kernel, grid=(nb, t // tm), in_specs=[tok, pl.BlockSpec((1, d), lambda n, i: (0, 0))],
        out_specs=tok, out_shape=jax.ShapeDtypeStruct((nb, t, d), f32), compiler_params=_cparams(2),
    )(x, g)


def _layer_weights(w_in, w_cmp, pe_cmp):
    d = w_in.shape[0]
    offs = np.cumsum([0, H_A * HEAD_DIM, G_A * HEAD_DIM, G_A * HEAD_DIM, H_IDX * D_IDX, H_IDX, D_IDX, 2 * D_CONV,
                      H_C * HEAD_DIM, 6 * G_C * HEAD_DIM, 3 * H_C, 3 * d])
    qa, ka, va, iq, iw, ik, glu, qc, kvc, gc, gm = [w_in[:, offs[k]:offs[k + 1]] for k in range(11)]
    ikp = jnp.pad(ik, ((0, 0), (0, CH - D_IDX)))
    ws = [qa, jnp.concatenate([ka, va], axis=1), iq, ikp, glu, qc, kvc[:, 0:512], kvc[:, 512:768], gm]
    ws = [w.astype(bf16) for w in ws]
    misc = jnp.concatenate([iw, jnp.zeros((d, 8 - H_IDX), f32), gc, jnp.zeros((d, CH - 8 - 3 * H_C), f32)], axis=1)
    wft = jnp.concatenate([va, kvc[:, 384:512], kvc[:, 640:768], misc], axis=1).T.astype(bf16)
    wbd = jnp.zeros((2, COMP_BLOCK, CH, CH), f32)
    for s in range(2):
        blk = w_cmp[s].reshape(COMP_BLOCK, HEAD_DIM, HEAD_DIM)
        for g in range(G_C):
            wbd = wbd.at[s, :, g * HEAD_DIM:(g + 1) * HEAD_DIM, g * HEAD_DIM:(g + 1) * HEAD_DIM].set(blk)
    pe256 = jnp.concatenate([pe_cmp] * G_C, axis=2)
    return ws, wft, wbd.astype(bf16), pe256


def kernel(x_prompt, x_sample, c_prompt, c_sample, cache_dsa_kv, cache_dsa_idx, cache_nsa_kv, state_nsa_win,
           state_conv, state_ffn, page_table, rel_bias, w_mod, b_mod, g_mix, g_ffn, w_in, w_cmp, pe_cmp, conv_w,
           conv_b, ln_g, ln_b, w_oa, w_ob, w_oc, w_o, w_up, ffn_conv_w, ffn_conv_b, w_down, g_final):
    n_p, t_p, d = x_prompt.shape
    n_s, t_s, _ = x_sample.shape
    depth = w_mod.shape[0]
    dff = w_down.shape[1]
    n_pool = cache_dsa_kv.shape[0]
    n_pages = page_table.shape[1]
    past = n_pages * PAGE_SIZE
    assert t_p % CH == 0 and t_s < COMP_STRIDE and (n_s * t_s) % CH == 0 and n_s % 8 == 0
    tm_p = 256
    rows_s = n_s * t_s

    mod = _modulation(jnp.concatenate([c_prompt, c_sample], axis=0), w_mod, b_mod)
    toep = _toeplitz_bias(rel_bias)
    nqb = past // CH + 1
    cmpb = _cmp_bias(rel_bias, max(nqb, t_p // CH))

    idx_pool = jnp.transpose(cache_dsa_idx, (0, 1, 3, 2))
    kv_pool = jnp.transpose(cache_dsa_kv, (0, 1, 3, 4, 5, 2)).reshape(n_pool, depth, 512, PAGE_SIZE)
    nsa_pool = jnp.transpose(cache_nsa_kv, (0, 1, 3, 4, 5, 2)).reshape(n_pool, depth, 512, PAGE_SIZE)
    win_t = jnp.transpose(state_nsa_win, (0, 1, 3, 4, 5, 2)).reshape(n_s, depth, 256, -1)

    xp = x_prompt
    xs = jnp.transpose(x_sample, (1, 0, 2)).reshape(1, rows_s, d)
    zeros_conv = jnp.zeros((n_p, 32, D_CONV), f32)
    zeros_ffn = jnp.zeros((8, dff), f32)
    tf = dff // 2

    def to_seq(a):
        return jnp.transpose(a.reshape(t_s, n_s, a.shape[-1]), (1, 0, 2))

    def to_rows(a):
        return jnp.transpose(a, (1, 0, 2)).reshape(1, t_s * n_s, a.shape[-1])

    outs_p = [[] for _ in range(6)]
    outs_s = [[] for _ in range(6)]
    for l in range(depth):
        ws, wft, wbd, pe256 = _layer_weights(w_in[l], w_cmp[l], pe_cmp[l])
        m = [mod[l][:, k * d:(k + 1) * d] for k in range(6)]
        mp = [a[:n_p].reshape(n_p, 1, d) for a in m]
        ms = [jnp.tile(a[n_p:], (t_s, 1)).reshape(1, rows_s, d) for a in m]
        g1 = g_mix[l].reshape(1, d)
        g2 = g_ffn[l].reshape(1, d)
        cw, cb = conv_w[l], conv_b[l].reshape(1, D_CONV)
        lg, lb = ln_g[l].reshape(1, D_CONV), ln_b[l].reshape(1, D_CONV)
        woa, wob, woc, wo = [w[l].astype(bf16) for w in (w_oa, w_ob, w_oc, w_o)]
        wa, wb = w_up[l][:, :dff].astype(bf16), w_up[l][:, dff:].astype(bf16)
        wd = w_down[l].astype(bf16)
        fcw, fcb = ffn_conv_w[l], ffn_conv_b[l].reshape(1, dff)

        qa, kva, iq, ikp, glu, qc, kvc4, kvw, gm, ft = _inproj(xp, mp[0], mp[1], g1, ws, wft, tm_p)
        o_a = _dsa_prompt(qa, iq, ikp, kva, ft, toep)
        kc, kct = _compress_prompt(kvc4, wbd, pe256)
        o_c = _nsa_prompt(qc, ft, kc, kct, cmpb, kvc4, kvw, toep)
        o_b, conv_rows = _conv_module(glu, zeros_conv, cw, cb, lg, lb, wob, tm_p, 1)
        xp = _merge(xp, mp[2], o_a, o_b, o_c, gm, woa, woc, wo, tm_p)
        xp, ffn_rows = _ffn(xp, mp[3], mp[4], mp[5], g2, zeros_ffn, wa, wb, fcw, fcb, wd, 512, tf, 1)
        outs_p[0].append(kva.reshape(n_p, t_p, 2, G_A, HEAD_DIM))
        outs_p[1].append(ikp[:, :, :D_IDX])
        outs_p[2].append(kvc4.reshape(n_p, t_p, 4, G_C, HEAD_DIM))
        outs_p[3].append(kvw[:, t_p - min(WINDOW, t_p):].reshape(n_p, min(WINDOW, t_p), 2, G_C, HEAD_DIM))
        outs_p[4].append(conv_rows)
        outs_p[5].append(ffn_rows[:, -1, 8 - (FFN_CONV_W - 1):])

        qa, kva, iq, ikp, glu, qc, kvc4, kvw, gm, ft = _inproj(xs, ms[0], ms[1], g1, ws, wft, rows_s)
        ft_s = jnp.transpose(jnp.transpose(ft[0], (1, 0, 2)).reshape(FT_ROWS, t_s, n_s), (2, 0, 1))
        qa_s, kva_s, iq_s, ikp_s, qc_s, kvc4_s, kvw_s = [to_seq(a) for a in (qa, kva, iq, ikp, qc, kvc4, kvw)]
        o_a = _dsa_sample(page_table, l, qa_s, iq_s, ft_s, kva_s, ikp_s, idx_pool, kv_pool, toep)
        kc, kct = _compress_sample(page_table, l, nsa_pool, wbd, pe256)
        o_c = _nsa_sample(page_table, l, qc_s, ft_s, kvc4_s, kvw_s, win_t, kc, kct, cmpb, nsa_pool, toep)
        hist_c = jnp.transpose(state_conv[:, l], (1, 0, 2)).reshape(1, (CONV_W - 1) * n_s, D_CONV)
        o_b, conv_rows = _conv_module(glu, hist_c, cw, cb, lg, lb, wob, rows_s, n_s)
        xs = _merge(xs, ms[2], to_rows(o_a), o_b, to_rows(o_c), gm, woa, woc, wo, rows_s)
        hist_f = jnp.transpose(state_ffn[:, l], (1, 0, 2)).reshape((FFN_CONV_W - 1) * n_s, dff)
        xs, ffn_rows = _ffn(xs, ms[3], ms[4], ms[5], g2, hist_f, wa, wb, fcw, fcb, wd, rows_s, tf, n_s)
        outs_s[0].append(kva_s.reshape(n_s, t_s, 2, G_A, HEAD_DIM))
        outs_s[1].append(ikp_s[:, :, :D_IDX])
        outs_s[2].append(kvc4_s.reshape(n_s, t_s, 4, G_C, HEAD_DIM))
        win_all = jnp.concatenate([win_t[:, l], jnp.transpose(kvw_s, (0, 2, 1))], axis=2)
        keep_w = min(WINDOW, past + t_s)
        win_all = win_all[:, :, win_all.shape[2] - keep_w:].reshape(n_s, 2, G_C, HEAD_DIM, keep_w)
        outs_s[3].append(jnp.transpose(win_all, (0, 4, 1, 2, 3)))
        outs_s[4].append(jnp.transpose(conv_rows[0].reshape(CONV_W - 1, n_s, D_CONV), (1, 0, 2)))
        outs_s[5].append(jnp.transpose(ffn_rows[0, 0].reshape(FFN_CONV_W - 1, n_s, dff), (1, 0, 2)))

    y_p = _final_norm(xp, g_final.reshape(1, d), 512)
    y_s = to_seq(_final_norm(xs, g_final.reshape(1, d), rows_s))
    stk = lambda group: [jnp.stack(rows, axis=1) for rows in group]
    return tuple([y_p, y_s] + stk(outs_p) + stk(outs_s))
```

```python
import functools
import math

import numpy as np
import jax
import jax.numpy as jnp
from jax import lax
from jax.experimental import pallas as pl
from jax.experimental.pallas import tpu as pltpu

HEAD_DIM = 64
H_A, G_A = 8, 4
R_A = H_A // G_A
H_IDX, D_IDX = 4, 64
DSA_TOPK = 256
D_CONV, CONV_W = 512, 31
H_C, G_C = 8, 2
R_C = H_C // G_C
COMP_BLOCK, COMP_STRIDE = 32, 16
SEL_BLOCK, SEL_TOPK = 64, 8
WINDOW = 512
FFN_CONV_W = 3
NUM_BUCKETS, MAX_DISTANCE = 32, 128
FORCE_SCORE = 1.0e4
EPS = 1e-6
PAGE_SIZE = 128

CH = 128
FT_ROWS = 640
VMEM_LIMIT = 56 * 1024 * 1024

f32, bf16, i32 = jnp.float32, jnp.bfloat16, jnp.int32
NEG_INF = float("-inf")


def _cparams(n_axes):
    return pltpu.CompilerParams(dimension_semantics=("arbitrary",) * n_axes, vmem_limit_bytes=VMEM_LIMIT)


def _dot(a, b):
    return jnp.dot(a, b, preferred_element_type=f32)


def _dot_nt(a, b):
    return lax.dot_general(a, b, (((1,), (1,)), ((), ())), preferred_element_type=f32)


def _colsum(x):
    return x.reshape(CH // 8, 8, CH).sum(axis=0)


def _colmax(x):
    return x.reshape(CH // 8, 8, CH).max(axis=0)


def _mod_kernel(c_ref, w_ref, b_ref, o_ref):
    c = c_ref[...]
    a = (c * jax.nn.sigmoid(c)).astype(bf16)
    o_ref[0] = _dot(a, w_ref[0].astype(bf16)) + b_ref[0]


def _modulation(c_all, w_mod, b_mod):
    depth, d, d6 = w_mod.shape
    n = c_all.shape[0]
    tn = 1024
    return pl.pallas_call(
        _mod_kernel,
        grid=(depth, d6 // tn),
        in_specs=[pl.BlockSpec((n, d), lambda l, j: (0, 0)),
                  pl.BlockSpec((1, d, tn), lambda l, j: (l, 0, j)),
                  pl.BlockSpec((1, 1, tn), lambda l, j: (l, 0, j))],
        out_specs=pl.BlockSpec((1, n, tn), lambda l, j: (l, 0, j)),
        out_shape=jax.ShapeDtypeStruct((depth, n, d6), f32),
        compiler_params=_cparams(2),
    )(c_all, w_mod, b_mod.reshape(depth, 1, d6))


def _t5_bucket(n):
    max_exact = NUM_BUCKETS // 2
    nf = jnp.maximum(n, 1).astype(f32)
    large = max_exact + (jnp.log(nf / max_exact) / math.log(MAX_DISTANCE / max_exact)
                         * (NUM_BUCKETS - max_exact)).astype(i32)
    return jnp.where(n < max_exact, n, jnp.minimum(large, NUM_BUCKETS - 1))


def _bias_from_dist(dist, tab_ref, h):
    b = _t5_bucket(jnp.maximum(dist, 0))
    out = jnp.zeros(dist.shape, f32)
    for k in range(NUM_BUCKETS):
        out = jnp.where(b == k, tab_ref[k, h], out)
    return out


def _toep_kernel(tab_ref, o_ref):
    h = pl.program_id(0)
    row = lax.broadcasted_iota(i32, (CH, CH), 0)
    lane = lax.broadcasted_iota(i32, (CH, CH), 1)
    o_ref[0, 0] = _bias_from_dist(jnp.full((CH, CH), 2 * CH, i32), tab_ref, h)
    o_ref[0, 1] = _bias_from_dist(CH + lane - row, tab_ref, h)
    o_ref[0, 2] = _bias_from_dist(lane - row, tab_ref, h)


def _toeplitz_bias(rel_bias):
    nh = rel_bias.shape[1]
    return pl.pallas_call(
        _toep_kernel,
        grid=(nh,),
        in_specs=[pl.BlockSpec(memory_space=pltpu.SMEM)],
        out_specs=pl.BlockSpec((1, 3, CH, CH), lambda h: (h, 0, 0, 0)),
        out_shape=jax.ShapeDtypeStruct((nh, 3, CH, CH), f32),
        compiler_params=_cparams(1),
    )(rel_bias)


def _cmpbias_kernel(tab_ref, o_ref):
    h = pl.program_id(0)
    qb = pl.program_id(1)
    row = lax.broadcasted_iota(i32, (CH, CH), 0)
    lane = lax.broadcasted_iota(i32, (CH, CH), 1)
    dist = qb * CH + lane - (row * COMP_STRIDE + COMP_BLOCK - 1)
    o_ref[0, 0] = _bias_from_dist(dist, tab_ref, H_A + h)


def _cmp_bias(rel_bias, nqb):
    return pl.pallas_call(
        _cmpbias_kernel,
        grid=(H_C, nqb),
        in_specs=[pl.BlockSpec(memory_space=pltpu.SMEM)],
        out_specs=pl.BlockSpec((1, 1, CH, CH), lambda h, q: (h, q, 0, 0)),
        out_shape=jax.ShapeDtypeStruct((H_C, nqb, CH, CH), f32),
        compiler_params=_cparams(2),
    )(rel_bias)


def _inproj_kernel(x_ref, sh_ref, sc_ref, g_ref, *refs, n_w):
    w_refs = refs[:n_w]
    wft_ref = refs[n_w]
    out_refs = refs[n_w + 1:2 * n_w + 1]
    ft_ref = refs[2 * n_w + 1]
    x = x_ref[0]
    y = x * lax.rsqrt(jnp.mean(x * x, axis=-1, keepdims=True) + EPS) * g_ref[...]
    hb = (y * (1.0 + sc_ref[0]) + sh_ref[0]).astype(bf16)
    for w_ref, o_ref in zip(w_refs, out_refs):
        o_ref[0] = _dot(hb, w_ref[...])
    ft = _dot_nt(wft_ref[...], hb)
    for j in range(ft.shape[1] // CH):
        ft_ref[0, j] = ft[:, j * CH:(j + 1) * CH]


def _mod_spec(m, tm):
    if m.shape[1] == 1:
        return pl.BlockSpec((1, 1, m.shape[2]), lambda n, i: (n, 0, 0))
    return pl.BlockSpec((1, tm, m.shape[2]), lambda n, i: (n, i, 0))


def _const_spec(a):
    nd = a.ndim
    return pl.BlockSpec(a.shape, lambda *_: (0,) * nd, pipeline_mode=pl.Buffered(1))


def _inproj(x, sh, sc, g, ws, wft, tm):
    nb, t, d = x.shape
    n_w = len(ws)
    in_specs = [pl.BlockSpec((1, tm, d), lambda n, i: (n, i, 0)), _mod_spec(sh, tm), _mod_spec(sc, tm),
                _const_spec(g)] + [_const_spec(w) for w in ws] + [_const_spec(wft)]
    out_specs = [pl.BlockSpec((1, tm, w.shape[1]), lambda n, i: (n, i, 0)) for w in ws]
    out_specs.append(pl.BlockSpec((1, tm // CH, FT_ROWS, CH), lambda n, i: (n, i, 0, 0)))
    out_shape = [jax.ShapeDtypeStruct((nb, t, w.shape[1]), f32) for w in ws]
    out_shape.append(jax.ShapeDtypeStruct((nb, t // CH, FT_ROWS, CH), f32))
    return pl.pallas_call(
        functools.partial(_inproj_kernel, n_w=n_w),
        grid=(nb, t // tm), in_specs=in_specs, out_specs=out_specs, out_shape=out_shape,
        compiler_params=_cparams(2),
    )(x, sh, sc, g, *ws, wft)


def _sortable(x):
    x = jnp.where(x == 0.0, 0.0, x)
    b = lax.bitcast_convert_type(x, i32)
    return jnp.where(b < 0, b ^ 0x7FFFFFFF, b)


def _head_operand(tile, src_half, dst_half, mask_other):
    if src_half != dst_half:
        tile = pltpu.roll(tile, 64, 1)
    if mask_other:
        lane = lax.broadcasted_iota(i32, tile.shape, 1)
        keep = (lane < 64) if dst_half == 0 else (lane >= 64)
        tile = jnp.where(keep, tile, 0.0)
    return tile.astype(bf16)


def _shr_pow2(x, g):
    return x // g if isinstance(x, int) else jnp.right_shift(x, g.bit_length() - 1)


def _chunk_loop(lo, hi, body, carry, group):
    if isinstance(lo, int) and isinstance(hi, int):
        for c in range(lo, hi):
            carry = body(c, carry)
        return carry

    def gbody(gi, carry):
        for k in range(group):
            carry = body(gi * group + k, carry)
        return carry

    return lax.fori_loop(_shr_pow2(lo, group), _shr_pow2(hi + group - 1, group), gbody, carry)


def _super_loop(lo, hi, body, carry, group):
    if isinstance(lo, int) and isinstance(hi, int):
        c = lo
        while c < hi:
            n = min(group, hi - c)
            carry = body(c, n, carry)
            c += n
        return carry
    return lax.fori_loop(_shr_pow2(lo, group), _shr_pow2(hi + group - 1, group),
                         lambda gi, carry: body(gi * group, group, carry), carry)


ATT_RUN = 4


def _attend(lo, hi, q_stack, get_k, extra_fn, get_vt, s_scr, acc_scr):
    n_groups = len(q_stack)
    rep = q_stack[0].shape[0] // CH
    nh = n_groups * rep

    def p1(c0, n, macc):
        adds = [extra_fn(c0 + k) for k in range(n)]
        macc = list(macc)
        for g in range(n_groups):
            s_all = _dot_nt(get_k(c0, g, n).astype(bf16), q_stack[g])
            for k in range(n):
                for r in range(rep):
                    h = g * rep + r
                    s = s_all[k * CH:(k + 1) * CH, r * CH:(r + 1) * CH] + adds[k](h, g)
                    s_scr[h, c0 + k] = s
                    macc[h] = jnp.maximum(macc[h], _colmax(s))
        return tuple(macc)

    macc = _super_loop(lo, hi, p1, tuple(jnp.full((8, CH), NEG_INF, f32) for _ in range(nh)), ATT_RUN)
    m = []
    for h in range(nh):
        mh = jnp.max(macc[h], axis=0, keepdims=True)
        m.append(jnp.where(mh > NEG_INF, mh, 0.0))
    for g in range(n_groups):
        acc_scr[g] = jnp.zeros(acc_scr.shape[1:], f32)

    def p2(c0, n, lacc):
        lacc = list(lacc)
        for g in range(n_groups):
            rows = []
            for k in range(n):
                ps = []
                for r in range(rep):
                    h = g * rep + r
                    p = jnp.exp(s_scr[h, c0 + k] - m[h])
                    lacc[h] = lacc[h] + _colsum(p)
                    ps.append(p.astype(bf16))
                rows.append(jnp.concatenate(ps, axis=1))
            p_run = rows[0] if n == 1 else jnp.concatenate(rows, axis=0)
            acc_scr[g] += _dot(get_vt(c0, g, n).astype(bf16), p_run)
        return tuple(lacc)

    lacc = _super_loop(lo, hi, p2, tuple(jnp.zeros((8, CH), f32) for _ in range(nh)), ATT_RUN)
    outs = []
    for h in range(nh):
        g, r = divmod(h, rep)
        inv = 1.0 / jnp.maximum(jnp.sum(lacc[h], axis=0, keepdims=True), 1e-30)
        outs.append(acc_scr[g, :, r * CH:(r + 1) * CH] * inv)
    return outs


def _rows(c0, n):
    start = c0 * CH if isinstance(c0, int) else pl.multiple_of(c0 * CH, CH)
    return pl.ds(start, n * CH)


def _vt_run(ref, c0, n, lo, hi):
    tiles = [ref[c0 + k, lo:hi, :] for k in range(n)]
    return tiles[0] if n == 1 else jnp.concatenate(tiles, axis=1)


def _dsa_core(i_abs, nci, topk, pos_bits, n_valid, q_tile, iq_tile, iw_t, get_ik, get_k, get_vt, toep_ref,
              keys_scr, mask_scr, s_scr, acc_scr, ot_scr):
    row = lax.broadcasted_iota(i32, (CH, CH), 0)
    lane = lax.broadcasted_iota(i32, (CH, CH), 1)
    t_pos = i_abs * CH + lane

    iq_stack = jnp.concatenate([_head_operand(iq_tile(h // 2), h % 2, 0, True) for h in range(H_IDX)], axis=0)

    def idx_body(c, carry):
        s_all = _dot_nt(get_ik(c).astype(bf16), iq_stack)
        acc = jnp.zeros((CH, CH), f32)
        for h in range(H_IDX):
            acc = acc + iw_t[h:h + 1, :] * jnp.maximum(s_all[:, h * CH:(h + 1) * CH], 0.0)
        acc = jnp.where(c * CH + row <= t_pos, acc, NEG_INF)
        keys_scr[c] = _sortable(acc)
        return carry

    _chunk_loop(0, nci, idx_body, 0, 4)

    def count(pred_fn):
        def body(c, acc):
            return acc + _colsum(jnp.where(pred_fn(c, keys_scr[c]), 1.0, 0.0))
        acc = _chunk_loop(0, nci, body, jnp.zeros((8, CH), f32), 4)
        return jnp.sum(acc, axis=0, keepdims=True)

    kf = float(topk)
    int_min = jnp.full((1, CH), -2 ** 31, i32)
    c0 = count(lambda c, k: k >= 0)
    cand = jnp.where(c0 >= kf, jnp.zeros((1, CH), i32), int_min)

    def bit_body(j, cand):
        trial = cand | jnp.left_shift(jnp.int32(1), 30 - j)
        cnt = count(lambda c, k: k >= trial)
        return jnp.where(cnt >= kf, trial, cand)

    cand = lax.fori_loop(0, 31, bit_body, cand)
    cnt_ge = count(lambda c, k: k >= cand)
    tied = (cnt_ge > kf) & (lane[0:1, :] < n_valid)
    has_tie = jnp.max(jnp.where(tied, 1.0, 0.0)) > 0.5

    @pl.when(jnp.logical_not(has_tie))
    def _():
        def mask_body(c, carry):
            sel = (keys_scr[c] >= cand) & (c * CH + row <= t_pos)
            mask_scr[c] = jnp.where(sel, 0.0, NEG_INF)
            return carry
        _chunk_loop(0, nci, mask_body, 0, 4)

    @pl.when(has_tie)
    def _():
        need = kf - count(lambda c, k: k > cand)

        def pos_body(j, pcut):
            trial = pcut | jnp.left_shift(jnp.int32(1), pos_bits - 1 - j)
            cnt = count(lambda c, k: (k == cand) & (c * CH + row < trial))
            return jnp.where(cnt < need, trial, pcut)

        pcut = lax.fori_loop(0, pos_bits, pos_body, jnp.zeros((1, CH), i32))

        def mask_body(c, carry):
            k = keys_scr[c]
            s_pos = c * CH + row
            sel = ((k > cand) | ((k == cand) & (s_pos <= pcut))) & (s_pos <= t_pos)
            mask_scr[c] = jnp.where(sel, 0.0, NEG_INF)
            return carry
        _chunk_loop(0, nci, mask_body, 0, 4)

    scale = HEAD_DIM ** -0.5
    q_stack = []
    for g in range(G_A):
        heads = [_head_operand(q_tile((g * R_A + r) // 2) * scale, (g * R_A + r) % 2, g % 2, True) for r in range(R_A)]
        q_stack.append(jnp.concatenate(heads, axis=0))

    def extra_fn(c):
        bidx = jnp.clip(c - i_abs + 2, 0, 2)
        mk = mask_scr[c]
        return lambda h, g: toep_ref[h, bidx] + mk

    outs = _attend(0, nci, q_stack, lambda c0, g, n: get_k(c0, g // 2, n), extra_fn, get_vt, s_scr, acc_scr)
    for h in range(H_A):
        ot_scr[h * HEAD_DIM:(h + 1) * HEAD_DIM, :] = outs[h]
    return ot_scr[...].T


def _dsa_prompt_kernel(q_ref, iq_ref, misc_ref, ik_ref, k_ref, vt_ref, toep_ref, o_ref,
                       keys_scr, mask_scr, s_scr, acc_scr, ot_scr, *, topk, pos_bits):
    i = pl.program_id(1)

    def rows(c):
        return pl.ds(pl.multiple_of(c * CH, CH), CH)

    o_ref[0] = _dsa_core(
        i, i + 1, topk, pos_bits, CH,
        lambda j: q_ref[0, :, j * CH:(j + 1) * CH],
        lambda j: iq_ref[0, :, j * CH:(j + 1) * CH],
        misc_ref[0, 0, 0:8, :],
        lambda c: ik_ref[0, rows(c), :],
        lambda c0, j, n: k_ref[0, _rows(c0, n), j * CH:(j + 1) * CH],
        lambda c0, g, n: _vt_run(vt_ref.at[0], c0, n, g * HEAD_DIM, (g + 1) * HEAD_DIM),
        toep_ref, keys_scr, mask_scr, s_scr, acc_scr, ot_scr)


def _dsa_prompt(qa, iq, ikp, kva, ft, toep):
    n, t, _ = qa.shape
    nc = t // CH
    topk = min(DSA_TOPK, t // 4)
    assert nc % 4 == 0, "traced chunk loops walk aligned groups of up to 4 chunks"
    kern = functools.partial(_dsa_prompt_kernel, topk=topk, pos_bits=max(1, (t - 1).bit_length()))
    return pl.pallas_call(
        kern,
        grid=(n, nc),
        in_specs=[pl.BlockSpec((1, CH, 512), lambda b, i: (b, i, 0)),
                  pl.BlockSpec((1, CH, 256), lambda b, i: (b, i, 0)),
                  pl.BlockSpec((1, 1, CH, CH), lambda b, i: (b, i, 4, 0)),
                  pl.BlockSpec((1, t, CH), lambda b, i: (b, 0, 0)),
                  pl.BlockSpec((1, t, 256), lambda b, i: (b, 0, 0)),
                  pl.BlockSpec((1, nc, 256, CH), lambda b, i: (b, 0, 0, 0)),
                  pl.BlockSpec((H_A, 3, CH, CH), lambda b, i: (0, 0, 0, 0))],
        out_specs=pl.BlockSpec((1, CH, 512), lambda b, i: (b, i, 0)),
        out_shape=jax.ShapeDtypeStruct((n, t, 512), f32),
        scratch_shapes=[pltpu.VMEM((nc, CH, CH), i32), pltpu.VMEM((nc, CH, CH), f32),
                        pltpu.VMEM((H_A, nc, CH, CH), f32), pltpu.VMEM((G_A, HEAD_DIM, R_A * CH), f32),
                        pltpu.VMEM((512, CH), f32)],
        compiler_params=_cparams(2),
    )(qa, iq, ft, ikp, kva, ft, toep)


def _pad_rows(ref, x):
    ref[...] = jnp.zeros(ref.shape, f32)
    ref[0:x.shape[0], :] = x


def _rowbias_kernel(tab_ref, o_ref, *, head0, q0, n_heads, t_new, key_mul, key_add):
    c = pl.program_id(0)
    nrow = n_heads * t_new
    row = lax.broadcasted_iota(i32, (nrow, CH), 0)
    lane = lax.broadcasted_iota(i32, (nrow, CH), 1)
    hrow = row // t_new
    b = _t5_bucket(jnp.maximum(q0 + (row - hrow * t_new) - ((c * CH + lane) * key_mul + key_add), 0))
    out = jnp.zeros((nrow, CH), f32)
    for k in range(NUM_BUCKETS):
        for h in range(n_heads):
            out = jnp.where((b == k) & (hrow == h), tab_ref[k, head0 + h], out)
    o_ref[0] = out


def _row_bias(rel_bias, head0, n_heads, t_new, q0, nc, key_mul=1, key_add=0):
    nrow = n_heads * t_new
    return pl.pallas_call(
        functools.partial(_rowbias_kernel, head0=head0, q0=q0, n_heads=n_heads, t_new=t_new, key_mul=key_mul,
                          key_add=key_add),
        grid=(nc,),
        in_specs=[pl.BlockSpec(memory_space=pltpu.SMEM)],
        out_specs=pl.BlockSpec((1, nrow, CH), lambda c: (c, 0, 0)),
        out_shape=jax.ShapeDtypeStruct((nc, nrow, CH), f32),
        compiler_params=_cparams(1),
    )(rel_bias)


def _rowsum(x):
    return jnp.sum(x, axis=1, keepdims=True)


def _dsa_sample_kernel(pt_ref, q_ref, iq_ref, kvn_ref, ikn_ref, *refs, n_pages, topk, pos_bits):
    idx_pages = refs[:n_pages]
    kv_pages = refs[n_pages:2 * n_pages]
    (bias_ref, o_ref, qrow_scr, iqrow_scr, wrow_scr, iknew_scr, knew_scr, vnew_scr,
     keys_scr, mask_scr, s_scr) = refs[2 * n_pages:]
    t_new = q_ref.shape[1]
    nrow = H_A * t_new
    nc = n_pages + 1
    q0 = n_pages * PAGE_SIZE
    lane8 = lax.broadcasted_iota(i32, (8, CH), 1)
    row8 = lax.broadcasted_iota(i32, (8, CH), 0)

    iqrow_scr[...] = jnp.zeros(iqrow_scr.shape, f32)
    wrow_scr[...] = jnp.zeros(wrow_scr.shape, f32)
    ikn = ikn_ref[0]
    for hi in range(H_IDX):
        iqrow_scr[hi * 8:hi * 8 + t_new, 0:D_IDX] = iq_ref[0][:, hi * D_IDX:(hi + 1) * D_IDX]
        wrow_scr[hi * 8:hi * 8 + t_new, :] = jnp.broadcast_to(ikn[:, D_IDX + hi:D_IDX + hi + 1], (t_new, CH))
    qrow_scr[...] = jnp.zeros(qrow_scr.shape, f32)
    scale = HEAD_DIM ** -0.5
    for h in range(H_A):
        g = h // R_A
        qrow_scr[h * t_new:(h + 1) * t_new, g * HEAD_DIM:(g + 1) * HEAD_DIM] = (
            q_ref[0][:, h * HEAD_DIM:(h + 1) * HEAD_DIM] * scale)
    _pad_rows(iknew_scr, ikn)
    _pad_rows(knew_scr, kvn_ref[0][:, 0:256])
    _pad_rows(vnew_scr, kvn_ref[0][:, 256:512])
    iqrows = iqrow_scr[...].astype(bf16)
    qrows = qrow_scr[...].astype(bf16)
    wrow = wrow_scr[...]

    def idx_keys(s, valid):
        s = jnp.maximum(s, 0.0) * wrow
        acc = s[0:8] + s[8:16] + s[16:24] + s[24:32]
        if valid is not None:
            acc = jnp.where(valid, acc, NEG_INF)
        return _sortable(acc)

    zpad = jnp.zeros((CH - D_IDX, CH), f32)
    for p in range(n_pages):
        ikt = jnp.concatenate([idx_pages[p][0, 0], zpad], axis=0).astype(bf16)
        keys_scr[p] = idx_keys(_dot(iqrows, ikt), None)
    keys_scr[n_pages] = idx_keys(_dot_nt(iqrows, iknew_scr[...].astype(bf16)), lane8 <= row8)

    def count(pred_fn):
        acc = jnp.zeros((8, CH), f32)
        for c in range(nc):
            acc = acc + jnp.where(pred_fn(c, keys_scr[c]), 1.0, 0.0)
        return _rowsum(acc)

    kf = float(topk)
    int_min = jnp.full((8, 1), -2 ** 31, i32)
    cand = jnp.where(count(lambda c, k: k >= 0) >= kf, jnp.zeros((8, 1), i32), int_min)

    def bit_body(j, cand):
        trial = cand | jnp.left_shift(jnp.int32(1), 30 - j)
        return jnp.where(count(lambda c, k: k >= trial) >= kf, trial, cand)

    cand = lax.fori_loop(0, 31, bit_body, cand)
    cnt_ge = count(lambda c, k: k >= cand)
    tied = (cnt_ge > kf) & (row8[:, 0:1] < t_new)
    has_tie = jnp.max(jnp.where(tied, 1.0, 0.0)) > 0.5

    def adm(c):
        return (c * CH + lane8 <= q0 + row8) if c == n_pages else None

    @pl.when(jnp.logical_not(has_tie))
    def _():
        for c in range(nc):
            sel = keys_scr[c] >= cand
            if adm(c) is not None:
                sel = sel & adm(c)
            mask_scr[c] = jnp.where(sel, 0.0, NEG_INF)

    @pl.when(has_tie)
    def _():
        need = kf - count(lambda c, k: k > cand)

        def pos_body(j, pcut):
            trial = pcut | jnp.left_shift(jnp.int32(1), pos_bits - 1 - j)
            cnt = count(lambda c, k: (k == cand) & (c * CH + lane8 < trial))
            return jnp.where(cnt < need, trial, pcut)

        pcut = lax.fori_loop(0, pos_bits, pos_body, jnp.zeros((8, 1), i32))
        for c in range(nc):
            k = keys_scr[c]
            sel = (k > cand) | ((k == cand) & (c * CH + lane8 <= pcut))
            if adm(c) is not None:
                sel = sel & adm(c)
            mask_scr[c] = jnp.where(sel, 0.0, NEG_INF)

    def rows_mask(c):
        m8 = mask_scr[c]
        m8 = jnp.where(row8 < t_new, m8, pltpu.roll(m8, t_new, 0))
        return jnp.concatenate([m8] * (nrow // 8), axis=0)

    macc = jnp.full((nrow, CH), NEG_INF, f32)
    for c in range(nc):
        if c < n_pages:
            s = _dot(qrows, kv_pages[c][0, 0][0:256, :].astype(bf16))
        else:
            s = _dot_nt(qrows, knew_scr[...].astype(bf16))
        s = s + bias_ref[c] + rows_mask(c)
        s_scr[c] = s
        macc = jnp.maximum(macc, s)
    m = jnp.max(macc, axis=1, keepdims=True)
    m = jnp.where(m > NEG_INF, m, 0.0)
    lacc = jnp.zeros((nrow, CH), f32)
    oacc = jnp.zeros((nrow, 256), f32)
    for c in range(nc):
        p = jnp.exp(s_scr[c] - m)
        lacc = lacc + p
        if c < n_pages:
            oacc = oacc + _dot_nt(p.astype(bf16), kv_pages[c][0, 0][256:512, :].astype(bf16))
        else:
            oacc = oacc + _dot(p.astype(bf16), vnew_scr[...].astype(bf16))
    o = oacc * (1.0 / jnp.maximum(_rowsum(lacc), 1e-30))
    for h in range(H_A):
        g = h // R_A
        o_ref[0, :, h * HEAD_DIM:(h + 1) * HEAD_DIM] = o[h * t_new:(h + 1) * t_new, g * HEAD_DIM:(g + 1) * HEAD_DIM]


def _page_specs(n_pages, layer, rows):
    return [pl.BlockSpec((1, 1, rows, PAGE_SIZE), lambda b, pt, p=p: (pt[b, p], layer, 0, 0)) for p in range(n_pages)]


def _dsa_sample(page_table, layer, qa, iq, kva, ikp, idx_pool, kv_pool, rowbias):
    n, t_new, _ = qa.shape
    n_pages = page_table.shape[1]
    nc = n_pages + 1
    lk = n_pages * PAGE_SIZE + t_new
    topk = min(DSA_TOPK, lk // 4)
    nrow = H_A * t_new
    assert nrow % 8 == 0 and t_new <= 4 and rowbias.shape == (nc, nrow, CH)
    kern = functools.partial(_dsa_sample_kernel, n_pages=n_pages, topk=topk, pos_bits=(nc * CH - 1).bit_length())
    seq = lambda w: pl.BlockSpec((1, t_new, w), lambda b, pt: (b, 0, 0))
    grid_spec = pltpu.PrefetchScalarGridSpec(
        num_scalar_prefetch=1, grid=(n,),
        in_specs=[seq(512), seq(256), seq(512), seq(CH)]
        + _page_specs(n_pages, layer, D_IDX) + _page_specs(n_pages, layer, 512)
        + [pl.BlockSpec((nc, nrow, CH), lambda b, pt: (0, 0, 0))],
        out_specs=pl.BlockSpec((1, t_new, 512), lambda b, pt: (b, 0, 0)),
        scratch_shapes=[pltpu.VMEM((nrow, 256), f32), pltpu.VMEM((H_IDX * 8, CH), f32), pltpu.VMEM((H_IDX * 8, CH), f32),
                        pltpu.VMEM((CH, CH), f32), pltpu.VMEM((CH, 256), f32), pltpu.VMEM((CH, 256), f32),
                        pltpu.VMEM((nc, 8, CH), i32), pltpu.VMEM((nc, 8, CH), f32), pltpu.VMEM((nc, nrow, CH), f32)])
    return pl.pallas_call(
        kern, grid_spec=grid_spec, out_shape=jax.ShapeDtypeStruct((n, t_new, 512), f32),
        compiler_params=_cparams(1),
    )(page_table, qa, iq, kva, ikp, *([idx_pool] * n_pages), *([kv_pool] * n_pages), rowbias)


def _compress_core(xk_ref, xv_ref, wbd_ref, pe_ref, kc_ref, kct_ref):
    nr = xk_ref.shape[0] // COMP_STRIDE
    halves = []
    for s, x_ref in enumerate((xk_ref, xv_ref)):
        a0 = jnp.zeros((nr, CH), f32)
        a1 = jnp.zeros((nr, CH), f32)
        for r in range(COMP_STRIDE):
            xr = x_ref[pl.ds(r, nr, stride=COMP_STRIDE), :]
            r2 = COMP_STRIDE + r
            a0 = a0 + _dot((xr + pe_ref[s, r:r + 1, :]).astype(bf16), wbd_ref[s, r])
            a1 = a1 + _dot((xr + pe_ref[s, r2:r2 + 1, :]).astype(bf16), wbd_ref[s, r2])
        kc = a0 + pltpu.roll(a1, nr - 1, 0)
        if nr < CH:
            kc = jnp.concatenate([kc, jnp.zeros((CH - nr, CH), f32)], axis=0)
        halves.append(kc)
    kc_ref[0, :, 0:CH] = halves[0]
    kc_ref[0, :, CH:2 * CH] = halves[1]
    kct_ref[0] = halves[1].T


def _compress_prompt_kernel(xk_ref, xv_ref, wbd_ref, pe_ref, kc_ref, kct_ref):
    _compress_core(xk_ref.at[0], xv_ref.at[0], wbd_ref, pe_ref, kc_ref, kct_ref)


def _compress_sample_kernel(pt_ref, *refs, n_pages):
    pages = refs[:n_pages]
    wbd_ref, pe_ref, kc_ref, kct_ref, xk_scr, xv_scr = refs[n_pages:]
    for p in range(n_pages):
        page = pages[p][0, 0]
        xk_scr[p * PAGE_SIZE:(p + 1) * PAGE_SIZE, :] = page[0:CH, :].T
        xv_scr[p * PAGE_SIZE:(p + 1) * PAGE_SIZE, :] = page[CH:2 * CH, :].T
    _compress_core(xk_scr, xv_scr, wbd_ref, pe_ref, kc_ref, kct_ref)


def _compress_out(n):
    return ([pl.BlockSpec((1, CH, 256), lambda b, *_: (b, 0, 0)), pl.BlockSpec((1, CH, CH), lambda b, *_: (b, 0, 0))],
            [jax.ShapeDtypeStruct((n, CH, 256), f32), jax.ShapeDtypeStruct((n, CH, CH), f32)])


def _compress_prompt(kvc4, wbd, pe256):
    n, t, _ = kvc4.shape
    assert t % (8 * COMP_STRIDE) == 0 and t <= COMP_STRIDE * CH
    out_specs, out_shape = _compress_out(n)
    return pl.pallas_call(
        _compress_prompt_kernel, grid=(n,),
        in_specs=[pl.BlockSpec((1, t, CH), lambda b: (b, 0, 0)), pl.BlockSpec((1, t, CH), lambda b: (b, 0, 1)),
                  _const_spec(wbd), _const_spec(pe256)],
        out_specs=out_specs, out_shape=out_shape, compiler_params=_cparams(1),
    )(kvc4, kvc4, wbd, pe256)


def _compress_sample(page_table, layer, nsa_pool, wbd, pe256):
    n, n_pages = page_table.shape
    assert n_pages * PAGE_SIZE <= COMP_STRIDE * CH
    out_specs, out_shape = _compress_out(n)
    grid_spec = pltpu.PrefetchScalarGridSpec(
        num_scalar_prefetch=1, grid=(n,),
        in_specs=_page_specs(n_pages, layer, 512) + [_const_spec(wbd), _const_spec(pe256)],
        out_specs=out_specs,
        scratch_shapes=[pltpu.VMEM((n_pages * PAGE_SIZE, CH), f32), pltpu.VMEM((n_pages * PAGE_SIZE, CH), f32)])
    return pl.pallas_call(
        functools.partial(_compress_sample_kernel, n_pages=n_pages), grid_spec=grid_spec, out_shape=out_shape,
        compiler_params=_cparams(1),
    )(page_table, *([nsa_pool] * n_pages), wbd, pe256)


def _split3(x):
    hi = x.astype(bf16)
    r1 = x - hi.astype(f32)
    mid = r1.astype(bf16)
    lo = (r1 - mid.astype(f32)).astype(bf16)
    return hi, mid, lo


def _nsa_core(i_abs, n_sel, win_c0, q_tile, gate_t, kc_ref, kct_ref, cmpb_ref, cover_ref, get_selk, get_selvt,
              get_wink, get_winvt, toep_ref, selm_scr, mask_scr, s_scr, acc_scr, ot_scr):
    row = lax.broadcasted_iota(i32, (CH, CH), 0)
    lane = lax.broadcasted_iota(i32, (CH, CH), 1)
    t_pos = i_abs * CH + lane
    scale = HEAD_DIM ** -0.5
    nci = i_abs + 1
    n_chunks_static = mask_scr.shape[1]
    ns_pad = -(-n_sel // 8) * 8

    q_stack = []
    for g in range(G_C):
        heads = [_head_operand(q_tile((g * R_C + r) // 2) * scale, (g * R_C + r) % 2, g, True) for r in range(R_C)]
        q_stack.append(jnp.concatenate(heads, axis=0))

    kc_b = kc_ref[0, :, 0:CH].astype(bf16)
    cmp_valid = t_pos >= row * COMP_STRIDE + COMP_BLOCK - 1
    o_cmp = []
    p_sum = []
    for g in range(G_C):
        s_all = _dot_nt(kc_b, q_stack[g])
        ps = []
        for r in range(R_C):
            h = g * R_C + r
            s = jnp.where(cmp_valid, s_all[:, r * CH:(r + 1) * CH] + cmpb_ref[h, 0], NEG_INF)
            m = jnp.max(s, axis=0, keepdims=True)
            e = jnp.exp(s - jnp.where(m > NEG_INF, m, 0.0))
            ps.append(e / jnp.maximum(jnp.sum(e, axis=0, keepdims=True), 1e-30))
        p_sum.append(ps[0] + ps[1] + ps[2] + ps[3])
        o_all = _dot(kct_ref[0, g * HEAD_DIM:(g + 1) * HEAD_DIM, :].astype(bf16),
                     jnp.concatenate([p.astype(bf16) for p in ps], axis=1))
        o_cmp += [o_all[:, r * CH:(r + 1) * CH] for r in range(R_C)]

    srow = lax.broadcasted_iota(i32, (ns_pad, CH), 0)
    cur = jnp.right_shift(i_abs * CH + lax.broadcasted_iota(i32, (ns_pad, CH), 1), SEL_BLOCK.bit_length() - 1)
    adm = srow <= cur
    forced = (srow == 0) | (srow == cur) | (srow == cur - 1)
    cov = cover_ref[...]
    for g in range(G_C):
        hi, mid, lo = _split3(p_sum[g])
        imp = (_dot(cov, hi) + _dot(cov, mid) + _dot(cov, lo))[0:ns_pad]
        score = jnp.where(adm, imp + jnp.where(forced, FORCE_SCORE, 0.0), NEG_INF)
        rank = jnp.zeros((ns_pad, CH), f32)
        for s2 in range(n_sel):
            other = score[s2:s2 + 1, :]
            ahead = (other > score) | ((other == score) & (s2 < srow))
            rank = rank + jnp.where(ahead, 1.0, 0.0)
        selm_scr[g, 0:ns_pad, :] = jnp.where(adm & (rank < float(SEL_TOPK)) & (srow < n_sel), 1.0, 0.0)
        for c in range(n_chunks_static):
            blk0 = selm_scr[g, 2 * c:2 * c + 1, :]
            blk1 = selm_scr[g, 2 * c + 1:2 * c + 2, :]
            on = jnp.where(row < SEL_BLOCK, blk0, blk1) > 0.5
            mask_scr[g, c] = jnp.where(on & (c * CH + row <= t_pos), 0.0, NEG_INF)

    def slc_extra(c):
        bidx = jnp.clip(c - i_abs + 2, 0, 2)
        return lambda h, g: toep_ref[h, bidx] + mask_scr[g, c]

    o_slc = _attend(0, nci, q_stack, lambda c0, g, n: get_selk(c0, n), slc_extra, get_selvt, s_scr, acc_scr)

    def win_extra(c):
        bidx = jnp.clip(c - i_abs + 2, 0, 2)
        wd = t_pos - (c * CH + row)
        wmask = jnp.where((wd >= 0) & (wd <= WINDOW), 0.0, NEG_INF)
        return lambda h, g: toep_ref[h, bidx] + wmask

    if isinstance(i_abs, int):
        win_lo = max(i_abs - WINDOW // CH, win_c0)
    else:
        win_lo = jnp.maximum(i_abs - WINDOW // CH, win_c0)
    o_win = _attend(win_lo, nci, q_stack, lambda c0, g, n: get_wink(c0, n), win_extra, get_winvt, s_scr, acc_scr)

    for h in range(H_C):
        g0 = jax.nn.sigmoid(gate_t[8 + h:9 + h, :])
        g1 = jax.nn.sigmoid(gate_t[16 + h:17 + h, :])
        g2 = jax.nn.sigmoid(gate_t[24 + h:25 + h, :])
        ot_scr[h * HEAD_DIM:(h + 1) * HEAD_DIM, :] = g0 * o_cmp[h] + g1 * o_slc[h] + g2 * o_win[h]
    return ot_scr[...].T


def _nsa_prompt_kernel(q_ref, misc_ref, kc_ref, kct_ref, cmpb_ref, cover_ref, selk_ref, selvt_ref, wink_ref, winvt_ref,
                       toep_ref, o_ref, selm_scr, mask_scr, s_scr, acc_scr, ot_scr, *, n_sel):
    i = pl.program_id(1)

    def rows(c):
        return pl.ds(pl.multiple_of(c * CH, CH), CH)

    o_ref[0] = _nsa_core(
        i, n_sel, 0,
        lambda j: q_ref[0, :, j * CH:(j + 1) * CH],
        misc_ref[0, 0, 0:32, :],
        kc_ref, kct_ref, cmpb_ref, cover_ref,
        lambda c0, n: selk_ref[0, _rows(c0, n), :],
        lambda c0, g, n: _vt_run(selvt_ref.at[0], c0, n, g * HEAD_DIM, (g + 1) * HEAD_DIM),
        lambda c0, n: wink_ref[0, _rows(c0, n), :],
        lambda c0, g, n: _vt_run(winvt_ref.at[0], c0, n, g * HEAD_DIM, (g + 1) * HEAD_DIM),
        toep_ref, selm_scr, mask_scr, s_scr, acc_scr, ot_scr)


def _cover_matrix(n_keys):
    n_cmp = (n_keys - COMP_BLOCK) // COMP_STRIDE + 1
    n_sel = -(-n_keys // SEL_BLOCK)
    c0 = np.arange(n_cmp)[:, None] * COMP_STRIDE
    s0 = np.arange(n_sel)[None, :] * SEL_BLOCK
    cover = ((c0 < s0 + SEL_BLOCK) & (c0 + COMP_BLOCK > s0)).astype(np.float32)
    out = np.zeros((CH, CH), np.float32)
    out[:n_sel, :n_cmp] = cover.T
    return jnp.asarray(out, dtype=bf16), n_sel


def _nsa_prompt(qc, ft, kc, kct, cmpb, kvc4, kvw, toep):
    n, t, _ = qc.shape
    nc = t // CH
    cover, n_sel = _cover_matrix(t)
    return pl.pallas_call(
        functools.partial(_nsa_prompt_kernel, n_sel=n_sel),
        grid=(n, nc),
        in_specs=[pl.BlockSpec((1, CH, 512), lambda b, i: (b, i, 0)),
                  pl.BlockSpec((1, 1, CH, CH), lambda b, i: (b, i, 4, 0)),
                  pl.BlockSpec((1, CH, 256), lambda b, i: (b, 0, 0)),
                  pl.BlockSpec((1, CH, CH), lambda b, i: (b, 0, 0)),
                  pl.BlockSpec((H_C, 1, CH, CH), lambda b, i: (0, i, 0, 0)),
                  pl.BlockSpec((CH, CH), lambda b, i: (0, 0)),
                  pl.BlockSpec((1, t, CH), lambda b, i: (b, 0, 2)),
                  pl.BlockSpec((1, nc, CH, CH), lambda b, i: (b, 0, 2, 0)),
                  pl.BlockSpec((1, t, CH), lambda b, i: (b, 0, 0)),
                  pl.BlockSpec((1, nc, CH, CH), lambda b, i: (b, 0, 3, 0)),
                  pl.BlockSpec((H_C, 3, CH, CH), lambda b, i: (1, 0, 0, 0))],
        out_specs=pl.BlockSpec((1, CH, 512), lambda b, i: (b, i, 0)),
        out_shape=jax.ShapeDtypeStruct((n, t, 512), f32),
        scratch_shapes=[pltpu.VMEM((G_C, CH, CH), f32), pltpu.VMEM((G_C, nc, CH, CH), f32),
                        pltpu.VMEM((H_C, nc, CH, CH), f32), pltpu.VMEM((G_C, HEAD_DIM, R_C * CH), f32),
                        pltpu.VMEM((512, CH), f32)],
        compiler_params=_cparams(2),
    )(qc, ft, kc, kct, cmpb, cover, kvc4, ft, kvw, ft, toep)


def _rows_softmax_pv(n, score_fn, pv_fn, s_scr, nrow):
    macc = jnp.full((nrow, CH), NEG_INF, f32)
    for i in range(n):
        s = score_fn(i)
        s_scr[i] = s
        macc = jnp.maximum(macc, s)
    m = jnp.max(macc, axis=1, keepdims=True)
    m = jnp.where(m > NEG_INF, m, 0.0)
    lacc = jnp.zeros((nrow, CH), f32)
    oacc = jnp.zeros((nrow, CH), f32)
    for i in range(n):
        p = jnp.exp(s_scr[i] - m)
        lacc = lacc + p
        oacc = oacc + pv_fn(i, p.astype(bf16))
    return oacc * (1.0 / jnp.maximum(_rowsum(lacc), 1e-30))


def _nsa_sample_kernel(pt_ref, q_ref, ikn_ref, kvn_ref, kvwn_ref, win_ref, kc_ref, cover_ref, grp_ref, bias_ref,
                       cmpb_ref, *refs, n_pages, n_sel, n_win):
    pages = refs[:n_pages]
    o_ref, qrow_scr, gate_scr, knew_scr, vnew_scr, wknew_scr, wvnew_scr, s_scr = refs[n_pages:]
    t_new = q_ref.shape[1]
    nrow = H_C * t_new
    nc = n_pages + 1
    q0 = n_pages * PAGE_SIZE
    win_c0 = n_pages - n_win
    scale = HEAD_DIM ** -0.5
    row = lax.broadcasted_iota(i32, (nrow, CH), 0)
    lane = lax.broadcasted_iota(i32, (nrow, CH), 1)
    t_pos = q0 + row - (row // t_new) * t_new
    row8 = lax.broadcasted_iota(i32, (8, CH), 0)
    lane8 = lax.broadcasted_iota(i32, (8, CH), 1)

    qrow_scr[...] = jnp.zeros(qrow_scr.shape, f32)
    ikn = ikn_ref[0]
    gate_col0 = D_IDX + H_IDX
    for h in range(H_C):
        g = h // R_C
        qrow_scr[h * t_new:(h + 1) * t_new, g * HEAD_DIM:(g + 1) * HEAD_DIM] = (
            q_ref[0][:, h * HEAD_DIM:(h + 1) * HEAD_DIM] * scale)
        for br in range(3):
            col = gate_col0 + br * H_C + h
            gate_scr[br, h * t_new:(h + 1) * t_new, :] = jnp.broadcast_to(ikn[:, col:col + 1], (t_new, CH))
    qrows = qrow_scr[...].astype(bf16)
    _pad_rows(knew_scr, kvn_ref[0][:, 256:384])
    _pad_rows(vnew_scr, kvn_ref[0][:, 384:512])
    _pad_rows(wknew_scr, kvwn_ref[0][:, 0:CH])
    _pad_rows(wvnew_scr, kvwn_ref[0][:, CH:2 * CH])

    s = _dot_nt(qrows, kc_ref[0, :, 0:CH].astype(bf16)) + cmpb_ref[0]
    s = jnp.where(t_pos >= lane * COMP_STRIDE + COMP_BLOCK - 1, s, NEG_INF)
    m = jnp.max(s, axis=1, keepdims=True)
    e = jnp.exp(s - jnp.where(m > NEG_INF, m, 0.0))
    p_cmp = e / jnp.maximum(_rowsum(e), 1e-30)
    o_cmp = _dot(p_cmp.astype(bf16), kc_ref[0, :, CH:2 * CH].astype(bf16))

    grp = grp_ref[...]
    cov = cover_ref[...]
    p_sum = sum(_dot(grp, part) for part in _split3(p_cmp))
    imp = sum(_dot(part, cov) for part in _split3(p_sum))
    cur = jnp.right_shift(q0 + row8 - (row8 // t_new) * t_new, SEL_BLOCK.bit_length() - 1)
    adm = lane8 <= cur
    forced = (lane8 == 0) | (lane8 == cur) | (lane8 == cur - 1)
    score = jnp.where(adm, imp + jnp.where(forced, FORCE_SCORE, 0.0), NEG_INF)
    rank = jnp.zeros((8, CH), f32)
    for s2 in range(n_sel):
        other = score[:, s2:s2 + 1]
        rank = rank + jnp.where((other > score) | ((other == score) & (s2 < lane8)), 1.0, 0.0)
    selm = jnp.where(adm & (rank < float(SEL_TOPK)) & (lane8 < n_sel), 1.0, 0.0)
    swapped = pltpu.roll(selm, t_new, 0)
    tiles = [jnp.where(row8 < t_new, selm, swapped), jnp.where(row8 < t_new, swapped, selm)]
    selm_rows = jnp.concatenate([tiles[(8 * i // t_new) // R_C] for i in range(nrow // 8)], axis=0)

    def slc_mask(c):
        on = jnp.where(lane < SEL_BLOCK, selm_rows[:, 2 * c:2 * c + 1], selm_rows[:, 2 * c + 1:2 * c + 2]) > 0.5
        if c == n_pages:
            on = on & (c * CH + lane <= t_pos)
        return jnp.where(on, 0.0, NEG_INF)

    def slc_score(c):
        if c < n_pages:
            s = _dot(qrows, pages[c][0, 0][256:384, :].astype(bf16))
        else:
            s = _dot_nt(qrows, knew_scr[...].astype(bf16))
        return s + bias_ref[c] + slc_mask(c)

    def slc_pv(c, p):
        if c < n_pages:
            return _dot_nt(p, pages[c][0, 0][384:512, :].astype(bf16))
        return _dot(p, vnew_scr[...].astype(bf16))

    o_slc = _rows_softmax_pv(nc, slc_score, slc_pv, s_scr, nrow)

    def win_score(w):
        c = win_c0 + w
        if w < n_win:
            s = _dot(qrows, win_ref[0, 0, 0:CH, w * CH:(w + 1) * CH].astype(bf16))
        else:
            s = _dot_nt(qrows, wknew_scr[...].astype(bf16))
        wd = t_pos - (c * CH + lane)
        return jnp.where((wd >= 0) & (wd <= WINDOW), s + bias_ref[c], NEG_INF)

    def win_pv(w, p):
        if w < n_win:
            return _dot_nt(p, win_ref[0, 0, CH:2 * CH, w * CH:(w + 1) * CH].astype(bf16))
        return _dot(p, wvnew_scr[...].astype(bf16))

    o_win = _rows_softmax_pv(n_win + 1, win_score, win_pv, s_scr, nrow)

    o = (jax.nn.sigmoid(gate_scr[0]) * o_cmp + jax.nn.sigmoid(gate_scr[1]) * o_slc
         + jax.nn.sigmoid(gate_scr[2]) * o_win)
    for h in range(H_C):
        g = h // R_C
        o_ref[0, :, h * HEAD_DIM:(h + 1) * HEAD_DIM] = o[h * t_new:(h + 1) * t_new, g * HEAD_DIM:(g + 1) * HEAD_DIM]


def _nsa_sample(page_table, layer, qc, ikp, kvc4, kvw, win_buf, kc, nsa_pool, rowbias, cmprow):
    n, t_new, _ = qc.shape
    n_pages = page_table.shape[1]
    nc = n_pages + 1
    w_buf = win_buf.shape[3]
    assert w_buf % CH == 0 and w_buf <= n_pages * PAGE_SIZE
    n_win = w_buf // CH
    nrow = H_C * t_new
    assert 8 % t_new == 0 and G_C == 2 and rowbias.shape == (nc, nrow, CH)
    cover, n_sel = _cover_matrix(n_pages * PAGE_SIZE + t_new)
    cover = cover.T
    grp = np.zeros((8, nrow), np.float32)
    for h in range(H_C):
        for j in range(t_new):
            grp[(h // R_C) * t_new + j, h * t_new + j] = 1.0
    grp = jnp.asarray(grp, dtype=bf16)
    seq = lambda w: pl.BlockSpec((1, t_new, w), lambda b, pt: (b, 0, 0))
    grid_spec = pltpu.PrefetchScalarGridSpec(
        num_scalar_prefetch=1, grid=(n,),
        in_specs=[seq(512), seq(CH), seq(512), seq(256),
                  pl.BlockSpec((1, 1, 256, w_buf), lambda b, pt: (b, layer, 0, 0)),
                  pl.BlockSpec((1, CH, 256), lambda b, pt: (b, 0, 0)),
                  pl.BlockSpec((CH, CH), lambda b, pt: (0, 0)),
                  pl.BlockSpec((8, nrow), lambda b, pt: (0, 0)),
                  pl.BlockSpec((nc, nrow, CH), lambda b, pt: (0, 0, 0)),
                  pl.BlockSpec((1, nrow, CH), lambda b, pt: (0, 0, 0))]
        + _page_specs(n_pages, layer, 512),
        out_specs=pl.BlockSpec((1, t_new, 512), lambda b, pt: (b, 0, 0)),
        scratch_shapes=[pltpu.VMEM((nrow, CH), f32), pltpu.VMEM((3, nrow, CH), f32),
                        pltpu.VMEM((CH, CH), f32), pltpu.VMEM((CH, CH), f32),
                        pltpu.VMEM((CH, CH), f32), pltpu.VMEM((CH, CH), f32),
                        pltpu.VMEM((nc, nrow, CH), f32)])
    return pl.pallas_call(
        functools.partial(_nsa_sample_kernel, n_pages=n_pages, n_sel=n_sel, n_win=n_win),
        grid_spec=grid_spec, out_shape=jax.ShapeDtypeStruct((n, t_new, 512), f32),
        compiler_params=_cparams(1),
    )(page_table, qc, ikp, kvc4, kvw, win_buf, kc, cover, grp, rowbias, cmprow, *([nsa_pool] * n_pages))


def _conv_kernel(glu_ref, hist_ref, cw_ref, cb_ref, lg_ref, lb_ref, wob_ref, ob_ref, rows_ref, ext_scr, *, tm, stride):
    hp = ext_scr.shape[0] - tm
    keep = (CONV_W - 1) * stride
    i = pl.program_id(1)

    @pl.when(i == 0)
    def _():
        ext_scr[0:hp, :] = hist_ref[0]

    glu = glu_ref[0]
    ext_scr[hp:hp + tm, :] = glu[:, 0:D_CONV] * jax.nn.sigmoid(glu[:, D_CONV:2 * D_CONV])
    y = jnp.zeros((tm, D_CONV), f32) + cb_ref[...]
    for w in range(CONV_W):
        y = y + ext_scr[hp - keep + w * stride:hp - keep + w * stride + tm, :] * cw_ref[w:w + 1, :]
    mu = jnp.mean(y, axis=-1, keepdims=True)
    var = jnp.mean(jnp.square(y - mu), axis=-1, keepdims=True)
    yn = (y - mu) * lax.rsqrt(var + EPS) * lg_ref[...] + lb_ref[...]
    act = (yn * jax.nn.sigmoid(yn)).astype(bf16)
    ob_ref[0] = _dot(act, wob_ref[...])
    tail = ext_scr[hp + tm - keep:hp + tm, :]
    rows_ref[0] = tail
    ext_scr[hp - keep:hp, :] = tail


def _conv_module(glu, hist, cw, cb, lg, lb, wob, tm, stride):
    nb, t, _ = glu.shape
    hp = hist.shape[1]
    keep = (CONV_W - 1) * stride
    d = wob.shape[1]
    return pl.pallas_call(
        functools.partial(_conv_kernel, tm=tm, stride=stride),
        grid=(nb, t // tm),
        in_specs=[pl.BlockSpec((1, tm, 2 * D_CONV), lambda n, i: (n, i, 0)),
                  pl.BlockSpec((1, hp, D_CONV), lambda n, i: (n, 0, 0)),
                  _const_spec(cw), _const_spec(cb), _const_spec(lg), _const_spec(lb), _const_spec(wob)],
        out_specs=[pl.BlockSpec((1, tm, d), lambda n, i: (n, i, 0)),
                   pl.BlockSpec((1, keep, D_CONV), lambda n, i: (n, 0, 0))],
        out_shape=[jax.ShapeDtypeStruct((nb, t, d), f32), jax.ShapeDtypeStruct((nb, keep, D_CONV), f32)],
        scratch_shapes=[pltpu.VMEM((hp + tm, D_CONV), f32)],
        compiler_params=_cparams(2),
    )(glu, hist, cw, cb, lg, lb, wob)


def _merge_kernel(x_ref, gt_ref, oa_ref, ob_ref, oc_ref, gm_ref, woa_ref, woc_ref, wo_ref, o_ref):
    d = x_ref.shape[2]
    ya = _dot(oa_ref[0].astype(bf16), woa_ref[...])
    yc = _dot(oc_ref[0].astype(bf16), woc_ref[...])
    gm = gm_ref[0]
    merged = (jax.nn.sigmoid(gm[:, 0:d]) * ya + jax.nn.sigmoid(gm[:, d:2 * d]) * ob_ref[0]
              + jax.nn.sigmoid(gm[:, 2 * d:3 * d]) * yc)
    o_ref[0] = x_ref[0] + gt_ref[0] * _dot(merged.astype(bf16), wo_ref[...])


def _merge(x, gt, oa, ob, oc, gm, woa, woc, wo, tm):
    nb, t, d = x.shape
    tok = lambda w: pl.BlockSpec((1, tm, w), lambda n, i: (n, i, 0))
    return pl.pallas_call(
        _merge_kernel, grid=(nb, t // tm),
        in_specs=[tok(d), _mod_spec(gt, tm), tok(512), tok(d), tok(512), tok(3 * d),
                  _const_spec(woa), _const_spec(woc), _const_spec(wo)],
        out_specs=tok(d), out_shape=jax.ShapeDtypeStruct((nb, t, d), f32),
        compiler_params=_cparams(2),
    )(x, gt, oa, ob, oc, gm, woa, woc, wo)


def _ffn_kernel(x_ref, sh_ref, sc_ref, gt_ref, g_ref, hist_ref, wa_ref, wb_ref, cw_ref, cb_ref, wd_ref,
                o_ref, rows_ref, h_scr, acc_scr, ext_scr, carry_scr, *, tm, stride):
    hp = ext_scr.shape[0] - tm
    i = pl.program_id(1)
    j = pl.program_id(2)

    @pl.when(j == 0)
    def _():
        x = x_ref[0]
        y = x * lax.rsqrt(jnp.mean(x * x, axis=-1, keepdims=True) + EPS) * g_ref[...]
        h_scr[...] = (y * (1.0 + sc_ref[0]) + sh_ref[0]).astype(bf16)
        acc_scr[...] = jnp.zeros(acc_scr.shape, f32)

    @pl.when(i == 0)
    def _():
        ext_scr[0:hp, :] = hist_ref[...]

    @pl.when(i > 0)
    def _():
        ext_scr[0:hp, :] = carry_scr[j]

    hb = h_scr[...]
    a = _dot(hb, wa_ref[...])
    b = _dot(hb, wb_ref[...])
    ext_scr[hp:hp + tm, :] = a
    conv = (ext_scr[hp - 2 * stride:hp - 2 * stride + tm, :] * cw_ref[0:1, :]
            + ext_scr[hp - stride:hp - stride + tm, :] * cw_ref[1:2, :]
            + a * cw_ref[2:3, :] + cb_ref[...])
    act = (jax.nn.gelu(conv, approximate=True) * b).astype(bf16)
    acc_scr[...] += _dot(act, wd_ref[...])
    tail = ext_scr[tm:tm + hp, :]
    carry_scr[j] = tail
    rows_ref[0, 0] = tail

    @pl.when(j == pl.num_programs(2) - 1)
    def _():
        o_ref[0] = x_ref[0] + gt_ref[0] * acc_scr[...]


def _ffn(x, sh, sc, gt, g, hist, wa, wb, cw, cb, wd, tm, tf, stride):
    nb, t, d = x.shape
    dff = wa.shape[1]
    hp = hist.shape[0]
    nj = dff // tf
    tok = pl.BlockSpec((1, tm, d), lambda n, i, j: (n, i, 0))

    def mod_spec(m):
        if m.shape[1] == 1:
            return pl.BlockSpec((1, 1, d), lambda n, i, j: (n, 0, 0))
        return pl.BlockSpec((1, tm, d), lambda n, i, j: (n, i, 0))

    return pl.pallas_call(
        functools.partial(_ffn_kernel, tm=tm, stride=stride),
        grid=(nb, t // tm, nj),
        in_specs=[tok, mod_spec(sh), mod_spec(sc), mod_spec(gt),
                  pl.BlockSpec((1, d), lambda n, i, j: (0, 0)),
                  pl.BlockSpec((hp, tf), lambda n, i, j: (0, j)),
                  pl.BlockSpec((d, tf), lambda n, i, j: (0, j)),
                  pl.BlockSpec((d, tf), lambda n, i, j: (0, j)),
                  pl.BlockSpec((FFN_CONV_W, tf), lambda n, i, j: (0, j)),
                  pl.BlockSpec((1, tf), lambda n, i, j: (0, j)),
                  pl.BlockSpec((tf, d), lambda n, i, j: (j, 0))],
        out_specs=[tok, pl.BlockSpec((1, 1, hp, tf), lambda n, i, j: (n, i, 0, j))],
        out_shape=[jax.ShapeDtypeStruct((nb, t, d), f32), jax.ShapeDtypeStruct((nb, t // tm, hp, dff), f32)],
        scratch_shapes=[pltpu.VMEM((tm, d), bf16), pltpu.VMEM((tm, d), f32), pltpu.VMEM((hp + tm, tf), f32),
                        pltpu.VMEM((nj, hp, tf), f32)],
        compiler_params=_cparams(3),
    )(x, sh, sc, gt, g, hist, wa, wb, cw, cb, wd)


def _final_kernel(x_ref, g_ref, o_ref):
    x = x_ref[0]
    o_ref[0] = x * lax.rsqrt(jnp.mean(x * x, axis=-1, keepdims=True) + EPS) * g_ref[...]


def _final_norm(x, g, tm):
    nb, t, d = x.shape
    tok = pl.BlockSpec((1, tm, d), lambda n, i: (n, i, 0))
    return pl.pallas_call(
        _final_kernel, grid=(nb, t // tm), in_specs=[tok, pl.BlockSpec((1, d), lambda n, i: (0, 0))],
        out_specs=tok, out_shape=jax.ShapeDtypeStruct((nb, t, d), f32), compiler_params=_cparams(2),
    )(x, g)


def _layer_weights(w_in, w_cmp, pe_cmp):
    d = w_in.shape[0]
    offs = np.cumsum([0, H_A * HEAD_DIM, G_A * HEAD_DIM, G_A * HEAD_DIM, H_IDX * D_IDX, H_IDX, D_IDX, 2 * D_CONV,
                      H_C * HEAD_DIM, 6 * G_C * HEAD_DIM, 3 * H_C, 3 * d])
    qa, ka, va, iq, iw, ik, glu, qc, kvc, gc, gm = [w_in[:, offs[k]:offs[k + 1]] for k in range(11)]
    ikp = jnp.concatenate([ik, iw, gc, jnp.zeros((d, CH - D_IDX - H_IDX - 3 * H_C), f32)], axis=1)
    ws = [qa, jnp.concatenate([ka, va], axis=1), iq, ikp, glu, qc, kvc[:, 0:512], kvc[:, 512:768], gm]
    ws = [w.astype(bf16) for w in ws]
    misc = jnp.concatenate([iw, jnp.zeros((d, 8 - H_IDX), f32), gc, jnp.zeros((d, CH - 8 - 3 * H_C), f32)], axis=1)
    wft = jnp.concatenate([va, kvc[:, 384:512], kvc[:, 640:768], misc], axis=1).T.astype(bf16)
    wbd = jnp.zeros((2, COMP_BLOCK, CH, CH), f32)
    for s in range(2):
        blk = w_cmp[s].reshape(COMP_BLOCK, HEAD_DIM, HEAD_DIM)
        for g in range(G_C):
            wbd = wbd.at[s, :, g * HEAD_DIM:(g + 1) * HEAD_DIM, g * HEAD_DIM:(g + 1) * HEAD_DIM].set(blk)
    pe256 = jnp.concatenate([pe_cmp] * G_C, axis=2)
    return ws, wft, wbd.astype(bf16), pe256


def kernel(x_prompt, x_sample, c_prompt, c_sample, cache_dsa_kv, cache_dsa_idx, cache_nsa_kv, state_nsa_win,
           state_conv, state_ffn, page_table, rel_bias, w_mod, b_mod, g_mix, g_ffn, w_in, w_cmp, pe_cmp, conv_w,
           conv_b, ln_g, ln_b, w_oa, w_ob, w_oc, w_o, w_up, ffn_conv_w, ffn_conv_b, w_down, g_final):
    n_p, t_p, d = x_prompt.shape
    n_s, t_s, _ = x_sample.shape
    depth = w_mod.shape[0]
    dff = w_down.shape[1]
    n_pool = cache_dsa_kv.shape[0]
    n_pages = page_table.shape[1]
    past = n_pages * PAGE_SIZE
    assert t_p % CH == 0 and t_s < COMP_STRIDE and (n_s * t_s) % CH == 0 and n_s % 8 == 0
    tm_p = 256
    rows_s = n_s * t_s

    mod = _modulation(jnp.concatenate([c_prompt, c_sample], axis=0), w_mod, b_mod)
    toep = _toeplitz_bias(rel_bias)
    nqb = past // CH + 1
    cmpb = _cmp_bias(rel_bias, max(nqb, t_p // CH))
    rowbias_a = _row_bias(rel_bias, 0, H_A, t_s, past, nqb)
    rowbias_c = _row_bias(rel_bias, H_A, H_C, t_s, past, nqb)
    cmprow = _row_bias(rel_bias, H_A, H_C, t_s, past, 1, COMP_STRIDE, COMP_BLOCK - 1)

    idx_pool = jnp.transpose(cache_dsa_idx, (0, 1, 3, 2))
    kv_pool = jnp.transpose(cache_dsa_kv, (0, 1, 3, 4, 5, 2)).reshape(n_pool, depth, 512, PAGE_SIZE)
    nsa_pool = jnp.transpose(cache_nsa_kv, (0, 1, 3, 4, 5, 2)).reshape(n_pool, depth, 512, PAGE_SIZE)
    win_t = jnp.transpose(state_nsa_win, (0, 1, 3, 4, 5, 2)).reshape(n_s, depth, 256, -1)

    xp = x_prompt
    xs = jnp.transpose(x_sample, (1, 0, 2)).reshape(1, rows_s, d)
    zeros_conv = jnp.zeros((n_p, 32, D_CONV), f32)
    zeros_ffn = jnp.zeros((8, dff), f32)
    tf = dff // 2

    def to_seq(a):
        return jnp.transpose(a.reshape(t_s, n_s, a.shape[-1]), (1, 0, 2))

    def to_rows(a):
        return jnp.transpose(a, (1, 0, 2)).reshape(1, t_s * n_s, a.shape[-1])

    outs_p = [[] for _ in range(6)]
    outs_s = [[] for _ in range(6)]
    for l in range(depth):
        ws, wft, wbd, pe256 = _layer_weights(w_in[l], w_cmp[l], pe_cmp[l])
        m = [mod[l][:, k * d:(k + 1) * d] for k in range(6)]
        mp = [a[:n_p].reshape(n_p, 1, d) for a in m]
        ms = [jnp.tile(a[n_p:], (t_s, 1)).reshape(1, rows_s, d) for a in m]
        g1 = g_mix[l].reshape(1, d)
        g2 = g_ffn[l].reshape(1, d)
        cw, cb = conv_w[l], conv_b[l].reshape(1, D_CONV)
        lg, lb = ln_g[l].reshape(1, D_CONV), ln_b[l].reshape(1, D_CONV)
        woa, wob, woc, wo = [w[l].astype(bf16) for w in (w_oa, w_ob, w_oc, w_o)]
        wa, wb = w_up[l][:, :dff].astype(bf16), w_up[l][:, dff:].astype(bf16)
        wd = w_down[l].astype(bf16)
        fcw, fcb = ffn_conv_w[l], ffn_conv_b[l].reshape(1, dff)

        qa, kva, iq, ikp, glu, qc, kvc4, kvw, gm, ft = _inproj(xp, mp[0], mp[1], g1, ws, wft, tm_p)
        o_a = _dsa_prompt(qa, iq, ikp, kva, ft, toep)
        kc, kct = _compress_prompt(kvc4, wbd, pe256)
        o_c = _nsa_prompt(qc, ft, kc, kct, cmpb, kvc4, kvw, toep)
        o_b, conv_rows = _conv_module(glu, zeros_conv, cw, cb, lg, lb, wob, tm_p, 1)
        xp = _merge(xp, mp[2], o_a, o_b, o_c, gm, woa, woc, wo, tm_p)
        xp, ffn_rows = _ffn(xp, mp[3], mp[4], mp[5], g2, zeros_ffn, wa, wb, fcw, fcb, wd, 512, tf, 1)
        outs_p[0].append(kva.reshape(n_p, t_p, 2, G_A, HEAD_DIM))
        outs_p[1].append(ikp[:, :, :D_IDX])
        outs_p[2].append(kvc4.reshape(n_p, t_p, 4, G_C, HEAD_DIM))
        outs_p[3].append(kvw[:, t_p - min(WINDOW, t_p):].reshape(n_p, min(WINDOW, t_p), 2, G_C, HEAD_DIM))
        outs_p[4].append(conv_rows)
        outs_p[5].append(ffn_rows[:, -1, 8 - (FFN_CONV_W - 1):])

        qa, kva, iq, ikp, glu, qc, kvc4, kvw, gm, ft = _inproj(xs, ms[0], ms[1], g1, ws, wft, rows_s)
        ft_s = jnp.transpose(jnp.transpose(ft[0], (1, 0, 2)).reshape(FT_ROWS, t_s, n_s), (2, 0, 1))
        qa_s, kva_s, iq_s, ikp_s, qc_s, kvc4_s, kvw_s = [to_seq(a) for a in (qa, kva, iq, ikp, qc, kvc4, kvw)]
        o_a = _dsa_sample(page_table, l, qa_s, iq_s, kva_s, ikp_s, idx_pool, kv_pool, rowbias_a)
        kc, kct = _compress_sample(page_table, l, nsa_pool, wbd, pe256)
        o_c = _nsa_sample(page_table, l, qc_s, ikp_s, kvc4_s, kvw_s, win_t, kc, nsa_pool, rowbias_c, cmprow)
        hist_c = jnp.transpose(state_conv[:, l], (1, 0, 2)).reshape(1, (CONV_W - 1) * n_s, D_CONV)
        o_b, conv_rows = _conv_module(glu, hist_c, cw, cb, lg, lb, wob, rows_s, n_s)
        xs = _merge(xs, ms[2], to_rows(o_a), o_b, to_rows(o_c), gm, woa, woc, wo, rows_s)
        hist_f = jnp.transpose(state_ffn[:, l], (1, 0, 2)).reshape((FFN_CONV_W - 1) * n_s, dff)
        xs, ffn_rows = _ffn(xs, ms[3], ms[4], ms[5], g2, hist_f, wa, wb, fcw, fcb, wd, rows_s, tf, n_s)
        outs_s[0].append(kva_s.reshape(n_s, t_s, 2, G_A, HEAD_DIM))
        outs_s[1].append(ikp_s[:, :, :D_IDX])
        outs_s[2].append(kvc4_s.reshape(n_s, t_s, 4, G_C, HEAD_DIM))
        win_all = jnp.concatenate([win_t[:, l], jnp.transpose(kvw_s, (0, 2, 1))], axis=2)
        keep_w = min(WINDOW, past + t_s)
        win_all = win_all[:, :, win_all.shape[2] - keep_w:].reshape(n_s, 2, G_C, HEAD_DIM, keep_w)
        outs_s[3].append(jnp.transpose(win_all, (0, 4, 1, 2, 3)))
        outs_s[4].append(jnp.transpose(conv_rows[0].reshape(CONV_W - 1, n_s, D_CONV), (1, 0, 2)))
        outs_s[5].append(jnp.transpose(ffn_rows[0, 0].reshape(FFN_CONV_W - 1, n_s, dff), (1, 0, 2)))

    y_p = _final_norm(xp, g_final.reshape(1, d), 512)
    y_s = to_seq(_final_norm(xs, g_final.reshape(1, d), rows_s))
    stk = lambda group: [jnp.stack(rows, axis=1) for rows in group]
    return tuple([y_p, y_s] + stk(outs_p) + stk(outs_s))
```

```python
import functools
import math

import numpy as np
import jax
import jax.numpy as jnp
from jax import lax
from jax.experimental import pallas as pl
from jax.experimental.pallas import tpu as pltpu

HEAD_DIM = 64
H_A, G_A = 8, 4
R_A = H_A // G_A
H_IDX, D_IDX = 4, 64
DSA_TOPK = 256
D_CONV, CONV_W = 512, 31
H_C, G_C = 8, 2
R_C = H_C // G_C
COMP_BLOCK, COMP_STRIDE = 32, 16
SEL_BLOCK, SEL_TOPK = 64, 8
WINDOW = 512
FFN_CONV_W = 3
NUM_BUCKETS, MAX_DISTANCE = 32, 128
FORCE_SCORE = 1.0e4
EPS = 1e-6
PAGE_SIZE = 128

CH = 128
FT_ROWS = 640
VMEM_LIMIT = 56 * 1024 * 1024

f32, bf16, i32 = jnp.float32, jnp.bfloat16, jnp.int32
NEG_INF = float("-inf")


def _cparams(n_axes):
    return pltpu.CompilerParams(dimension_semantics=("arbitrary",) * n_axes, vmem_limit_bytes=VMEM_LIMIT)


def _dot(a, b):
    return jnp.dot(a, b, preferred_element_type=f32)


def _dot_nt(a, b):
    return lax.dot_general(a, b, (((1,), (1,)), ((), ())), preferred_element_type=f32)


def _colsum(x):
    return x.reshape(CH // 8, 8, CH).sum(axis=0)


def _colmax(x):
    return x.reshape(CH // 8, 8, CH).max(axis=0)


def _mod_kernel(c_ref, w_ref, b_ref, o_ref):
    c = c_ref[...]
    a = (c * jax.nn.sigmoid(c)).astype(bf16)
    o_ref[0] = _dot(a, w_ref[0].astype(bf16)) + b_ref[0]


def _modulation(c_all, w_mod, b_mod):
    depth, d, d6 = w_mod.shape
    n = c_all.shape[0]
    tn = 1024
    return pl.pallas_call(
        _mod_kernel,
        grid=(depth, d6 // tn),
        in_specs=[pl.BlockSpec((n, d), lambda l, j: (0, 0)),
                  pl.BlockSpec((1, d, tn), lambda l, j: (l, 0, j)),
                  pl.BlockSpec((1, 1, tn), lambda l, j: (l, 0, j))],
        out_specs=pl.BlockSpec((1, n, tn), lambda l, j: (l, 0, j)),
        out_shape=jax.ShapeDtypeStruct((depth, n, d6), f32),
        compiler_params=_cparams(2),
    )(c_all, w_mod, b_mod.reshape(depth, 1, d6))


def _t5_bucket(n):
    max_exact = NUM_BUCKETS // 2
    nf = jnp.maximum(n, 1).astype(f32)
    large = max_exact + (jnp.log(nf / max_exact) / math.log(MAX_DISTANCE / max_exact)
                         * (NUM_BUCKETS - max_exact)).astype(i32)
    return jnp.where(n < max_exact, n, jnp.minimum(large, NUM_BUCKETS - 1))


def _bias_from_dist(dist, tab_ref, h):
    b = _t5_bucket(jnp.maximum(dist, 0))
    out = jnp.zeros(dist.shape, f32)
    for k in range(NUM_BUCKETS):
        out = jnp.where(b == k, tab_ref[k, h], out)
    return out


def _toep_kernel(tab_ref, o_ref):
    h = pl.program_id(0)
    row = lax.broadcasted_iota(i32, (CH, CH), 0)
    lane = lax.broadcasted_iota(i32, (CH, CH), 1)
    o_ref[0, 0] = _bias_from_dist(jnp.full((CH, CH), 2 * CH, i32), tab_ref, h)
    o_ref[0, 1] = _bias_from_dist(CH + lane - row, tab_ref, h)
    o_ref[0, 2] = _bias_from_dist(lane - row, tab_ref, h)


def _toeplitz_bias(rel_bias):
    nh = rel_bias.shape[1]
    return pl.pallas_call(
        _toep_kernel,
        grid=(nh,),
        in_specs=[pl.BlockSpec(memory_space=pltpu.SMEM)],
        out_specs=pl.BlockSpec((1, 3, CH, CH), lambda h: (h, 0, 0, 0)),
        out_shape=jax.ShapeDtypeStruct((nh, 3, CH, CH), f32),
        compiler_params=_cparams(1),
    )(rel_bias)


def _cmpbias_kernel(tab_ref, o_ref):
    h = pl.program_id(0)
    qb = pl.program_id(1)
    row = lax.broadcasted_iota(i32, (CH, CH), 0)
    lane = lax.broadcasted_iota(i32, (CH, CH), 1)
    dist = qb * CH + lane - (row * COMP_STRIDE + COMP_BLOCK - 1)
    o_ref[0, 0] = _bias_from_dist(dist, tab_ref, H_A + h)


def _cmp_bias(rel_bias, nqb):
    return pl.pallas_call(
        _cmpbias_kernel,
        grid=(H_C, nqb),
        in_specs=[pl.BlockSpec(memory_space=pltpu.SMEM)],
        out_specs=pl.BlockSpec((1, 1, CH, CH), lambda h, q: (h, q, 0, 0)),
        out_shape=jax.ShapeDtypeStruct((H_C, nqb, CH, CH), f32),
        compiler_params=_cparams(2),
    )(rel_bias)


def _inproj_kernel(x_ref, sh_ref, sc_ref, g_ref, *refs, n_w):
    w_refs = refs[:n_w]
    wft_ref = refs[n_w]
    out_refs = refs[n_w + 1:2 * n_w + 1]
    ft_ref = refs[2 * n_w + 1]
    x = x_ref[0]
    y = x * lax.rsqrt(jnp.mean(x * x, axis=-1, keepdims=True) + EPS) * g_ref[...]
    hb = (y * (1.0 + sc_ref[0]) + sh_ref[0]).astype(bf16)
    for w_ref, o_ref in zip(w_refs, out_refs):
        o_ref[0] = _dot(hb, w_ref[...])
    ft = _dot_nt(wft_ref[...], hb)
    for j in range(ft.shape[1] // CH):
        ft_ref[0, j] = ft[:, j * CH:(j + 1) * CH]


def _mod_spec(m, tm):
    if m.shape[1] == 1:
        return pl.BlockSpec((1, 1, m.shape[2]), lambda n, i: (n, 0, 0))
    return pl.BlockSpec((1, tm, m.shape[2]), lambda n, i: (n, i, 0))


def _const_spec(a):
    nd = a.ndim
    return pl.BlockSpec(a.shape, lambda *_: (0,) * nd, pipeline_mode=pl.Buffered(1))


def _inproj(x, sh, sc, g, ws, wft, tm):
    nb, t, d = x.shape
    n_w = len(ws)
    in_specs = [pl.BlockSpec((1, tm, d), lambda n, i: (n, i, 0)), _mod_spec(sh, tm), _mod_spec(sc, tm),
                _const_spec(g)] + [_const_spec(w) for w in ws] + [_const_spec(wft)]
    out_specs = [pl.BlockSpec((1, tm, w.shape[1]), lambda n, i: (n, i, 0)) for w in ws]
    out_specs.append(pl.BlockSpec((1, tm // CH, FT_ROWS, CH), lambda n, i: (n, i, 0, 0)))
    out_shape = [jax.ShapeDtypeStruct((nb, t, w.shape[1]), f32) for w in ws]
    out_shape.append(jax.ShapeDtypeStruct((nb, t // CH, FT_ROWS, CH), f32))
    return pl.pallas_call(
        functools.partial(_inproj_kernel, n_w=n_w),
        grid=(nb, t // tm), in_specs=in_specs, out_specs=out_specs, out_shape=out_shape,
        compiler_params=_cparams(2),
    )(x, sh, sc, g, *ws, wft)


def _sortable(x):
    x = jnp.where(x == 0.0, 0.0, x)
    b = lax.bitcast_convert_type(x, i32)
    return jnp.where(b < 0, b ^ 0x7FFFFFFF, b)


def _head_operand(tile, src_half, dst_half, mask_other):
    if src_half != dst_half:
        tile = pltpu.roll(tile, 64, 1)
    if mask_other:
        lane = lax.broadcasted_iota(i32, tile.shape, 1)
        keep = (lane < 64) if dst_half == 0 else (lane >= 64)
        tile = jnp.where(keep, tile, 0.0)
    return tile.astype(bf16)


def _shr_pow2(x, g):
    return x // g if isinstance(x, int) else jnp.right_shift(x, g.bit_length() - 1)


def _chunk_loop(lo, hi, body, carry, group):
    if isinstance(lo, int) and isinstance(hi, int):
        for c in range(lo, hi):
            carry = body(c, carry)
        return carry

    def gbody(gi, carry):
        for k in range(group):
            carry = body(gi * group + k, carry)
        return carry

    return lax.fori_loop(_shr_pow2(lo, group), _shr_pow2(hi + group - 1, group), gbody, carry)


def _super_loop(lo, hi, body, carry, group):
    if isinstance(lo, int) and isinstance(hi, int):
        c = lo
        while c < hi:
            n = min(group, hi - c)
            carry = body(c, n, carry)
            c += n
        return carry
    return lax.fori_loop(_shr_pow2(lo, group), _shr_pow2(hi + group - 1, group),
                         lambda gi, carry: body(gi * group, group, carry), carry)


ATT_RUN = 4


def _attend(lo, hi, q_stack, get_k, extra_fn, get_vt, s_scr, acc_scr, run=None):
    n_groups = len(q_stack)
    rep = q_stack[0].shape[0] // CH
    nh = n_groups * rep

    def p1(c0, n, macc):
        adds = [extra_fn(c0 + k) for k in range(n)]
        macc = list(macc)
        for g in range(n_groups):
            s_all = _dot_nt(get_k(c0, g, n).astype(bf16), q_stack[g])
            for k in range(n):
                for r in range(rep):
                    h = g * rep + r
                    s = s_all[k * CH:(k + 1) * CH, r * CH:(r + 1) * CH] + adds[k](h, g)
                    s_scr[h, c0 + k] = s
                    macc[h] = jnp.maximum(macc[h], _colmax(s))
        return tuple(macc)

    def walk(body, carry):
        if run is not None:
            return body(lo, run, carry)
        return _super_loop(lo, hi, body, carry, ATT_RUN)

    macc = walk(p1, tuple(jnp.full((8, CH), NEG_INF, f32) for _ in range(nh)))
    m = []
    for h in range(nh):
        mh = jnp.max(macc[h], axis=0, keepdims=True)
        m.append(jnp.where(mh > NEG_INF, mh, 0.0))
    for g in range(n_groups):
        acc_scr[g] = jnp.zeros(acc_scr.shape[1:], f32)

    def p2(c0, n, lacc):
        lacc = list(lacc)
        for g in range(n_groups):
            rows = []
            for k in range(n):
                ps = []
                for r in range(rep):
                    h = g * rep + r
                    p = jnp.exp(s_scr[h, c0 + k] - m[h])
                    lacc[h] = lacc[h] + _colsum(p)
                    ps.append(p.astype(bf16))
                rows.append(jnp.concatenate(ps, axis=1))
            p_run = rows[0] if n == 1 else jnp.concatenate(rows, axis=0)
            acc_scr[g] += _dot(get_vt(c0, g, n).astype(bf16), p_run)
        return tuple(lacc)

    lacc = walk(p2, tuple(jnp.zeros((8, CH), f32) for _ in range(nh)))
    outs = []
    for h in range(nh):
        g, r = divmod(h, rep)
        inv = 1.0 / jnp.maximum(jnp.sum(lacc[h], axis=0, keepdims=True), 1e-30)
        outs.append(acc_scr[g, :, r * CH:(r + 1) * CH] * inv)
    return outs


def _rows(c0, n):
    start = c0 * CH if isinstance(c0, int) else pl.multiple_of(c0 * CH, CH)
    return pl.ds(start, n * CH)


def _vt_run(ref, c0, n, lo, hi):
    tiles = [ref[c0 + k, lo:hi, :] for k in range(n)]
    return tiles[0] if n == 1 else jnp.concatenate(tiles, axis=1)


def _dsa_core(i_abs, nci, topk, pos_bits, n_valid, q_tile, iq_tile, iw_t, get_ik, get_k, get_vt, toep_ref,
              keys_scr, mask_scr, s_scr, acc_scr, ot_scr):
    row = lax.broadcasted_iota(i32, (CH, CH), 0)
    lane = lax.broadcasted_iota(i32, (CH, CH), 1)
    t_pos = i_abs * CH + lane

    iq_stack = jnp.concatenate([_head_operand(iq_tile(h // 2), h % 2, 0, True) for h in range(H_IDX)], axis=0)

    def idx_body(c, carry):
        s_all = _dot_nt(get_ik(c).astype(bf16), iq_stack)
        acc = jnp.zeros((CH, CH), f32)
        for h in range(H_IDX):
            acc = acc + iw_t[h:h + 1, :] * jnp.maximum(s_all[:, h * CH:(h + 1) * CH], 0.0)
        acc = jnp.where(c * CH + row <= t_pos, acc, NEG_INF)
        keys_scr[c] = _sortable(acc)
        return carry

    _chunk_loop(0, nci, idx_body, 0, 4)

    def count(pred_fn):
        def body(c, acc):
            return acc + _colsum(jnp.where(pred_fn(c, keys_scr[c]), 1.0, 0.0))
        acc = _chunk_loop(0, nci, body, jnp.zeros((8, CH), f32), 4)
        return jnp.sum(acc, axis=0, keepdims=True)

    kf = float(topk)
    int_min = jnp.full((1, CH), -2 ** 31, i32)
    c0 = count(lambda c, k: k >= 0)
    cand = jnp.where(c0 >= kf, jnp.zeros((1, CH), i32), int_min)

    def bit_body(j, cand):
        trial = cand | jnp.left_shift(jnp.int32(1), 30 - j)
        cnt = count(lambda c, k: k >= trial)
        return jnp.where(cnt >= kf, trial, cand)

    cand = lax.fori_loop(0, 31, bit_body, cand)
    cnt_ge = count(lambda c, k: k >= cand)
    tied = (cnt_ge > kf) & (lane[0:1, :] < n_valid)
    has_tie = jnp.max(jnp.where(tied, 1.0, 0.0)) > 0.5

    @pl.when(jnp.logical_not(has_tie))
    def _():
        def mask_body(c, carry):
            sel = (keys_scr[c] >= cand) & (c * CH + row <= t_pos)
            mask_scr[c] = jnp.where(sel, 0.0, NEG_INF)
            return carry
        _chunk_loop(0, nci, mask_body, 0, 4)

    @pl.when(has_tie)
    def _():
        need = kf - count(lambda c, k: k > cand)

        def pos_body(j, pcut):
            trial = pcut | jnp.left_shift(jnp.int32(1), pos_bits - 1 - j)
            cnt = count(lambda c, k: (k == cand) & (c * CH + row < trial))
            return jnp.where(cnt < need, trial, pcut)

        pcut = lax.fori_loop(0, pos_bits, pos_body, jnp.zeros((1, CH), i32))

        def mask_body(c, carry):
            k = keys_scr[c]
            s_pos = c * CH + row
            sel = ((k > cand) | ((k == cand) & (s_pos <= pcut))) & (s_pos <= t_pos)
            mask_scr[c] = jnp.where(sel, 0.0, NEG_INF)
            return carry
        _chunk_loop(0, nci, mask_body, 0, 4)

    scale = HEAD_DIM ** -0.5
    q_stack = []
    for g in range(G_A):
        heads = [_head_operand(q_tile((g * R_A + r) // 2) * scale, (g * R_A + r) % 2, g % 2, True) for r in range(R_A)]
        q_stack.append(jnp.concatenate(heads, axis=0))

    def extra_fn(c):
        bidx = jnp.clip(c - i_abs + 2, 0, 2)
        mk = mask_scr[c]
        return lambda h, g: toep_ref[h, bidx] + mk

    outs = _attend(0, nci, q_stack, lambda c0, g, n: get_k(c0, g // 2, n), extra_fn, get_vt, s_scr, acc_scr)
    for h in range(H_A):
        ot_scr[h * HEAD_DIM:(h + 1) * HEAD_DIM, :] = outs[h]
    return ot_scr[...].T


def _dsa_prompt_kernel(q_ref, iq_ref, misc_ref, ik_ref, k_ref, vt_ref, toep_ref, o_ref,
                       keys_scr, mask_scr, s_scr, acc_scr, ot_scr, *, topk, pos_bits):
    i = pl.program_id(1)

    def rows(c):
        return pl.ds(pl.multiple_of(c * CH, CH), CH)

    o_ref[0] = _dsa_core(
        i, i + 1, topk, pos_bits, CH,
        lambda j: q_ref[0, :, j * CH:(j + 1) * CH],
        lambda j: iq_ref[0, :, j * CH:(j + 1) * CH],
        misc_ref[0, 0, 0:8, :],
        lambda c: ik_ref[0, rows(c), :],
        lambda c0, j, n: k_ref[0, _rows(c0, n), j * CH:(j + 1) * CH],
        lambda c0, g, n: _vt_run(vt_ref.at[0], c0, n, g * HEAD_DIM, (g + 1) * HEAD_DIM),
        toep_ref, keys_scr, mask_scr, s_scr, acc_scr, ot_scr)


def _dsa_prompt(qa, iq, ikp, kva, ft, toep):
    n, t, _ = qa.shape
    nc = t // CH
    topk = min(DSA_TOPK, t // 4)
    assert nc % 4 == 0, "traced chunk loops walk aligned groups of up to 4 chunks"
    kern = functools.partial(_dsa_prompt_kernel, topk=topk, pos_bits=max(1, (t - 1).bit_length()))
    return pl.pallas_call(
        kern,
        grid=(n, nc),
        in_specs=[pl.BlockSpec((1, CH, 512), lambda b, i: (b, i, 0)),
                  pl.BlockSpec((1, CH, 256), lambda b, i: (b, i, 0)),
                  pl.BlockSpec((1, 1, CH, CH), lambda b, i: (b, i, 4, 0)),
                  pl.BlockSpec((1, t, CH), lambda b, i: (b, 0, 0)),
                  pl.BlockSpec((1, t, 256), lambda b, i: (b, 0, 0)),
                  pl.BlockSpec((1, nc, 256, CH), lambda b, i: (b, 0, 0, 0)),
                  pl.BlockSpec((H_A, 3, CH, CH), lambda b, i: (0, 0, 0, 0))],
        out_specs=pl.BlockSpec((1, CH, 512), lambda b, i: (b, i, 0)),
        out_shape=jax.ShapeDtypeStruct((n, t, 512), f32),
        scratch_shapes=[pltpu.VMEM((nc, CH, CH), i32), pltpu.VMEM((nc, CH, CH), f32),
                        pltpu.VMEM((H_A, nc, CH, CH), f32), pltpu.VMEM((G_A, HEAD_DIM, R_A * CH), f32),
                        pltpu.VMEM((512, CH), f32)],
        compiler_params=_cparams(2),
    )(qa, iq, ft, ikp, kva, ft, toep)


def _pad_rows(ref, x):
    ref[...] = jnp.zeros(ref.shape, f32)
    ref[0:x.shape[0], :] = x


def _rowbias_kernel(tab_ref, o_ref, *, head0, q0, n_heads, t_new, key_mul, key_add):
    c = pl.program_id(0)
    nrow = n_heads * t_new
    row = lax.broadcasted_iota(i32, (nrow, CH), 0)
    lane = lax.broadcasted_iota(i32, (nrow, CH), 1)
    hrow = row // t_new
    b = _t5_bucket(jnp.maximum(q0 + (row - hrow * t_new) - ((c * CH + lane) * key_mul + key_add), 0))
    out = jnp.zeros((nrow, CH), f32)
    for k in range(NUM_BUCKETS):
        for h in range(n_heads):
            out = jnp.where((b == k) & (hrow == h), tab_ref[k, head0 + h], out)
    o_ref[0] = out


def _row_bias(rel_bias, head0, n_heads, t_new, q0, nc, key_mul=1, key_add=0):
    nrow = n_heads * t_new
    return pl.pallas_call(
        functools.partial(_rowbias_kernel, head0=head0, q0=q0, n_heads=n_heads, t_new=t_new, key_mul=key_mul,
                          key_add=key_add),
        grid=(nc,),
        in_specs=[pl.BlockSpec(memory_space=pltpu.SMEM)],
        out_specs=pl.BlockSpec((1, nrow, CH), lambda c: (c, 0, 0)),
        out_shape=jax.ShapeDtypeStruct((nc, nrow, CH), f32),
        compiler_params=_cparams(1),
    )(rel_bias)


def _rowsum(x):
    return jnp.sum(x, axis=1, keepdims=True)


def _dsa_sample_kernel(pt_ref, q_ref, iq_ref, kvn_ref, ikn_ref, *refs, n_pages, topk, pos_bits):
    idx_pages = refs[:n_pages]
    kv_pages = refs[n_pages:2 * n_pages]
    (bias_ref, o_ref, qrow_scr, iqrow_scr, wrow_scr, iknew_scr, knew_scr, vnew_scr,
     keys_scr, mask_scr, s_scr) = refs[2 * n_pages:]
    t_new = q_ref.shape[1]
    nrow = H_A * t_new
    nc = n_pages + 1
    q0 = n_pages * PAGE_SIZE
    lane8 = lax.broadcasted_iota(i32, (8, CH), 1)
    row8 = lax.broadcasted_iota(i32, (8, CH), 0)

    iqrow_scr[...] = jnp.zeros(iqrow_scr.shape, f32)
    wrow_scr[...] = jnp.zeros(wrow_scr.shape, f32)
    ikn = ikn_ref[0]
    for hi in range(H_IDX):
        iqrow_scr[hi * 8:hi * 8 + t_new, 0:D_IDX] = iq_ref[0][:, hi * D_IDX:(hi + 1) * D_IDX]
        wrow_scr[hi * 8:hi * 8 + t_new, :] = jnp.broadcast_to(ikn[:, D_IDX + hi:D_IDX + hi + 1], (t_new, CH))
    qrow_scr[...] = jnp.zeros(qrow_scr.shape, f32)
    scale = HEAD_DIM ** -0.5
    for h in range(H_A):
        g = h // R_A
        qrow_scr[h * t_new:(h + 1) * t_new, g * HEAD_DIM:(g + 1) * HEAD_DIM] = (
            q_ref[0][:, h * HEAD_DIM:(h + 1) * HEAD_DIM] * scale)
    _pad_rows(iknew_scr, ikn)
    _pad_rows(knew_scr, kvn_ref[0][:, 0:256])
    _pad_rows(vnew_scr, kvn_ref[0][:, 256:512])
    iqrows = iqrow_scr[...].astype(bf16)
    qrows = qrow_scr[...].astype(bf16)
    wrow = wrow_scr[...]

    def idx_keys(s, valid):
        s = jnp.maximum(s, 0.0) * wrow
        acc = s[0:8] + s[8:16] + s[16:24] + s[24:32]
        if valid is not None:
            acc = jnp.where(valid, acc, NEG_INF)
        return _sortable(acc)

    zpad = jnp.zeros((CH - D_IDX, CH), f32)
    for p in range(n_pages):
        ikt = jnp.concatenate([idx_pages[p][0, 0], zpad], axis=0).astype(bf16)
        keys_scr[p] = idx_keys(_dot(iqrows, ikt), None)
    keys_scr[n_pages] = idx_keys(_dot_nt(iqrows, iknew_scr[...].astype(bf16)), lane8 <= row8)

    def count(pred_fn):
        acc = jnp.zeros((8, CH), f32)
        for c in range(nc):
            acc = acc + jnp.where(pred_fn(c, keys_scr[c]), 1.0, 0.0)
        return _rowsum(acc)

    def reach(base, trials):
        accs = [jnp.zeros((8, CH), f32) for _ in trials]
        for c in range(nc):
            k = keys_scr[c]
            for i, t in enumerate(trials):
                accs[i] = accs[i] + jnp.where(k >= t, 1.0, 0.0)
        for a, t in zip(accs, trials):
            base = jnp.where(_rowsum(a) >= kf, t, base)
        return base

    kf = float(topk)
    cand = reach(jnp.full((8, 1), -2 ** 31, i32), [jnp.zeros((8, 1), i32)])
    cand = reach(cand, [cand | (1 << 30)])

    def bits_body(j, cand):
        sh = 27 - 3 * j
        return reach(cand, [cand | jnp.left_shift(jnp.int32(v), sh) for v in range(1, 8)])

    cand = lax.fori_loop(0, 10, bits_body, cand)
    cnt_ge = count(lambda c, k: k >= cand)
    tied = (cnt_ge > kf) & (row8[:, 0:1] < t_new)
    has_tie = jnp.max(jnp.where(tied, 1.0, 0.0)) > 0.5

    def adm(c):
        return (c * CH + lane8 <= q0 + row8) if c == n_pages else None

    @pl.when(jnp.logical_not(has_tie))
    def _():
        for c in range(nc):
            sel = keys_scr[c] >= cand
            if adm(c) is not None:
                sel = sel & adm(c)
            mask_scr[c] = jnp.where(sel, 0.0, NEG_INF)

    @pl.when(has_tie)
    def _():
        need = kf - count(lambda c, k: k > cand)

        def pos_body(j, pcut):
            trial = pcut | jnp.left_shift(jnp.int32(1), pos_bits - 1 - j)
            cnt = count(lambda c, k: (k == cand) & (c * CH + lane8 < trial))
            return jnp.where(cnt < need, trial, pcut)

        pcut = lax.fori_loop(0, pos_bits, pos_body, jnp.zeros((8, 1), i32))
        for c in range(nc):
            k = keys_scr[c]
            sel = (k > cand) | ((k == cand) & (c * CH + lane8 <= pcut))
            if adm(c) is not None:
                sel = sel & adm(c)
            mask_scr[c] = jnp.where(sel, 0.0, NEG_INF)

    def rows_mask(c):
        m8 = mask_scr[c]
        m8 = jnp.where(row8 < t_new, m8, pltpu.roll(m8, t_new, 0))
        return jnp.concatenate([m8] * (nrow // 8), axis=0)

    macc = jnp.full((nrow, CH), NEG_INF, f32)
    for c in range(nc):
        if c < n_pages:
            s = _dot(qrows, kv_pages[c][0, 0][0:256, :].astype(bf16))
        else:
            s = _dot_nt(qrows, knew_scr[...].astype(bf16))
        s = s + bias_ref[c] + rows_mask(c)
        s_scr[c] = s
        macc = jnp.maximum(macc, s)
    m = jnp.max(macc, axis=1, keepdims=True)
    m = jnp.where(m > NEG_INF, m, 0.0)
    lacc = jnp.zeros((nrow, CH), f32)
    oacc = jnp.zeros((nrow, 256), f32)
    for c in range(nc):
        p = jnp.exp(s_scr[c] - m)
        lacc = lacc + p
        if c < n_pages:
            oacc = oacc + _dot_nt(p.astype(bf16), kv_pages[c][0, 0][256:512, :].astype(bf16))
        else:
            oacc = oacc + _dot(p.astype(bf16), vnew_scr[...].astype(bf16))
    o = oacc * (1.0 / jnp.maximum(_rowsum(lacc), 1e-30))
    for h in range(H_A):
        g = h // R_A
        o_ref[0, :, h * HEAD_DIM:(h + 1) * HEAD_DIM] = o[h * t_new:(h + 1) * t_new, g * HEAD_DIM:(g + 1) * HEAD_DIM]


def _page_specs(n_pages, layer, rows, row_block=0):
    return [pl.BlockSpec((1, 1, rows, PAGE_SIZE), lambda b, pt, p=p: (pt[b, p], layer, row_block, 0))
            for p in range(n_pages)]


def _dsa_sample(page_table, layer, qa, iq, kva, ikp, idx_pool, kv_pool, rowbias):
    n, t_new, _ = qa.shape
    n_pages = page_table.shape[1]
    nc = n_pages + 1
    lk = n_pages * PAGE_SIZE + t_new
    topk = min(DSA_TOPK, lk // 4)
    nrow = H_A * t_new
    assert nrow % 8 == 0 and t_new <= 4 and rowbias.shape == (nc, nrow, CH)
    kern = functools.partial(_dsa_sample_kernel, n_pages=n_pages, topk=topk, pos_bits=(nc * CH - 1).bit_length())
    seq = lambda w: pl.BlockSpec((1, t_new, w), lambda b, pt: (b, 0, 0))
    grid_spec = pltpu.PrefetchScalarGridSpec(
        num_scalar_prefetch=1, grid=(n,),
        in_specs=[seq(512), seq(256), seq(512), seq(CH)]
        + _page_specs(n_pages, layer, D_IDX) + _page_specs(n_pages, layer, 512)
        + [pl.BlockSpec((nc, nrow, CH), lambda b, pt: (0, 0, 0))],
        out_specs=pl.BlockSpec((1, t_new, 512), lambda b, pt: (b, 0, 0)),
        scratch_shapes=[pltpu.VMEM((nrow, 256), f32), pltpu.VMEM((H_IDX * 8, CH), f32), pltpu.VMEM((H_IDX * 8, CH), f32),
                        pltpu.VMEM((CH, CH), f32), pltpu.VMEM((CH, 256), f32), pltpu.VMEM((CH, 256), f32),
                        pltpu.VMEM((nc, 8, CH), i32), pltpu.VMEM((nc, 8, CH), f32), pltpu.VMEM((nc, nrow, CH), f32)])
    return pl.pallas_call(
        kern, grid_spec=grid_spec, out_shape=jax.ShapeDtypeStruct((n, t_new, 512), f32),
        compiler_params=_cparams(1),
    )(page_table, qa, iq, kva, ikp, *([idx_pool] * n_pages), *([kv_pool] * n_pages), rowbias)


def _compress_core(xk_ref, xv_ref, wbd_ref, pe_ref, kc_ref, kct_ref):
    nr = xk_ref.shape[0] // COMP_STRIDE
    halves = []
    for s, x_ref in enumerate((xk_ref, xv_ref)):
        a0 = jnp.zeros((nr, CH), f32)
        a1 = jnp.zeros((nr, CH), f32)
        for r in range(COMP_STRIDE):
            xr = x_ref[pl.ds(r, nr, stride=COMP_STRIDE), :]
            r2 = COMP_STRIDE + r
            a0 = a0 + _dot((xr + pe_ref[s, r:r + 1, :]).astype(bf16), wbd_ref[s, r])
            a1 = a1 + _dot((xr + pe_ref[s, r2:r2 + 1, :]).astype(bf16), wbd_ref[s, r2])
        kc = a0 + pltpu.roll(a1, nr - 1, 0)
        if nr < CH:
            kc = jnp.concatenate([kc, jnp.zeros((CH - nr, CH), f32)], axis=0)
        halves.append(kc)
    kc_ref[0, :, 0:CH] = halves[0]
    kc_ref[0, :, CH:2 * CH] = halves[1]
    kct_ref[0] = halves[1].T


def _compress_prompt_kernel(xk_ref, xv_ref, wbd_ref, pe_ref, kc_ref, kct_ref):
    _compress_core(xk_ref.at[0], xv_ref.at[0], wbd_ref, pe_ref, kc_ref, kct_ref)


def _compress_sample_kernel(pt_ref, *refs, n_pages):
    pages = refs[:n_pages]
    wbd_ref, pe_ref, kc_ref, kct_ref, xk_scr, xv_scr = refs[n_pages:]
    for p in range(n_pages):
        page = pages[p][0, 0]
        xk_scr[p * PAGE_SIZE:(p + 1) * PAGE_SIZE, :] = page[0:CH, :].T
        xv_scr[p * PAGE_SIZE:(p + 1) * PAGE_SIZE, :] = page[CH:2 * CH, :].T
    _compress_core(xk_scr, xv_scr, wbd_ref, pe_ref, kc_ref, kct_ref)


def _compress_out(n):
    return ([pl.BlockSpec((1, CH, 256), lambda b, *_: (b, 0, 0)), pl.BlockSpec((1, CH, CH), lambda b, *_: (b, 0, 0))],
            [jax.ShapeDtypeStruct((n, CH, 256), f32), jax.ShapeDtypeStruct((n, CH, CH), f32)])


def _compress_prompt(kvc4, wbd, pe256):
    n, t, _ = kvc4.shape
    assert t % (8 * COMP_STRIDE) == 0 and t <= COMP_STRIDE * CH
    out_specs, out_shape = _compress_out(n)
    return pl.pallas_call(
        _compress_prompt_kernel, grid=(n,),
        in_specs=[pl.BlockSpec((1, t, CH), lambda b: (b, 0, 0)), pl.BlockSpec((1, t, CH), lambda b: (b, 0, 1)),
                  _const_spec(wbd), _const_spec(pe256)],
        out_specs=out_specs, out_shape=out_shape, compiler_params=_cparams(1),
    )(kvc4, kvc4, wbd, pe256)


def _compress_sample(page_table, layer, nsa_pool, wbd, pe256):
    n, n_pages = page_table.shape
    assert n_pages * PAGE_SIZE <= COMP_STRIDE * CH
    out_specs, out_shape = _compress_out(n)
    grid_spec = pltpu.PrefetchScalarGridSpec(
        num_scalar_prefetch=1, grid=(n,),
        in_specs=_page_specs(n_pages, layer, 256) + [_const_spec(wbd), _const_spec(pe256)],
        out_specs=out_specs,
        scratch_shapes=[pltpu.VMEM((n_pages * PAGE_SIZE, CH), f32), pltpu.VMEM((n_pages * PAGE_SIZE, CH), f32)])
    return pl.pallas_call(
        functools.partial(_compress_sample_kernel, n_pages=n_pages), grid_spec=grid_spec, out_shape=out_shape,
        compiler_params=_cparams(1),
    )(page_table, *([nsa_pool] * n_pages), wbd, pe256)


def _split3(x):
    hi = x.astype(bf16)
    r1 = x - hi.astype(f32)
    mid = r1.astype(bf16)
    lo = (r1 - mid.astype(f32)).astype(bf16)
    return hi, mid, lo


def _nsa_core(i_abs, n_sel, win_c0, q_tile, gate_t, kc_ref, kct_ref, cmpb_ref, cover_ref, get_selk, get_selvt,
              get_wink, get_winvt, toep_ref, selm_scr, mask_scr, s_scr, acc_scr, ot_scr):
    row = lax.broadcasted_iota(i32, (CH, CH), 0)
    lane = lax.broadcasted_iota(i32, (CH, CH), 1)
    t_pos = i_abs * CH + lane
    scale = HEAD_DIM ** -0.5
    nci = i_abs + 1
    n_chunks_static = mask_scr.shape[1]
    ns_pad = -(-n_sel // 8) * 8

    q_stack = []
    for g in range(G_C):
        heads = [_head_operand(q_tile((g * R_C + r) // 2) * scale, (g * R_C + r) % 2, g, True) for r in range(R_C)]
        q_stack.append(jnp.concatenate(heads, axis=0))

    kc_b = kc_ref[0, :, 0:CH].astype(bf16)
    cmp_valid = t_pos >= row * COMP_STRIDE + COMP_BLOCK - 1
    o_cmp = []
    p_sum = []
    for g in range(G_C):
        s_all = _dot_nt(kc_b, q_stack[g])
        ps = []
        for r in range(R_C):
            h = g * R_C + r
            s = jnp.where(cmp_valid, s_all[:, r * CH:(r + 1) * CH] + cmpb_ref[h, 0], NEG_INF)
            m = jnp.max(s, axis=0, keepdims=True)
            e = jnp.exp(s - jnp.where(m > NEG_INF, m, 0.0))
            ps.append(e / jnp.maximum(jnp.sum(e, axis=0, keepdims=True), 1e-30))
        p_sum.append(ps[0] + ps[1] + ps[2] + ps[3])
        o_all = _dot(kct_ref[0, g * HEAD_DIM:(g + 1) * HEAD_DIM, :].astype(bf16),
                     jnp.concatenate([p.astype(bf16) for p in ps], axis=1))
        o_cmp += [o_all[:, r * CH:(r + 1) * CH] for r in range(R_C)]

    srow = lax.broadcasted_iota(i32, (ns_pad, CH), 0)
    cur = jnp.right_shift(i_abs * CH + lax.broadcasted_iota(i32, (ns_pad, CH), 1), SEL_BLOCK.bit_length() - 1)
    adm = srow <= cur
    forced = (srow == 0) | (srow == cur) | (srow == cur - 1)
    cov = cover_ref[...]
    for g in range(G_C):
        hi, mid, lo = _split3(p_sum[g])
        imp = (_dot(cov, hi) + _dot(cov, mid) + _dot(cov, lo))[0:ns_pad]
        score = jnp.where(adm, imp + jnp.where(forced, FORCE_SCORE, 0.0), NEG_INF)
        rank = jnp.zeros((ns_pad, CH), f32)
        for s2 in range(n_sel):
            other = score[s2:s2 + 1, :]
            ahead = (other > score) | ((other == score) & (s2 < srow))
            rank = rank + jnp.where(ahead, 1.0, 0.0)
        selm_scr[g, 0:ns_pad, :] = jnp.where(adm & (rank < float(SEL_TOPK)) & (srow < n_sel), 1.0, 0.0)
        for c in range(n_chunks_static):
            blk0 = selm_scr[g, 2 * c:2 * c + 1, :]
            blk1 = selm_scr[g, 2 * c + 1:2 * c + 2, :]
            on = jnp.where(row < SEL_BLOCK, blk0, blk1) > 0.5
            mask_scr[g, c] = jnp.where(on & (c * CH + row <= t_pos), 0.0, NEG_INF)

    def slc_extra(c):
        bidx = jnp.clip(c - i_abs + 2, 0, 2)
        return lambda h, g: toep_ref[h, bidx] + mask_scr[g, c]

    o_slc = _attend(0, nci, q_stack, lambda c0, g, n: get_selk(c0, n), slc_extra, get_selvt, s_scr, acc_scr)

    def win_extra(c):
        bidx = jnp.clip(c - i_abs + 2, 0, 2)
        wd = t_pos - (c * CH + row)
        wmask = jnp.where((wd >= 0) & (wd <= WINDOW), 0.0, NEG_INF)
        return lambda h, g: toep_ref[h, bidx] + wmask

    win_run = min(WINDOW // CH + 1, n_chunks_static)
    win_lo = jnp.clip(i_abs - WINDOW // CH, win_c0, n_chunks_static - win_run)
    o_win = _attend(win_lo, nci, q_stack, lambda c0, g, n: get_wink(c0, n), win_extra, get_winvt, s_scr, acc_scr,
                    run=win_run)

    for h in range(H_C):
        g0 = jax.nn.sigmoid(gate_t[8 + h:9 + h, :])
        g1 = jax.nn.sigmoid(gate_t[16 + h:17 + h, :])
        g2 = jax.nn.sigmoid(gate_t[24 + h:25 + h, :])
        ot_scr[h * HEAD_DIM:(h + 1) * HEAD_DIM, :] = g0 * o_cmp[h] + g1 * o_slc[h] + g2 * o_win[h]
    return ot_scr[...].T


def _nsa_prompt_kernel(q_ref, misc_ref, kc_ref, kct_ref, cmpb_ref, cover_ref, selk_ref, selvt_ref, wink_ref, winvt_ref,
                       toep_ref, o_ref, selm_scr, mask_scr, s_scr, acc_scr, ot_scr, *, n_sel):
    i = pl.program_id(1)

    def rows(c):
        return pl.ds(pl.multiple_of(c * CH, CH), CH)

    o_ref[0] = _nsa_core(
        i, n_sel, 0,
        lambda j: q_ref[0, :, j * CH:(j + 1) * CH],
        misc_ref[0, 0, 0:32, :],
        kc_ref, kct_ref, cmpb_ref, cover_ref,
        lambda c0, n: selk_ref[0, _rows(c0, n), :],
        lambda c0, g, n: _vt_run(selvt_ref.at[0], c0, n, g * HEAD_DIM, (g + 1) * HEAD_DIM),
        lambda c0, n: wink_ref[0, _rows(c0, n), :],
        lambda c0, g, n: _vt_run(winvt_ref.at[0], c0, n, g * HEAD_DIM, (g + 1) * HEAD_DIM),
        toep_ref, selm_scr, mask_scr, s_scr, acc_scr, ot_scr)


def _cover_matrix(n_keys):
    n_cmp = (n_keys - COMP_BLOCK) // COMP_STRIDE + 1
    n_sel = -(-n_keys // SEL_BLOCK)
    c0 = np.arange(n_cmp)[:, None] * COMP_STRIDE
    s0 = np.arange(n_sel)[None, :] * SEL_BLOCK
    cover = ((c0 < s0 + SEL_BLOCK) & (c0 + COMP_BLOCK > s0)).astype(np.float32)
    out = np.zeros((CH, CH), np.float32)
    out[:n_sel, :n_cmp] = cover.T
    return jnp.asarray(out, dtype=bf16), n_sel


def _nsa_prompt(qc, ft, kc, kct, cmpb, kvc4, kvw, toep):
    n, t, _ = qc.shape
    nc = t // CH
    cover, n_sel = _cover_matrix(t)
    return pl.pallas_call(
        functools.partial(_nsa_prompt_kernel, n_sel=n_sel),
        grid=(n, nc),
        in_specs=[pl.BlockSpec((1, CH, 512), lambda b, i: (b, i, 0)),
                  pl.BlockSpec((1, 1, CH, CH), lambda b, i: (b, i, 4, 0)),
                  pl.BlockSpec((1, CH, 256), lambda b, i: (b, 0, 0)),
                  pl.BlockSpec((1, CH, CH), lambda b, i: (b, 0, 0)),
                  pl.BlockSpec((H_C, 1, CH, CH), lambda b, i: (0, i, 0, 0)),
                  pl.BlockSpec((CH, CH), lambda b, i: (0, 0)),
                  pl.BlockSpec((1, t, CH), lambda b, i: (b, 0, 2)),
                  pl.BlockSpec((1, nc, CH, CH), lambda b, i: (b, 0, 2, 0)),
                  pl.BlockSpec((1, t, CH), lambda b, i: (b, 0, 0)),
                  pl.BlockSpec((1, nc, CH, CH), lambda b, i: (b, 0, 3, 0)),
                  pl.BlockSpec((H_C, 3, CH, CH), lambda b, i: (1, 0, 0, 0))],
        out_specs=pl.BlockSpec((1, CH, 512), lambda b, i: (b, i, 0)),
        out_shape=jax.ShapeDtypeStruct((n, t, 512), f32),
        scratch_shapes=[pltpu.VMEM((G_C, CH, CH), f32), pltpu.VMEM((G_C, nc, CH, CH), f32),
                        pltpu.VMEM((H_C, nc, CH, CH), f32), pltpu.VMEM((G_C, HEAD_DIM, R_C * CH), f32),
                        pltpu.VMEM((512, CH), f32)],
        compiler_params=_cparams(2),
    )(qc, ft, kc, kct, cmpb, cover, kvc4, ft, kvw, ft, toep)


def _rows_softmax_pv(n, score_fn, pv_fn, s_scr, nrow):
    macc = jnp.full((nrow, CH), NEG_INF, f32)
    for i in range(n):
        s = score_fn(i)
        s_scr[i] = s
        macc = jnp.maximum(macc, s)
    m = jnp.max(macc, axis=1, keepdims=True)
    m = jnp.where(m > NEG_INF, m, 0.0)
    lacc = jnp.zeros((nrow, CH), f32)
    oacc = jnp.zeros((nrow, CH), f32)
    for i in range(n):
        p = jnp.exp(s_scr[i] - m)
        lacc = lacc + p
        oacc = oacc + pv_fn(i, p.astype(bf16))
    return oacc * (1.0 / jnp.maximum(_rowsum(lacc), 1e-30))


def _nsa_sample_kernel(pt_ref, q_ref, ikn_ref, kvn_ref, kvwn_ref, win_ref, kc_ref, cover_ref, grp_ref, bias_ref,
                       cmpb_ref, *refs, n_pages, n_sel, n_win):
    pages = refs[:n_pages]
    o_ref, qrow_scr, gate_scr, knew_scr, vnew_scr, wknew_scr, wvnew_scr, s_scr = refs[n_pages:]
    t_new = q_ref.shape[1]
    nrow = H_C * t_new
    nc = n_pages + 1
    q0 = n_pages * PAGE_SIZE
    win_c0 = n_pages - n_win
    scale = HEAD_DIM ** -0.5
    row = lax.broadcasted_iota(i32, (nrow, CH), 0)
    lane = lax.broadcasted_iota(i32, (nrow, CH), 1)
    t_pos = q0 + row - (row // t_new) * t_new
    row8 = lax.broadcasted_iota(i32, (8, CH), 0)
    lane8 = lax.broadcasted_iota(i32, (8, CH), 1)

    qrow_scr[...] = jnp.zeros(qrow_scr.shape, f32)
    ikn = ikn_ref[0]
    gate_col0 = D_IDX + H_IDX
    for h in range(H_C):
        g = h // R_C
        qrow_scr[h * t_new:(h + 1) * t_new, g * HEAD_DIM:(g + 1) * HEAD_DIM] = (
            q_ref[0][:, h * HEAD_DIM:(h + 1) * HEAD_DIM] * scale)
        for br in range(3):
            col = gate_col0 + br * H_C + h
            gate_scr[br, h * t_new:(h + 1) * t_new, :] = jnp.broadcast_to(ikn[:, col:col + 1], (t_new, CH))
    qrows = qrow_scr[...].astype(bf16)
    _pad_rows(knew_scr, kvn_ref[0][:, 256:384])
    _pad_rows(vnew_scr, kvn_ref[0][:, 384:512])
    _pad_rows(wknew_scr, kvwn_ref[0][:, 0:CH])
    _pad_rows(wvnew_scr, kvwn_ref[0][:, CH:2 * CH])

    s = _dot_nt(qrows, kc_ref[0, :, 0:CH].astype(bf16)) + cmpb_ref[0]
    s = jnp.where(t_pos >= lane * COMP_STRIDE + COMP_BLOCK - 1, s, NEG_INF)
    m = jnp.max(s, axis=1, keepdims=True)
    e = jnp.exp(s - jnp.where(m > NEG_INF, m, 0.0))
    p_cmp = e / jnp.maximum(_rowsum(e), 1e-30)
    o_cmp = _dot(p_cmp.astype(bf16), kc_ref[0, :, CH:2 * CH].astype(bf16))

    grp = grp_ref[...]
    cov = cover_ref[...]
    p_sum = sum(_dot(grp, part) for part in _split3(p_cmp))
    imp = sum(_dot(part, cov) for part in _split3(p_sum))
    cur = jnp.right_shift(q0 + row8 - (row8 // t_new) * t_new, SEL_BLOCK.bit_length() - 1)
    adm = lane8 <= cur
    forced = (lane8 == 0) | (lane8 == cur) | (lane8 == cur - 1)
    score = jnp.where(adm, imp + jnp.where(forced, FORCE_SCORE, 0.0), NEG_INF)
    rank = jnp.zeros((8, CH), f32)
    for s2 in range(n_sel):
        other = score[:, s2:s2 + 1]
        rank = rank + jnp.where((other > score) | ((other == score) & (s2 < lane8)), 1.0, 0.0)
    selm = jnp.where(adm & (rank < float(SEL_TOPK)) & (lane8 < n_sel), 1.0, 0.0)
    swapped = pltpu.roll(selm, t_new, 0)
    tiles = [jnp.where(row8 < t_new, selm, swapped), jnp.where(row8 < t_new, swapped, selm)]
    selm_rows = jnp.concatenate([tiles[(8 * i // t_new) // R_C] for i in range(nrow // 8)], axis=0)

    def slc_mask(c):
        on = jnp.where(lane < SEL_BLOCK, selm_rows[:, 2 * c:2 * c + 1], selm_rows[:, 2 * c + 1:2 * c + 2]) > 0.5
        if c == n_pages:
            on = on & (c * CH + lane <= t_pos)
        return jnp.where(on, 0.0, NEG_INF)

    def slc_score(c):
        if c < n_pages:
            s = _dot(qrows, pages[c][0, 0][0:CH, :].astype(bf16))
        else:
            s = _dot_nt(qrows, knew_scr[...].astype(bf16))
        return s + bias_ref[c] + slc_mask(c)

    def slc_pv(c, p):
        if c < n_pages:
            return _dot_nt(p, pages[c][0, 0][CH:2 * CH, :].astype(bf16))
        return _dot(p, vnew_scr[...].astype(bf16))

    o_slc = _rows_softmax_pv(nc, slc_score, slc_pv, s_scr, nrow)

    def win_score(w):
        c = win_c0 + w
        if w < n_win:
            s = _dot(qrows, win_ref[0, 0, 0:CH, w * CH:(w + 1) * CH].astype(bf16))
        else:
            s = _dot_nt(qrows, wknew_scr[...].astype(bf16))
        wd = t_pos - (c * CH + lane)
        return jnp.where((wd >= 0) & (wd <= WINDOW), s + bias_ref[c], NEG_INF)

    def win_pv(w, p):
        if w < n_win:
            return _dot_nt(p, win_ref[0, 0, CH:2 * CH, w * CH:(w + 1) * CH].astype(bf16))
        return _dot(p, wvnew_scr[...].astype(bf16))

    o_win = _rows_softmax_pv(n_win + 1, win_score, win_pv, s_scr, nrow)

    o = (jax.nn.sigmoid(gate_scr[0]) * o_cmp + jax.nn.sigmoid(gate_scr[1]) * o_slc
         + jax.nn.sigmoid(gate_scr[2]) * o_win)
    for h in range(H_C):
        g = h // R_C
        o_ref[0, :, h * HEAD_DIM:(h + 1) * HEAD_DIM] = o[h * t_new:(h + 1) * t_new, g * HEAD_DIM:(g + 1) * HEAD_DIM]


def _nsa_sample(page_table, layer, qc, ikp, kvc4, kvw, win_buf, kc, nsa_pool, rowbias, cmprow):
    n, t_new, _ = qc.shape
    n_pages = page_table.shape[1]
    nc = n_pages + 1
    w_buf = win_buf.shape[3]
    assert w_buf % CH == 0 and w_buf <= n_pages * PAGE_SIZE
    n_win = w_buf // CH
    nrow = H_C * t_new
    assert 8 % t_new == 0 and G_C == 2 and rowbias.shape == (nc, nrow, CH)
    cover, n_sel = _cover_matrix(n_pages * PAGE_SIZE + t_new)
    cover = cover.T
    grp = np.zeros((8, nrow), np.float32)
    for h in range(H_C):
        for j in range(t_new):
            grp[(h // R_C) * t_new + j, h * t_new + j] = 1.0
    grp = jnp.asarray(grp, dtype=bf16)
    seq = lambda w: pl.BlockSpec((1, t_new, w), lambda b, pt: (b, 0, 0))
    grid_spec = pltpu.PrefetchScalarGridSpec(
        num_scalar_prefetch=1, grid=(n,),
        in_specs=[seq(512), seq(CH), seq(512), seq(256),
                  pl.BlockSpec((1, 1, 256, w_buf), lambda b, pt: (b, layer, 0, 0)),
                  pl.BlockSpec((1, CH, 256), lambda b, pt: (b, 0, 0)),
                  pl.BlockSpec((CH, CH), lambda b, pt: (0, 0)),
                  pl.BlockSpec((8, nrow), lambda b, pt: (0, 0)),
                  pl.BlockSpec((nc, nrow, CH), lambda b, pt: (0, 0, 0)),
                  pl.BlockSpec((1, nrow, CH), lambda b, pt: (0, 0, 0))]
        + _page_specs(n_pages, layer, 256, 1),
        out_specs=pl.BlockSpec((1, t_new, 512), lambda b, pt: (b, 0, 0)),
        scratch_shapes=[pltpu.VMEM((nrow, CH), f32), pltpu.VMEM((3, nrow, CH), f32),
                        pltpu.VMEM((CH, CH), f32), pltpu.VMEM((CH, CH), f32),
                        pltpu.VMEM((CH, CH), f32), pltpu.VMEM((CH, CH), f32),
                        pltpu.VMEM((nc, nrow, CH), f32)])
    return pl.pallas_call(
        functools.partial(_nsa_sample_kernel, n_pages=n_pages, n_sel=n_sel, n_win=n_win),
        grid_spec=grid_spec, out_shape=jax.ShapeDtypeStruct((n, t_new, 512), f32),
        compiler_params=_cparams(1),
    )(page_table, qc, ikp, kvc4, kvw, win_buf, kc, cover, grp, rowbias, cmprow, *([nsa_pool] * n_pages))


def _conv_kernel(glu_ref, hist_ref, cw_ref, cb_ref, lg_ref, lb_ref, wob_ref, ob_ref, rows_ref, ext_scr, *, tm, stride):
    hp = ext_scr.shape[0] - tm
    keep = (CONV_W - 1) * stride
    i = pl.program_id(1)

    @pl.when(i == 0)
    def _():
        ext_scr[0:hp, :] = hist_ref[0]

    glu = glu_ref[0]
    ext_scr[hp:hp + tm, :] = glu[:, 0:D_CONV] * jax.nn.sigmoid(glu[:, D_CONV:2 * D_CONV])
    y = jnp.zeros((tm, D_CONV), f32) + cb_ref[...]
    for w in range(CONV_W):
        y = y + ext_scr[hp - keep + w * stride:hp - keep + w * stride + tm, :] * cw_ref[w:w + 1, :]
    mu = jnp.mean(y, axis=-1, keepdims=True)
    var = jnp.mean(jnp.square(y - mu), axis=-1, keepdims=True)
    yn = (y - mu) * lax.rsqrt(var + EPS) * lg_ref[...] + lb_ref[...]
    act = (yn * jax.nn.sigmoid(yn)).astype(bf16)
    ob_ref[0] = _dot(act, wob_ref[...])
    tail = ext_scr[hp + tm - keep:hp + tm, :]
    rows_ref[0] = tail
    ext_scr[hp - keep:hp, :] = tail


def _conv_module(glu, hist, cw, cb, lg, lb, wob, tm, stride):
    nb, t, _ = glu.shape
    hp = hist.shape[1]
    keep = (CONV_W - 1) * stride
    d = wob.shape[1]
    return pl.pallas_call(
        functools.partial(_conv_kernel, tm=tm, stride=stride),
        grid=(nb, t // tm),
        in_specs=[pl.BlockSpec((1, tm, 2 * D_CONV), lambda n, i: (n, i, 0)),
                  pl.BlockSpec((1, hp, D_CONV), lambda n, i: (n, 0, 0)),
                  _const_spec(cw), _const_spec(cb), _const_spec(lg), _const_spec(lb), _const_spec(wob)],
        out_specs=[pl.BlockSpec((1, tm, d), lambda n, i: (n, i, 0)),
                   pl.BlockSpec((1, keep, D_CONV), lambda n, i: (n, 0, 0))],
        out_shape=[jax.ShapeDtypeStruct((nb, t, d), f32), jax.ShapeDtypeStruct((nb, keep, D_CONV), f32)],
        scratch_shapes=[pltpu.VMEM((hp + tm, D_CONV), f32)],
        compiler_params=_cparams(2),
    )(glu, hist, cw, cb, lg, lb, wob)


def _merge_kernel(x_ref, gt_ref, oa_ref, ob_ref, oc_ref, gm_ref, woa_ref, woc_ref, wo_ref, o_ref):
    d = x_ref.shape[2]
    ya = _dot(oa_ref[0].astype(bf16), woa_ref[...])
    yc = _dot(oc_ref[0].astype(bf16), woc_ref[...])
    gm = gm_ref[0]
    merged = (jax.nn.sigmoid(gm[:, 0:d]) * ya + jax.nn.sigmoid(gm[:, d:2 * d]) * ob_ref[0]
              + jax.nn.sigmoid(gm[:, 2 * d:3 * d]) * yc)
    o_ref[0] = x_ref[0] + gt_ref[0] * _dot(merged.astype(bf16), wo_ref[...])


def _merge(x, gt, oa, ob, oc, gm, woa, woc, wo, tm):
    nb, t, d = x.shape
    tok = lambda w: pl.BlockSpec((1, tm, w), lambda n, i: (n, i, 0))
    return pl.pallas_call(
        _merge_kernel, grid=(nb, t // tm),
        in_specs=[tok(d), _mod_spec(gt, tm), tok(512), tok(d), tok(512), tok(3 * d),
                  _const_spec(woa), _const_spec(woc), _const_spec(wo)],
        out_specs=tok(d), out_shape=jax.ShapeDtypeStruct((nb, t, d), f32),
        compiler_params=_cparams(2),
    )(x, gt, oa, ob, oc, gm, woa, woc, wo)


def _ffn_kernel(x_ref, sh_ref, sc_ref, gt_ref, g_ref, hist_ref, wa_ref, wb_ref, cw_ref, cb_ref, wd_ref,
                o_ref, rows_ref, h_scr, acc_scr, ext_scr, carry_scr, *, tm, stride):
    hp = ext_scr.shape[0] - tm
    i = pl.program_id(1)
    j = pl.program_id(2)

    @pl.when(j == 0)
    def _():
        x = x_ref[0]
        y = x * lax.rsqrt(jnp.mean(x * x, axis=-1, keepdims=True) + EPS) * g_ref[...]
        h_scr[...] = (y * (1.0 + sc_ref[0]) + sh_ref[0]).astype(bf16)
        acc_scr[...] = jnp.zeros(acc_scr.shape, f32)

    @pl.when(i == 0)
    def _():
        ext_scr[0:hp, :] = hist_ref[...]

    @pl.when(i > 0)
    def _():
        ext_scr[0:hp, :] = carry_scr[j]

    hb = h_scr[...]
    a = _dot(hb, wa_ref[...])
    b = _dot(hb, wb_ref[...])
    ext_scr[hp:hp + tm, :] = a
    conv = (ext_scr[hp - 2 * stride:hp - 2 * stride + tm, :] * cw_ref[0:1, :]
            + ext_scr[hp - stride:hp - stride + tm, :] * cw_ref[1:2, :]
            + a * cw_ref[2:3, :] + cb_ref[...])
    act = (jax.nn.gelu(conv, approximate=True) * b).astype(bf16)
    acc_scr[...] += _dot(act, wd_ref[...])
    tail = ext_scr[tm:tm + hp, :]
    carry_scr[j] = tail
    rows_ref[0, 0] = tail

    @pl.when(j == pl.num_programs(2) - 1)
    def _():
        o_ref[0] = x_ref[0] + gt_ref[0] * acc_scr[...]


def _ffn(x, sh, sc, gt, g, hist, wa, wb, cw, cb, wd, tm, tf, stride):
    nb, t, d = x.shape
    dff = wa.shape[1]
    hp = hist.shape[0]
    nj = dff // tf
    tok = pl.BlockSpec((1, tm, d), lambda n, i, j: (n, i, 0))

    def mod_spec(m):
        if m.shape[1] == 1:
            return pl.BlockSpec((1, 1, d), lambda n, i, j: (n, 0, 0))
        return pl.BlockSpec((1, tm, d), lambda n, i, j: (n, i, 0))

    return pl.pallas_call(
        functools.partial(_ffn_kernel, tm=tm, stride=stride),
        grid=(nb, t // tm, nj),
        in_specs=[tok, mod_spec(sh), mod_spec(sc), mod_spec(gt),
                  pl.BlockSpec((1, d), lambda n, i, j: (0, 0)),
                  pl.BlockSpec((hp, tf), lambda n, i, j: (0, j)),
                  pl.BlockSpec((d, tf), lambda n, i, j: (0, j)),
                  pl.BlockSpec((d, tf), lambda n, i, j: (0, j)),
                  pl.BlockSpec((FFN_CONV_W, tf), lambda n, i, j: (0, j)),
                  pl.BlockSpec((1, tf), lambda n, i, j: (0, j)),
                  pl.BlockSpec((tf, d), lambda n, i, j: (j, 0))],
        out_specs=[tok, pl.BlockSpec((1, 1, hp, tf), lambda n, i, j: (n, i, 0, j))],
        out_shape=[jax.ShapeDtypeStruct((nb, t, d), f32), jax.ShapeDtypeStruct((nb, t // tm, hp, dff), f32)],
        scratch_shapes=[pltpu.VMEM((tm, d), bf16), pltpu.VMEM((tm, d), f32), pltpu.VMEM((hp + tm, tf), f32),
                        pltpu.VMEM((nj, hp, tf), f32)],
        compiler_params=_cparams(3),
    )(x, sh, sc, gt, g, hist, wa, wb, cw, cb, wd)


def _final_kernel(x_ref, g_ref, o_ref):
    x = x_ref[0]
    o_ref[0] = x * lax.rsqrt(jnp.mean(x * x, axis=-1, keepdims=True) + EPS) * g_ref[...]


def _final_norm(x, g, tm):
    nb, t, d = x.shape
    tok = pl.BlockSpec((1, tm, d), lambda n, i: (n, i, 0))
    return pl.pallas_call(
        _final_kernel, grid=(nb, t // tm), in_specs=[tok, pl.BlockSpec((1, d), lambda n, i: (0, 0))],
        out_specs=tok, out_shape=jax.ShapeDtypeStruct((nb, t, d), f32), compiler_params=_cparams(2),
    )(x, g)


def _layer_weights(w_in, w_cmp, pe_cmp):
    d = w_in.shape[0]
    offs = np.cumsum([0, H_A * HEAD_DIM, G_A * HEAD_DIM, G_A * HEAD_DIM, H_IDX * D_IDX, H_IDX, D_IDX, 2 * D_CONV,
                      H_C * HEAD_DIM, 6 * G_C * HEAD_DIM, 3 * H_C, 3 * d])
    qa, ka, va, iq, iw, ik, glu, qc, kvc, gc, gm = [w_in[:, offs[k]:offs[k + 1]] for k in range(11)]
    ikp = jnp.concatenate([ik, iw, gc, jnp.zeros((d, CH - D_IDX - H_IDX - 3 * H_C), f32)], axis=1)
    ws = [qa, jnp.concatenate([ka, va], axis=1), iq, ikp, glu, qc, kvc[:, 0:512], kvc[:, 512:768], gm]
    ws = [w.astype(bf16) for w in ws]
    misc = jnp.concatenate([iw, jnp.zeros((d, 8 - H_IDX), f32), gc, jnp.zeros((d, CH - 8 - 3 * H_C), f32)], axis=1)
    wft = jnp.concatenate([va, kvc[:, 384:512], kvc[:, 640:768], misc], axis=1).T.astype(bf16)
    wbd = jnp.zeros((2, COMP_BLOCK, CH, CH), f32)
    for s in range(2):
        blk = w_cmp[s].reshape(COMP_BLOCK, HEAD_DIM, HEAD_DIM)
        for g in range(G_C):
            wbd = wbd.at[s, :, g * HEAD_DIM:(g + 1) * HEAD_DIM, g * HEAD_DIM:(g + 1) * HEAD_DIM].set(blk)
    pe256 = jnp.concatenate([pe_cmp] * G_C, axis=2)
    return ws, wft, wbd.astype(bf16), pe256


def kernel(x_prompt, x_sample, c_prompt, c_sample, cache_dsa_kv, cache_dsa_idx, cache_nsa_kv, state_nsa_win,
           state_conv, state_ffn, page_table, rel_bias, w_mod, b_mod, g_mix, g_ffn, w_in, w_cmp, pe_cmp, conv_w,
           conv_b, ln_g, ln_b, w_oa, w_ob, w_oc, w_o, w_up, ffn_conv_w, ffn_conv_b, w_down, g_final):
    n_p, t_p, d = x_prompt.shape
    n_s, t_s, _ = x_sample.shape
    depth = w_mod.shape[0]
    dff = w_down.shape[1]
    n_pool = cache_dsa_kv.shape[0]
    n_pages = page_table.shape[1]
    past = n_pages * PAGE_SIZE
    assert t_p % CH == 0 and t_s < COMP_STRIDE and (n_s * t_s) % CH == 0 and n_s % 8 == 0
    tm_p = 256
    rows_s = n_s * t_s

    mod = _modulation(jnp.concatenate([c_prompt, c_sample], axis=0), w_mod, b_mod)
    toep = _toeplitz_bias(rel_bias)
    nqb = past // CH + 1
    cmpb = _cmp_bias(rel_bias, max(nqb, t_p // CH))
    rowbias_a = _row_bias(rel_bias, 0, H_A, t_s, past, nqb)
    rowbias_c = _row_bias(rel_bias, H_A, H_C, t_s, past, nqb)
    cmprow = _row_bias(rel_bias, H_A, H_C, t_s, past, 1, COMP_STRIDE, COMP_BLOCK - 1)

    idx_pool = jnp.transpose(cache_dsa_idx, (0, 1, 3, 2))
    kv_pool = jnp.transpose(cache_dsa_kv, (0, 1, 3, 4, 5, 2)).reshape(n_pool, depth, 512, PAGE_SIZE)
    nsa_pool = jnp.transpose(cache_nsa_kv, (0, 1, 3, 4, 5, 2)).reshape(n_pool, depth, 512, PAGE_SIZE)
    win_t = jnp.transpose(state_nsa_win, (0, 1, 3, 4, 5, 2)).reshape(n_s, depth, 256, -1)

    xp = x_prompt
    xs = jnp.transpose(x_sample, (1, 0, 2)).reshape(1, rows_s, d)
    zeros_conv = jnp.zeros((n_p, 32, D_CONV), f32)
    zeros_ffn = jnp.zeros((8, dff), f32)
    tf = dff // 2

    def to_seq(a):
        return jnp.transpose(a.reshape(t_s, n_s, a.shape[-1]), (1, 0, 2))

    def to_rows(a):
        return jnp.transpose(a, (1, 0, 2)).reshape(1, t_s * n_s, a.shape[-1])

    outs_p = [[] for _ in range(6)]
    outs_s = [[] for _ in range(6)]
    for l in range(depth):
        ws, wft, wbd, pe256 = _layer_weights(w_in[l], w_cmp[l], pe_cmp[l])
        m = [mod[l][:, k * d:(k + 1) * d] for k in range(6)]
        mp = [a[:n_p].reshape(n_p, 1, d) for a in m]
        ms = [jnp.tile(a[n_p:], (t_s, 1)).reshape(1, rows_s, d) for a in m]
        g1 = g_mix[l].reshape(1, d)
        g2 = g_ffn[l].reshape(1, d)
        cw, cb = conv_w[l], conv_b[l].reshape(1, D_CONV)
        lg, lb = ln_g[l].reshape(1, D_CONV), ln_b[l].reshape(1, D_CONV)
        woa, wob, woc, wo = [w[l].astype(bf16) for w in (w_oa, w_ob, w_oc, w_o)]
        wa, wb = w_up[l][:, :dff].astype(bf16), w_up[l][:, dff:].astype(bf16)
        wd = w_down[l].astype(bf16)
        fcw, fcb = ffn_conv_w[l], ffn_conv_b[l].reshape(1, dff)

        qa, kva, iq, ikp, glu, qc, kvc4, kvw, gm, ft = _inproj(xp, mp[0], mp[1], g1, ws, wft, tm_p)
        o_a = _dsa_prompt(qa, iq, ikp, kva, ft, toep)
        kc, kct = _compress_prompt(kvc4, wbd, pe256)
        o_c = _nsa_prompt(qc, ft, kc, kct, cmpb, kvc4, kvw, toep)
        o_b, conv_rows = _conv_module(glu, zeros_conv, cw, cb, lg, lb, wob, tm_p, 1)
        xp = _merge(xp, mp[2], o_a, o_b, o_c, gm, woa, woc, wo, tm_p)
        xp, ffn_rows = _ffn(xp, mp[3], mp[4], mp[5], g2, zeros_ffn, wa, wb, fcw, fcb, wd, 512, tf, 1)
        outs_p[0].append(kva.reshape(n_p, t_p, 2, G_A, HEAD_DIM))
        outs_p[1].append(ikp[:, :, :D_IDX])
        outs_p[2].append(kvc4.reshape(n_p, t_p, 4, G_C, HEAD_DIM))
        outs_p[3].append(kvw[:, t_p - min(WINDOW, t_p):].reshape(n_p, min(WINDOW, t_p), 2, G_C, HEAD_DIM))
        outs_p[4].append(conv_rows)
        outs_p[5].append(ffn_rows[:, -1, 8 - (FFN_CONV_W - 1):])

        qa, kva, iq, ikp, glu, qc, kvc4, kvw, gm, ft = _inproj(xs, ms[0], ms[1], g1, ws, wft, rows_s)
        ft_s = jnp.transpose(jnp.transpose(ft[0], (1, 0, 2)).reshape(FT_ROWS, t_s, n_s), (2, 0, 1))
        qa_s, kva_s, iq_s, ikp_s, qc_s, kvc4_s, kvw_s = [to_seq(a) for a in (qa, kva, iq, ikp, qc, kvc4, kvw)]
        o_a = _dsa_sample(page_table, l, qa_s, iq_s, kva_s, ikp_s, idx_pool, kv_pool, rowbias_a)
        kc, kct = _compress_sample(page_table, l, nsa_pool, wbd, pe256)
        o_c = _nsa_sample(page_table, l, qc_s, ikp_s, kvc4_s, kvw_s, win_t, kc, nsa_pool, rowbias_c, cmprow)
        hist_c = jnp.transpose(state_conv[:, l], (1, 0, 2)).reshape(1, (CONV_W - 1) * n_s, D_CONV)
        o_b, conv_rows = _conv_module(glu, hist_c, cw, cb, lg, lb, wob, rows_s, n_s)
        xs = _merge(xs, ms[2], to_rows(o_a), o_b, to_rows(o_c), gm, woa, woc, wo, rows_s)
        hist_f = jnp.transpose(state_ffn[:, l], (1, 0, 2)).reshape((FFN_CONV_W - 1) * n_s, dff)
        xs, ffn_rows = _ffn(xs, ms[3], ms[4], ms[5], g2, hist_f, wa, wb, fcw, fcb, wd, rows_s, tf, n_s)
        outs_s[0].append(kva_s.reshape(n_s, t_s, 2, G_A, HEAD_DIM))
        outs_s[1].append(ikp_s[:, :, :D_IDX])
        outs_s[2].append(kvc4_s.reshape(n_s, t_s, 4, G_C, HEAD_DIM))
        outs_s[3].append(jnp.transpose(kvw_s, (0, 2, 1)))
        outs_s[4].append(jnp.transpose(conv_rows[0].reshape(CONV_W - 1, n_s, D_CONV), (1, 0, 2)))
        outs_s[5].append(jnp.transpose(ffn_rows[0, 0].reshape(FFN_CONV_W - 1, n_s, dff), (1, 0, 2)))

    y_p = _final_norm(xp, g_final.reshape(1, d), 512)
    y_s = to_seq(_final_norm(xs, g_final.reshape(1, d), rows_s))
    stk = lambda group: [jnp.stack(rows, axis=1) for rows in group]
    keep_w = min(WINDOW, past + t_s)
    win_all = jnp.concatenate([win_t, jnp.stack(outs_s[3], axis=1)], axis=3)
    win_all = win_all[..., win_all.shape[3] - keep_w:].reshape(n_s, depth, 2, G_C, HEAD_DIM, keep_w)
    win_out = jnp.transpose(win_all, (0, 1, 5, 2, 3, 4))
    out_s = stk(outs_s[:3]) + [win_out] + stk(outs_s[4:])
    return tuple([y_p, y_s] + stk(outs_p) + out_s)
```

```python
import functools
import math

import numpy as np
import jax
import jax.numpy as jnp
from jax import lax
from jax.experimental import pallas as pl
from jax.experimental.pallas import tpu as pltpu

HEAD_DIM = 64
H_A, G_A = 8, 4
R_A = H_A // G_A
H_IDX, D_IDX = 4, 64
DSA_TOPK = 256
D_CONV, CONV_W = 512, 31
H_C, G_C = 8, 2
R_C = H_C // G_C
COMP_BLOCK, COMP_STRIDE = 32, 16
SEL_BLOCK, SEL_TOPK = 64, 8
WINDOW = 512
FFN_CONV_W = 3
NUM_BUCKETS, MAX_DISTANCE = 32, 128
FORCE_SCORE = 1.0e4
EPS = 1e-6
PAGE_SIZE = 128

CH = 128
FT_ROWS = 640
VMEM_LIMIT = 56 * 1024 * 1024

f32, bf16, i32 = jnp.float32, jnp.bfloat16, jnp.int32
NEG_INF = float("-inf")


def _cparams(n_axes):
    return pltpu.CompilerParams(dimension_semantics=("arbitrary",) * n_axes, vmem_limit_bytes=VMEM_LIMIT)


def _dot(a, b):
    return jnp.dot(a, b, preferred_element_type=f32)


def _dot_nt(a, b):
    return lax.dot_general(a, b, (((1,), (1,)), ((), ())), preferred_element_type=f32)


def _colsum(x):
    return x.reshape(CH // 8, 8, CH).sum(axis=0)


def _colmax(x):
    return x.reshape(CH // 8, 8, CH).max(axis=0)


def _mod_kernel(c_ref, w_ref, b_ref, o_ref):
    c = c_ref[...]
    a = (c * jax.nn.sigmoid(c)).astype(bf16)
    o_ref[0] = _dot(a, w_ref[0].astype(bf16)) + b_ref[0]


def _modulation(c_all, w_mod, b_mod):
    depth, d, d6 = w_mod.shape
    n = c_all.shape[0]
    tn = 1024
    return pl.pallas_call(
        _mod_kernel,
        grid=(depth, d6 // tn),
        in_specs=[pl.BlockSpec((n, d), lambda l, j: (0, 0)),
                  pl.BlockSpec((1, d, tn), lambda l, j: (l, 0, j)),
                  pl.BlockSpec((1, 1, tn), lambda l, j: (l, 0, j))],
        out_specs=pl.BlockSpec((1, n, tn), lambda l, j: (l, 0, j)),
        out_shape=jax.ShapeDtypeStruct((depth, n, d6), f32),
        compiler_params=_cparams(2),
    )(c_all, w_mod, b_mod.reshape(depth, 1, d6))


def _t5_bucket(n):
    max_exact = NUM_BUCKETS // 2
    nf = jnp.maximum(n, 1).astype(f32)
    large = max_exact + (jnp.log(nf / max_exact) / math.log(MAX_DISTANCE / max_exact)
                         * (NUM_BUCKETS - max_exact)).astype(i32)
    return jnp.where(n < max_exact, n, jnp.minimum(large, NUM_BUCKETS - 1))


def _bias_from_dist(dist, tab_ref, h):
    b = _t5_bucket(jnp.maximum(dist, 0))
    out = jnp.zeros(dist.shape, f32)
    for k in range(NUM_BUCKETS):
        out = jnp.where(b == k, tab_ref[k, h], out)
    return out


def _toep_kernel(tab_ref, o_ref):
    h = pl.program_id(0)
    row = lax.broadcasted_iota(i32, (CH, CH), 0)
    lane = lax.broadcasted_iota(i32, (CH, CH), 1)
    o_ref[0, 0] = _bias_from_dist(jnp.full((CH, CH), 2 * CH, i32), tab_ref, h)
    o_ref[0, 1] = _bias_from_dist(CH + lane - row, tab_ref, h)
    o_ref[0, 2] = _bias_from_dist(lane - row, tab_ref, h)


def _toeplitz_bias(rel_bias):
    nh = rel_bias.shape[1]
    return pl.pallas_call(
        _toep_kernel,
        grid=(nh,),
        in_specs=[pl.BlockSpec(memory_space=pltpu.SMEM)],
        out_specs=pl.BlockSpec((1, 3, CH, CH), lambda h: (h, 0, 0, 0)),
        out_shape=jax.ShapeDtypeStruct((nh, 3, CH, CH), f32),
        compiler_params=_cparams(1),
    )(rel_bias)


def _cmpbias_kernel(tab_ref, o_ref):
    h = pl.program_id(0)
    qb = pl.program_id(1)
    row = lax.broadcasted_iota(i32, (CH, CH), 0)
    lane = lax.broadcasted_iota(i32, (CH, CH), 1)
    dist = qb * CH + lane - (row * COMP_STRIDE + COMP_BLOCK - 1)
    o_ref[0, 0] = _bias_from_dist(dist, tab_ref, H_A + h)


def _cmp_bias(rel_bias, nqb):
    return pl.pallas_call(
        _cmpbias_kernel,
        grid=(H_C, nqb),
        in_specs=[pl.BlockSpec(memory_space=pltpu.SMEM)],
        out_specs=pl.BlockSpec((1, 1, CH, CH), lambda h, q: (h, q, 0, 0)),
        out_shape=jax.ShapeDtypeStruct((H_C, nqb, CH, CH), f32),
        compiler_params=_cparams(2),
    )(rel_bias)


def _inproj_kernel(x_ref, sh_ref, sc_ref, g_ref, *refs, n_w):
    w_refs = refs[:n_w]
    wft_ref = refs[n_w]
    out_refs = refs[n_w + 1:2 * n_w + 1]
    ft_ref = refs[2 * n_w + 1]
    x = x_ref[0]
    y = x * lax.rsqrt(jnp.mean(x * x, axis=-1, keepdims=True) + EPS) * g_ref[...]
    hb = (y * (1.0 + sc_ref[0]) + sh_ref[0]).astype(bf16)
    for w_ref, o_ref in zip(w_refs, out_refs):
        o_ref[0] = _dot(hb, w_ref[...])
    ft = _dot_nt(wft_ref[...], hb)
    for j in range(ft.shape[1] // CH):
        ft_ref[0, j] = ft[:, j * CH:(j + 1) * CH]


def _mod_spec(m, tm):
    if m.shape[1] == 1:
        return pl.BlockSpec((1, 1, m.shape[2]), lambda n, i: (n, 0, 0))
    return pl.BlockSpec((1, tm, m.shape[2]), lambda n, i: (n, i, 0))


def _const_spec(a):
    nd = a.ndim
    return pl.BlockSpec(a.shape, lambda *_: (0,) * nd, pipeline_mode=pl.Buffered(1))


def _inproj(x, sh, sc, g, ws, wft, tm):
    nb, t, d = x.shape
    n_w = len(ws)
    in_specs = [pl.BlockSpec((1, tm, d), lambda n, i: (n, i, 0)), _mod_spec(sh, tm), _mod_spec(sc, tm),
                _const_spec(g)] + [_const_spec(w) for w in ws] + [_const_spec(wft)]
    out_specs = [pl.BlockSpec((1, tm, w.shape[1]), lambda n, i: (n, i, 0)) for w in ws]
    out_specs.append(pl.BlockSpec((1, tm // CH, FT_ROWS, CH), lambda n, i: (n, i, 0, 0)))
    out_shape = [jax.ShapeDtypeStruct((nb, t, w.shape[1]), f32) for w in ws]
    out_shape.append(jax.ShapeDtypeStruct((nb, t // CH, FT_ROWS, CH), f32))
    return pl.pallas_call(
        functools.partial(_inproj_kernel, n_w=n_w),
        grid=(nb, t // tm), in_specs=in_specs, out_specs=out_specs, out_shape=out_shape,
        compiler_params=_cparams(2),
    )(x, sh, sc, g, *ws, wft)


def _sortable(x):
    x = jnp.where(x == 0.0, 0.0, x)
    b = lax.bitcast_convert_type(x, i32)
    return jnp.where(b < 0, b ^ 0x7FFFFFFF, b)


def _head_operand(tile, src_half, dst_half, mask_other):
    if src_half != dst_half:
        tile = pltpu.roll(tile, 64, 1)
    if mask_other:
        lane = lax.broadcasted_iota(i32, tile.shape, 1)
        keep = (lane < 64) if dst_half == 0 else (lane >= 64)
        tile = jnp.where(keep, tile, 0.0)
    return tile.astype(bf16)


def _shr_pow2(x, g):
    return x // g if isinstance(x, int) else jnp.right_shift(x, g.bit_length() - 1)


def _chunk_loop(lo, hi, body, carry, group):
    if isinstance(lo, int) and isinstance(hi, int):
        for c in range(lo, hi):
            carry = body(c, carry)
        return carry

    def gbody(gi, carry):
        for k in range(group):
            carry = body(gi * group + k, carry)
        return carry

    return lax.fori_loop(_shr_pow2(lo, group), _shr_pow2(hi + group - 1, group), gbody, carry)


def _super_loop(lo, hi, body, carry, group):
    if isinstance(lo, int) and isinstance(hi, int):
        c = lo
        while c < hi:
            n = min(group, hi - c)
            carry = body(c, n, carry)
            c += n
        return carry
    return lax.fori_loop(_shr_pow2(lo, group), _shr_pow2(hi + group - 1, group),
                         lambda gi, carry: body(gi * group, group, carry), carry)


ATT_RUN = 4


def _attend(lo, hi, q_stack, get_k, extra_fn, get_vt, s_scr, acc_scr, run=None):
    n_groups = len(q_stack)
    rep = q_stack[0].shape[0] // CH
    nh = n_groups * rep

    def p1(c0, n, macc):
        adds = [extra_fn(c0 + k) for k in range(n)]
        macc = list(macc)
        for g in range(n_groups):
            s_all = _dot_nt(get_k(c0, g, n).astype(bf16), q_stack[g])
            for k in range(n):
                for r in range(rep):
                    h = g * rep + r
                    s = s_all[k * CH:(k + 1) * CH, r * CH:(r + 1) * CH] + adds[k](h, g)
                    s_scr[h, c0 + k] = s
                    macc[h] = jnp.maximum(macc[h], _colmax(s))
        return tuple(macc)

    def walk(body, carry):
        if run is not None:
            return body(lo, run, carry)
        return _super_loop(lo, hi, body, carry, ATT_RUN)

    macc = walk(p1, tuple(jnp.full((8, CH), NEG_INF, f32) for _ in range(nh)))
    m = []
    for h in range(nh):
        mh = jnp.max(macc[h], axis=0, keepdims=True)
        m.append(jnp.where(mh > NEG_INF, mh, 0.0))
    for g in range(n_groups):
        acc_scr[g] = jnp.zeros(acc_scr.shape[1:], f32)

    def p2(c0, n, lacc):
        lacc = list(lacc)
        for g in range(n_groups):
            rows = []
            for k in range(n):
                ps = []
                for r in range(rep):
                    h = g * rep + r
                    p = jnp.exp(s_scr[h, c0 + k] - m[h])
                    lacc[h] = lacc[h] + _colsum(p)
                    ps.append(p.astype(bf16))
                rows.append(jnp.concatenate(ps, axis=1))
            p_run = rows[0] if n == 1 else jnp.concatenate(rows, axis=0)
            acc_scr[g] += _dot(get_vt(c0, g, n).astype(bf16), p_run)
        return tuple(lacc)

    lacc = walk(p2, tuple(jnp.zeros((8, CH), f32) for _ in range(nh)))
    outs = []
    for h in range(nh):
        g, r = divmod(h, rep)
        inv = 1.0 / jnp.maximum(jnp.sum(lacc[h], axis=0, keepdims=True), 1e-30)
        outs.append(acc_scr[g, :, r * CH:(r + 1) * CH] * inv)
    return outs


def _rows(c0, n):
    start = c0 * CH if isinstance(c0, int) else pl.multiple_of(c0 * CH, CH)
    return pl.ds(start, n * CH)


def _vt_run(ref, c0, n, lo, hi):
    tiles = [ref[c0 + k, lo:hi, :] for k in range(n)]
    return tiles[0] if n == 1 else jnp.concatenate(tiles, axis=1)


def _dsa_core(i_abs, nci, topk, pos_bits, n_valid, q_tile, iq_tile, iw_t, get_ik, get_k, get_vt, toep_ref,
              keys_scr, mask_scr, s_scr, acc_scr, ot_scr):
    row = lax.broadcasted_iota(i32, (CH, CH), 0)
    lane = lax.broadcasted_iota(i32, (CH, CH), 1)
    t_pos = i_abs * CH + lane

    iq_stack = jnp.concatenate([_head_operand(iq_tile(h // 2), h % 2, 0, True) for h in range(H_IDX)], axis=0)

    def idx_body(c, carry):
        s_all = _dot_nt(get_ik(c).astype(bf16), iq_stack)
        acc = jnp.zeros((CH, CH), f32)
        for h in range(H_IDX):
            acc = acc + iw_t[h:h + 1, :] * jnp.maximum(s_all[:, h * CH:(h + 1) * CH], 0.0)
        acc = jnp.where(c * CH + row <= t_pos, acc, NEG_INF)
        keys_scr[c] = _sortable(acc)
        return carry

    _chunk_loop(0, nci, idx_body, 0, 4)

    def count(pred_fn):
        def body(c, acc):
            return acc + _colsum(jnp.where(pred_fn(c, keys_scr[c]), 1.0, 0.0))
        acc = _chunk_loop(0, nci, body, jnp.zeros((8, CH), f32), 4)
        return jnp.sum(acc, axis=0, keepdims=True)

    kf = float(topk)
    int_min = jnp.full((1, CH), -2 ** 31, i32)
    c0 = count(lambda c, k: k >= 0)
    cand = jnp.where(c0 >= kf, jnp.zeros((1, CH), i32), int_min)

    def bit_body(j, cand):
        trial = cand | jnp.left_shift(jnp.int32(1), 30 - j)
        cnt = count(lambda c, k: k >= trial)
        return jnp.where(cnt >= kf, trial, cand)

    cand = lax.fori_loop(0, 31, bit_body, cand)
    cnt_ge = count(lambda c, k: k >= cand)
    tied = (cnt_ge > kf) & (lane[0:1, :] < n_valid)
    has_tie = jnp.max(jnp.where(tied, 1.0, 0.0)) > 0.5

    @pl.when(jnp.logical_not(has_tie))
    def _():
        def mask_body(c, carry):
            sel = (keys_scr[c] >= cand) & (c * CH + row <= t_pos)
            mask_scr[c] = jnp.where(sel, 0.0, NEG_INF)
            return carry
        _chunk_loop(0, nci, mask_body, 0, 4)

    @pl.when(has_tie)
    def _():
        need = kf - count(lambda c, k: k > cand)

        def pos_body(j, pcut):
            trial = pcut | jnp.left_shift(jnp.int32(1), pos_bits - 1 - j)
            cnt = count(lambda c, k: (k == cand) & (c * CH + row < trial))
            return jnp.where(cnt < need, trial, pcut)

        pcut = lax.fori_loop(0, pos_bits, pos_body, jnp.zeros((1, CH), i32))

        def mask_body(c, carry):
            k = keys_scr[c]
            s_pos = c * CH + row
            sel = ((k > cand) | ((k == cand) & (s_pos <= pcut))) & (s_pos <= t_pos)
            mask_scr[c] = jnp.where(sel, 0.0, NEG_INF)
            return carry
        _chunk_loop(0, nci, mask_body, 0, 4)

    scale = HEAD_DIM ** -0.5
    q_stack = []
    for g in range(G_A):
        heads = [_head_operand(q_tile((g * R_A + r) // 2) * scale, (g * R_A + r) % 2, g % 2, True) for r in range(R_A)]
        q_stack.append(jnp.concatenate(heads, axis=0))

    def extra_fn(c):
        bidx = jnp.clip(c - i_abs + 2, 0, 2)
        mk = mask_scr[c]
        return lambda h, g: toep_ref[h, bidx] + mk

    outs = _attend(0, nci, q_stack, lambda c0, g, n: get_k(c0, g // 2, n), extra_fn, get_vt, s_scr, acc_scr)
    for h in range(H_A):
        ot_scr[h * HEAD_DIM:(h + 1) * HEAD_DIM, :] = outs[h]
    return ot_scr[...].T


def _dsa_prompt_kernel(q_ref, iq_ref, misc_ref, ik_ref, k_ref, vt_ref, toep_ref, o_ref,
                       keys_scr, mask_scr, s_scr, acc_scr, ot_scr, *, topk, pos_bits):
    i = pl.program_id(1)

    def rows(c):
        return pl.ds(pl.multiple_of(c * CH, CH), CH)

    o_ref[0] = _dsa_core(
        i, i + 1, topk, pos_bits, CH,
        lambda j: q_ref[0, :, j * CH:(j + 1) * CH],
        lambda j: iq_ref[0, :, j * CH:(j + 1) * CH],
        misc_ref[0, 0, 0:8, :],
        lambda c: ik_ref[0, rows(c), :],
        lambda c0, j, n: k_ref[0, _rows(c0, n), j * CH:(j + 1) * CH],
        lambda c0, g, n: _vt_run(vt_ref.at[0], c0, n, g * HEAD_DIM, (g + 1) * HEAD_DIM),
        toep_ref, keys_scr, mask_scr, s_scr, acc_scr, ot_scr)


def _dsa_prompt(qa, iq, ikp, kva, ft, toep):
    n, t, _ = qa.shape
    nc = t // CH
    topk = min(DSA_TOPK, t // 4)
    assert nc % 4 == 0, "traced chunk loops walk aligned groups of up to 4 chunks"
    kern = functools.partial(_dsa_prompt_kernel, topk=topk, pos_bits=max(1, (t - 1).bit_length()))
    return pl.pallas_call(
        kern,
        grid=(n, nc),
        in_specs=[pl.BlockSpec((1, CH, 512), lambda b, i: (b, i, 0)),
                  pl.BlockSpec((1, CH, 256), lambda b, i: (b, i, 0)),
                  pl.BlockSpec((1, 1, CH, CH), lambda b, i: (b, i, 4, 0)),
                  pl.BlockSpec((1, t, CH), lambda b, i: (b, 0, 0)),
                  pl.BlockSpec((1, t, 256), lambda b, i: (b, 0, 0)),
                  pl.BlockSpec((1, nc, 256, CH), lambda b, i: (b, 0, 0, 0)),
                  pl.BlockSpec((H_A, 3, CH, CH), lambda b, i: (0, 0, 0, 0))],
        out_specs=pl.BlockSpec((1, CH, 512), lambda b, i: (b, i, 0)),
        out_shape=jax.ShapeDtypeStruct((n, t, 512), f32),
        scratch_shapes=[pltpu.VMEM((nc, CH, CH), i32), pltpu.VMEM((nc, CH, CH), f32),
                        pltpu.VMEM((H_A, nc, CH, CH), f32), pltpu.VMEM((G_A, HEAD_DIM, R_A * CH), f32),
                        pltpu.VMEM((512, CH), f32)],
        compiler_params=_cparams(2),
    )(qa, iq, ft, ikp, kva, ft, toep)


def _pad_rows(ref, x):
    ref[...] = jnp.zeros(ref.shape, f32)
    ref[0:x.shape[0], :] = x


def _rowbias_kernel(tab_ref, o_ref, *, head0, q0, n_heads, t_new, key_mul, key_add):
    c = pl.program_id(0)
    nrow = n_heads * t_new
    row = lax.broadcasted_iota(i32, (nrow, CH), 0)
    lane = lax.broadcasted_iota(i32, (nrow, CH), 1)
    hrow = row // t_new
    b = _t5_bucket(jnp.maximum(q0 + (row - hrow * t_new) - ((c * CH + lane) * key_mul + key_add), 0))
    out = jnp.zeros((nrow, CH), f32)
    for k in range(NUM_BUCKETS):
        for h in range(n_heads):
            out = jnp.where((b == k) & (hrow == h), tab_ref[k, head0 + h], out)
    o_ref[0] = out


def _row_bias(rel_bias, head0, n_heads, t_new, q0, nc, key_mul=1, key_add=0):
    nrow = n_heads * t_new
    return pl.pallas_call(
        functools.partial(_rowbias_kernel, head0=head0, q0=q0, n_heads=n_heads, t_new=t_new, key_mul=key_mul,
                          key_add=key_add),
        grid=(nc,),
        in_specs=[pl.BlockSpec(memory_space=pltpu.SMEM)],
        out_specs=pl.BlockSpec((1, nrow, CH), lambda c: (c, 0, 0)),
        out_shape=jax.ShapeDtypeStruct((nc, nrow, CH), f32),
        compiler_params=_cparams(1),
    )(rel_bias)


def _rowsum(x):
    return jnp.sum(x, axis=1, keepdims=True)


def _dsa_sample_kernel(pt_ref, q_ref, iq_ref, kvn_ref, ikn_ref, *refs, n_pages, topk, pos_bits):
    idx_pages = refs[:n_pages]
    kv_pages = refs[n_pages:2 * n_pages]
    (bias_ref, o_ref, qrow_scr, iqrow_scr, wrow_scr, iknew_scr, knew_scr, vnew_scr,
     keys_scr, mask_scr, s_scr) = refs[2 * n_pages:]
    t_new = q_ref.shape[1]
    nrow = H_A * t_new
    nc = n_pages + 1
    q0 = n_pages * PAGE_SIZE
    lane8 = lax.broadcasted_iota(i32, (8, CH), 1)
    row8 = lax.broadcasted_iota(i32, (8, CH), 0)

    iqrow_scr[...] = jnp.zeros(iqrow_scr.shape, f32)
    wrow_scr[...] = jnp.zeros(wrow_scr.shape, f32)
    ikn = ikn_ref[0]
    for hi in range(H_IDX):
        iqrow_scr[hi * 8:hi * 8 + t_new, 0:D_IDX] = iq_ref[0][:, hi * D_IDX:(hi + 1) * D_IDX]
        wrow_scr[hi * 8:hi * 8 + t_new, :] = jnp.broadcast_to(ikn[:, D_IDX + hi:D_IDX + hi + 1], (t_new, CH))
    qrow_scr[...] = jnp.zeros(qrow_scr.shape, f32)
    scale = HEAD_DIM ** -0.5
    for h in range(H_A):
        g = h // R_A
        qrow_scr[h * t_new:(h + 1) * t_new, g * HEAD_DIM:(g + 1) * HEAD_DIM] = (
            q_ref[0][:, h * HEAD_DIM:(h + 1) * HEAD_DIM] * scale)
    _pad_rows(iknew_scr, ikn)
    _pad_rows(knew_scr, kvn_ref[0][:, 0:256])
    _pad_rows(vnew_scr, kvn_ref[0][:, 256:512])
    iqrows = iqrow_scr[...].astype(bf16)
    qrows = qrow_scr[...].astype(bf16)
    wrow = wrow_scr[...]

    def idx_keys(s, valid):
        s = jnp.maximum(s, 0.0) * wrow
        acc = s[0:8] + s[8:16] + s[16:24] + s[24:32]
        if valid is not None:
            acc = jnp.where(valid, acc, NEG_INF)
        return _sortable(acc)

    zpad = jnp.zeros((CH - D_IDX, CH), f32)
    for p in range(n_pages):
        ikt = jnp.concatenate([idx_pages[p][0, 0], zpad], axis=0).astype(bf16)
        keys_scr[p] = idx_keys(_dot(iqrows, ikt), None)
    keys_scr[n_pages] = idx_keys(_dot_nt(iqrows, iknew_scr[...].astype(bf16)), lane8 <= row8)

    def count(pred_fn):
        acc = jnp.zeros((8, CH), f32)
        for c in range(nc):
            acc = acc + jnp.where(pred_fn(c, keys_scr[c]), 1.0, 0.0)
        return _rowsum(acc)

    def reach(base, trials):
        accs = [jnp.zeros((8, CH), f32) for _ in trials]
        for c in range(nc):
            k = keys_scr[c]
            for i, t in enumerate(trials):
                accs[i] = accs[i] + jnp.where(k >= t, 1.0, 0.0)
        for a, t in zip(accs, trials):
            base = jnp.where(_rowsum(a) >= kf, t, base)
        return base

    kf = float(topk)
    cand = reach(jnp.full((8, 1), -2 ** 31, i32), [jnp.zeros((8, 1), i32)])
    cand = reach(cand, [cand | (1 << 30)])

    def bits_body(j, cand):
        sh = 27 - 3 * j
        return reach(cand, [cand | jnp.left_shift(jnp.int32(v), sh) for v in range(1, 8)])

    cand = lax.fori_loop(0, 10, bits_body, cand)
    cnt_ge = count(lambda c, k: k >= cand)
    tied = (cnt_ge > kf) & (row8[:, 0:1] < t_new)
    has_tie = jnp.max(jnp.where(tied, 1.0, 0.0)) > 0.5

    def adm(c):
        return (c * CH + lane8 <= q0 + row8) if c == n_pages else None

    @pl.when(jnp.logical_not(has_tie))
    def _():
        for c in range(nc):
            sel = keys_scr[c] >= cand
            if adm(c) is not None:
                sel = sel & adm(c)
            mask_scr[c] = jnp.where(sel, 0.0, NEG_INF)

    @pl.when(has_tie)
    def _():
        need = kf - count(lambda c, k: k > cand)

        def pos_body(j, pcut):
            trial = pcut | jnp.left_shift(jnp.int32(1), pos_bits - 1 - j)
            cnt = count(lambda c, k: (k == cand) & (c * CH + lane8 < trial))
            return jnp.where(cnt < need, trial, pcut)

        pcut = lax.fori_loop(0, pos_bits, pos_body, jnp.zeros((8, 1), i32))
        for c in range(nc):
            k = keys_scr[c]
            sel = (k > cand) | ((k == cand) & (c * CH + lane8 <= pcut))
            if adm(c) is not None:
                sel = sel & adm(c)
            mask_scr[c] = jnp.where(sel, 0.0, NEG_INF)

    def rows_mask(c):
        m8 = mask_scr[c]
        m8 = jnp.where(row8 < t_new, m8, pltpu.roll(m8, t_new, 0))
        return jnp.concatenate([m8] * (nrow // 8), axis=0)

    macc = jnp.full((nrow, CH), NEG_INF, f32)
    for c in range(nc):
        if c < n_pages:
            s = _dot(qrows, kv_pages[c][0, 0][0:256, :].astype(bf16))
        else:
            s = _dot_nt(qrows, knew_scr[...].astype(bf16))
        s = s + bias_ref[c] + rows_mask(c)
        s_scr[c] = s
        macc = jnp.maximum(macc, s)
    m = jnp.max(macc, axis=1, keepdims=True)
    m = jnp.where(m > NEG_INF, m, 0.0)
    lacc = jnp.zeros((nrow, CH), f32)
    oacc = jnp.zeros((nrow, 256), f32)
    for c in range(nc):
        p = jnp.exp(s_scr[c] - m)
        lacc = lacc + p
        if c < n_pages:
            oacc = oacc + _dot_nt(p.astype(bf16), kv_pages[c][0, 0][256:512, :].astype(bf16))
        else:
            oacc = oacc + _dot(p.astype(bf16), vnew_scr[...].astype(bf16))
    o = oacc * (1.0 / jnp.maximum(_rowsum(lacc), 1e-30))
    for h in range(H_A):
        g = h // R_A
        o_ref[0, :, h * HEAD_DIM:(h + 1) * HEAD_DIM] = o[h * t_new:(h + 1) * t_new, g * HEAD_DIM:(g + 1) * HEAD_DIM]


def _page_specs(n_pages, layer, rows, row_block=0):
    return [pl.BlockSpec((1, 1, rows, PAGE_SIZE), lambda b, pt, p=p: (pt[b, p], layer, row_block, 0))
            for p in range(n_pages)]


def _dsa_sample(page_table, layer, qa, iq, kva, ikp, idx_pool, kv_pool, rowbias):
    n, t_new, _ = qa.shape
    n_pages = page_table.shape[1]
    nc = n_pages + 1
    lk = n_pages * PAGE_SIZE + t_new
    topk = min(DSA_TOPK, lk // 4)
    nrow = H_A * t_new
    assert nrow % 8 == 0 and t_new <= 4 and rowbias.shape == (nc, nrow, CH)
    kern = functools.partial(_dsa_sample_kernel, n_pages=n_pages, topk=topk, pos_bits=(nc * CH - 1).bit_length())
    seq = lambda w: pl.BlockSpec((1, t_new, w), lambda b, pt: (b, 0, 0))
    grid_spec = pltpu.PrefetchScalarGridSpec(
        num_scalar_prefetch=1, grid=(n,),
        in_specs=[seq(512), seq(256), seq(512), seq(CH)]
        + _page_specs(n_pages, layer, D_IDX) + _page_specs(n_pages, layer, 512)
        + [pl.BlockSpec((nc, nrow, CH), lambda b, pt: (0, 0, 0))],
        out_specs=pl.BlockSpec((1, t_new, 512), lambda b, pt: (b, 0, 0)),
        scratch_shapes=[pltpu.VMEM((nrow, 256), f32), pltpu.VMEM((H_IDX * 8, CH), f32), pltpu.VMEM((H_IDX * 8, CH), f32),
                        pltpu.VMEM((CH, CH), f32), pltpu.VMEM((CH, 256), f32), pltpu.VMEM((CH, 256), f32),
                        pltpu.VMEM((nc, 8, CH), i32), pltpu.VMEM((nc, 8, CH), f32), pltpu.VMEM((nc, nrow, CH), f32)])
    return pl.pallas_call(
        kern, grid_spec=grid_spec, out_shape=jax.ShapeDtypeStruct((n, t_new, 512), f32),
        compiler_params=_cparams(1),
    )(page_table, qa, iq, kva, ikp, *([idx_pool] * n_pages), *([kv_pool] * n_pages), rowbias)


def _compress_core(xk_ref, xv_ref, wbd_ref, pe_ref):
    nr = xk_ref.shape[0] // COMP_STRIDE
    halves = []
    for s, x_ref in enumerate((xk_ref, xv_ref)):
        a0 = jnp.zeros((nr, CH), f32)
        a1 = jnp.zeros((nr, CH), f32)
        for r in range(COMP_STRIDE):
            xr = x_ref[pl.ds(r, nr, stride=COMP_STRIDE), :]
            r2 = COMP_STRIDE + r
            a0 = a0 + _dot((xr + pe_ref[s, r:r + 1, :]).astype(bf16), wbd_ref[s, r])
            a1 = a1 + _dot((xr + pe_ref[s, r2:r2 + 1, :]).astype(bf16), wbd_ref[s, r2])
        kc = a0 + pltpu.roll(a1, nr - 1, 0)
        if nr < CH:
            kc = jnp.concatenate([kc, jnp.zeros((CH - nr, CH), f32)], axis=0)
        halves.append(kc)
    return halves


def _compress_prompt_kernel(xk_ref, xv_ref, wbd_ref, pe_ref, kc_ref, kct_ref):
    kck, kcv = _compress_core(xk_ref.at[0], xv_ref.at[0], wbd_ref, pe_ref)
    kc_ref[0, :, 0:CH] = kck
    kc_ref[0, :, CH:2 * CH] = kcv
    kct_ref[0] = kcv.T


def _compress_out(n):
    return ([pl.BlockSpec((1, CH, 256), lambda b, *_: (b, 0, 0)), pl.BlockSpec((1, CH, CH), lambda b, *_: (b, 0, 0))],
            [jax.ShapeDtypeStruct((n, CH, 256), f32), jax.ShapeDtypeStruct((n, CH, CH), f32)])


def _compress_prompt(kvc4, wbd, pe256):
    n, t, _ = kvc4.shape
    assert t % (8 * COMP_STRIDE) == 0 and t <= COMP_STRIDE * CH
    out_specs, out_shape = _compress_out(n)
    return pl.pallas_call(
        _compress_prompt_kernel, grid=(n,),
        in_specs=[pl.BlockSpec((1, t, CH), lambda b: (b, 0, 0)), pl.BlockSpec((1, t, CH), lambda b: (b, 0, 1)),
                  _const_spec(wbd), _const_spec(pe256)],
        out_specs=out_specs, out_shape=out_shape, compiler_params=_cparams(1),
    )(kvc4, kvc4, wbd, pe256)


def _split3(x):
    hi = x.astype(bf16)
    r1 = x - hi.astype(f32)
    mid = r1.astype(bf16)
    lo = (r1 - mid.astype(f32)).astype(bf16)
    return hi, mid, lo


def _nsa_core(i_abs, n_sel, win_c0, q_tile, gate_t, kc_ref, kct_ref, cmpb_ref, cover_ref, get_selk, get_selvt,
              get_wink, get_winvt, toep_ref, selm_scr, mask_scr, s_scr, acc_scr, ot_scr):
    row = lax.broadcasted_iota(i32, (CH, CH), 0)
    lane = lax.broadcasted_iota(i32, (CH, CH), 1)
    t_pos = i_abs * CH + lane
    scale = HEAD_DIM ** -0.5
    nci = i_abs + 1
    n_chunks_static = mask_scr.shape[1]
    ns_pad = -(-n_sel // 8) * 8

    q_stack = []
    for g in range(G_C):
        heads = [_head_operand(q_tile((g * R_C + r) // 2) * scale, (g * R_C + r) % 2, g, True) for r in range(R_C)]
        q_stack.append(jnp.concatenate(heads, axis=0))

    kc_b = kc_ref[0, :, 0:CH].astype(bf16)
    cmp_valid = t_pos >= row * COMP_STRIDE + COMP_BLOCK - 1
    o_cmp = []
    p_sum = []
    for g in range(G_C):
        s_all = _dot_nt(kc_b, q_stack[g])
        ps = []
        for r in range(R_C):
            h = g * R_C + r
            s = jnp.where(cmp_valid, s_all[:, r * CH:(r + 1) * CH] + cmpb_ref[h, 0], NEG_INF)
            m = jnp.max(s, axis=0, keepdims=True)
            e = jnp.exp(s - jnp.where(m > NEG_INF, m, 0.0))
            ps.append(e / jnp.maximum(jnp.sum(e, axis=0, keepdims=True), 1e-30))
        p_sum.append(ps[0] + ps[1] + ps[2] + ps[3])
        o_all = _dot(kct_ref[0, g * HEAD_DIM:(g + 1) * HEAD_DIM, :].astype(bf16),
                     jnp.concatenate([p.astype(bf16) for p in ps], axis=1))
        o_cmp += [o_all[:, r * CH:(r + 1) * CH] for r in range(R_C)]

    srow = lax.broadcasted_iota(i32, (ns_pad, CH), 0)
    cur = jnp.right_shift(i_abs * CH + lax.broadcasted_iota(i32, (ns_pad, CH), 1), SEL_BLOCK.bit_length() - 1)
    adm = srow <= cur
    forced = (srow == 0) | (srow == cur) | (srow == cur - 1)
    cov = cover_ref[...]
    for g in range(G_C):
        hi, mid, lo = _split3(p_sum[g])
        imp = (_dot(cov, hi) + _dot(cov, mid) + _dot(cov, lo))[0:ns_pad]
        score = jnp.where(adm, imp + jnp.where(forced, FORCE_SCORE, 0.0), NEG_INF)
        rank = jnp.zeros((ns_pad, CH), f32)
        for s2 in range(n_sel):
            other = score[s2:s2 + 1, :]
            ahead = (other > score) | ((other == score) & (s2 < srow))
            rank = rank + jnp.where(ahead, 1.0, 0.0)
        selm_scr[g, 0:ns_pad, :] = jnp.where(adm & (rank < float(SEL_TOPK)) & (srow < n_sel), 1.0, 0.0)
        for c in range(n_chunks_static):
            blk0 = selm_scr[g, 2 * c:2 * c + 1, :]
            blk1 = selm_scr[g, 2 * c + 1:2 * c + 2, :]
            on = jnp.where(row < SEL_BLOCK, blk0, blk1) > 0.5
            mask_scr[g, c] = jnp.where(on & (c * CH + row <= t_pos), 0.0, NEG_INF)

    def slc_extra(c):
        bidx = jnp.clip(c - i_abs + 2, 0, 2)
        return lambda h, g: toep_ref[h, bidx] + mask_scr[g, c]

    o_slc = _attend(0, nci, q_stack, lambda c0, g, n: get_selk(c0, n), slc_extra, get_selvt, s_scr, acc_scr)

    def win_extra(c):
        bidx = jnp.clip(c - i_abs + 2, 0, 2)
        wd = t_pos - (c * CH + row)
        wmask = jnp.where((wd >= 0) & (wd <= WINDOW), 0.0, NEG_INF)
        return lambda h, g: toep_ref[h, bidx] + wmask

    win_run = min(WINDOW // CH + 1, n_chunks_static)
    win_lo = jnp.clip(i_abs - WINDOW // CH, win_c0, n_chunks_static - win_run)
    o_win = _attend(win_lo, nci, q_stack, lambda c0, g, n: get_wink(c0, n), win_extra, get_winvt, s_scr, acc_scr,
                    run=win_run)

    for h in range(H_C):
        g0 = jax.nn.sigmoid(gate_t[8 + h:9 + h, :])
        g1 = jax.nn.sigmoid(gate_t[16 + h:17 + h, :])
        g2 = jax.nn.sigmoid(gate_t[24 + h:25 + h, :])
        ot_scr[h * HEAD_DIM:(h + 1) * HEAD_DIM, :] = g0 * o_cmp[h] + g1 * o_slc[h] + g2 * o_win[h]
    return ot_scr[...].T


def _nsa_prompt_kernel(q_ref, misc_ref, kc_ref, kct_ref, cmpb_ref, cover_ref, selk_ref, selvt_ref, wink_ref, winvt_ref,
                       toep_ref, o_ref, selm_scr, mask_scr, s_scr, acc_scr, ot_scr, *, n_sel):
    i = pl.program_id(1)

    def rows(c):
        return pl.ds(pl.multiple_of(c * CH, CH), CH)

    o_ref[0] = _nsa_core(
        i, n_sel, 0,
        lambda j: q_ref[0, :, j * CH:(j + 1) * CH],
        misc_ref[0, 0, 0:32, :],
        kc_ref, kct_ref, cmpb_ref, cover_ref,
        lambda c0, n: selk_ref[0, _rows(c0, n), :],
        lambda c0, g, n: _vt_run(selvt_ref.at[0], c0, n, g * HEAD_DIM, (g + 1) * HEAD_DIM),
        lambda c0, n: wink_ref[0, _rows(c0, n), :],
        lambda c0, g, n: _vt_run(winvt_ref.at[0], c0, n, g * HEAD_DIM, (g + 1) * HEAD_DIM),
        toep_ref, selm_scr, mask_scr, s_scr, acc_scr, ot_scr)


def _cover_matrix(n_keys):
    n_cmp = (n_keys - COMP_BLOCK) // COMP_STRIDE + 1
    n_sel = -(-n_keys // SEL_BLOCK)
    c0 = np.arange(n_cmp)[:, None] * COMP_STRIDE
    s0 = np.arange(n_sel)[None, :] * SEL_BLOCK
    cover = ((c0 < s0 + SEL_BLOCK) & (c0 + COMP_BLOCK > s0)).astype(np.float32)
    out = np.zeros((CH, CH), np.float32)
    out[:n_sel, :n_cmp] = cover.T
    return jnp.asarray(out, dtype=bf16), n_sel


def _nsa_prompt(qc, ft, kc, kct, cmpb, kvc4, kvw, toep):
    n, t, _ = qc.shape
    nc = t // CH
    cover, n_sel = _cover_matrix(t)
    return pl.pallas_call(
        functools.partial(_nsa_prompt_kernel, n_sel=n_sel),
        grid=(n, nc),
        in_specs=[pl.BlockSpec((1, CH, 512), lambda b, i: (b, i, 0)),
                  pl.BlockSpec((1, 1, CH, CH), lambda b, i: (b, i, 4, 0)),
                  pl.BlockSpec((1, CH, 256), lambda b, i: (b, 0, 0)),
                  pl.BlockSpec((1, CH, CH), lambda b, i: (b, 0, 0)),
                  pl.BlockSpec((H_C, 1, CH, CH), lambda b, i: (0, i, 0, 0)),
                  pl.BlockSpec((CH, CH), lambda b, i: (0, 0)),
                  pl.BlockSpec((1, t, CH), lambda b, i: (b, 0, 2)),
                  pl.BlockSpec((1, nc, CH, CH), lambda b, i: (b, 0, 2, 0)),
                  pl.BlockSpec((1, t, CH), lambda b, i: (b, 0, 0)),
                  pl.BlockSpec((1, nc, CH, CH), lambda b, i: (b, 0, 3, 0)),
                  pl.BlockSpec((H_C, 3, CH, CH), lambda b, i: (1, 0, 0, 0))],
        out_specs=pl.BlockSpec((1, CH, 512), lambda b, i: (b, i, 0)),
        out_shape=jax.ShapeDtypeStruct((n, t, 512), f32),
        scratch_shapes=[pltpu.VMEM((G_C, CH, CH), f32), pltpu.VMEM((G_C, nc, CH, CH), f32),
                        pltpu.VMEM((H_C, nc, CH, CH), f32), pltpu.VMEM((G_C, HEAD_DIM, R_C * CH), f32),
                        pltpu.VMEM((512, CH), f32)],
        compiler_params=_cparams(2),
    )(qc, ft, kc, kct, cmpb, cover, kvc4, ft, kvw, ft, toep)


def _rows_softmax_pv(n, score_fn, pv_fn, s_scr, nrow):
    macc = jnp.full((nrow, CH), NEG_INF, f32)
    for i in range(n):
        s = score_fn(i)
        s_scr[i] = s
        macc = jnp.maximum(macc, s)
    m = jnp.max(macc, axis=1, keepdims=True)
    m = jnp.where(m > NEG_INF, m, 0.0)
    lacc = jnp.zeros((nrow, CH), f32)
    oacc = jnp.zeros((nrow, CH), f32)
    for i in range(n):
        p = jnp.exp(s_scr[i] - m)
        lacc = lacc + p
        oacc = oacc + pv_fn(i, p.astype(bf16))
    return oacc * (1.0 / jnp.maximum(_rowsum(lacc), 1e-30))


def _nsa_sample_kernel(pt_ref, q_ref, ikn_ref, kvn_ref, kvwn_ref, win_ref, wbd_ref, pe_ref, cover_ref, grp_ref, bias_ref,
                       cmpb_ref, *refs, n_pages, n_sel, n_win):
    pages = refs[:n_pages]
    o_ref, qrow_scr, gate_scr, knew_scr, vnew_scr, wknew_scr, wvnew_scr, xk_scr, xv_scr, s_scr = refs[n_pages:]
    t_new = q_ref.shape[1]
    nrow = H_C * t_new
    nc = n_pages + 1
    q0 = n_pages * PAGE_SIZE
    win_c0 = n_pages - n_win
    scale = HEAD_DIM ** -0.5
    row = lax.broadcasted_iota(i32, (nrow, CH), 0)
    lane = lax.broadcasted_iota(i32, (nrow, CH), 1)
    t_pos = q0 + row - (row // t_new) * t_new
    row8 = lax.broadcasted_iota(i32, (8, CH), 0)
    lane8 = lax.broadcasted_iota(i32, (8, CH), 1)

    qrow_scr[...] = jnp.zeros(qrow_scr.shape, f32)
    ikn = ikn_ref[0]
    gate_col0 = D_IDX + H_IDX
    for h in range(H_C):
        g = h // R_C
        qrow_scr[h * t_new:(h + 1) * t_new, g * HEAD_DIM:(g + 1) * HEAD_DIM] = (
            q_ref[0][:, h * HEAD_DIM:(h + 1) * HEAD_DIM] * scale)
        for br in range(3):
            col = gate_col0 + br * H_C + h
            gate_scr[br, h * t_new:(h + 1) * t_new, :] = jnp.broadcast_to(ikn[:, col:col + 1], (t_new, CH))
    qrows = qrow_scr[...].astype(bf16)
    _pad_rows(knew_scr, kvn_ref[0][:, 256:384])
    _pad_rows(vnew_scr, kvn_ref[0][:, 384:512])
    _pad_rows(wknew_scr, kvwn_ref[0][:, 0:CH])
    _pad_rows(wvnew_scr, kvwn_ref[0][:, CH:2 * CH])

    for p in range(n_pages):
        page = pages[p][0, 0]
        xk_scr[p * PAGE_SIZE:(p + 1) * PAGE_SIZE, :] = page[0:CH, :].T
        xv_scr[p * PAGE_SIZE:(p + 1) * PAGE_SIZE, :] = page[CH:2 * CH, :].T
    kck, kcv = _compress_core(xk_scr, xv_scr, wbd_ref, pe_ref)
    s = _dot_nt(qrows, kck.astype(bf16)) + cmpb_ref[0]
    s = jnp.where(t_pos >= lane * COMP_STRIDE + COMP_BLOCK - 1, s, NEG_INF)
    m = jnp.max(s, axis=1, keepdims=True)
    e = jnp.exp(s - jnp.where(m > NEG_INF, m, 0.0))
    p_cmp = e / jnp.maximum(_rowsum(e), 1e-30)
    o_cmp = _dot(p_cmp.astype(bf16), kcv.astype(bf16))

    grp = grp_ref[...]
    cov = cover_ref[...]
    p_sum = sum(_dot(grp, part) for part in _split3(p_cmp))
    imp = sum(_dot(part, cov) for part in _split3(p_sum))
    cur = jnp.right_shift(q0 + row8 - (row8 // t_new) * t_new, SEL_BLOCK.bit_length() - 1)
    adm = lane8 <= cur
    forced = (lane8 == 0) | (lane8 == cur) | (lane8 == cur - 1)
    score = jnp.where(adm, imp + jnp.where(forced, FORCE_SCORE, 0.0), NEG_INF)
    rank = jnp.zeros((8, CH), f32)
    for s2 in range(n_sel):
        other = score[:, s2:s2 + 1]
        rank = rank + jnp.where((other > score) | ((other == score) & (s2 < lane8)), 1.0, 0.0)
    selm = jnp.where(adm & (rank < float(SEL_TOPK)) & (lane8 < n_sel), 1.0, 0.0)
    swapped = pltpu.roll(selm, t_new, 0)
    tiles = [jnp.where(row8 < t_new, selm, swapped), jnp.where(row8 < t_new, swapped, selm)]
    selm_rows = jnp.concatenate([tiles[(8 * i // t_new) // R_C] for i in range(nrow // 8)], axis=0)

    def slc_mask(c):
        on = jnp.where(lane < SEL_BLOCK, selm_rows[:, 2 * c:2 * c + 1], selm_rows[:, 2 * c + 1:2 * c + 2]) > 0.5
        if c == n_pages:
            on = on & (c * CH + lane <= t_pos)
        return jnp.where(on, 0.0, NEG_INF)

    def slc_score(c):
        if c < n_pages:
            s = _dot(qrows, pages[c][0, 0][2 * CH:3 * CH, :].astype(bf16))
        else:
            s = _dot_nt(qrows, knew_scr[...].astype(bf16))
        return s + bias_ref[c] + slc_mask(c)

    def slc_pv(c, p):
        if c < n_pages:
            return _dot_nt(p, pages[c][0, 0][3 * CH:4 * CH, :].astype(bf16))
        return _dot(p, vnew_scr[...].astype(bf16))

    o_slc = _rows_softmax_pv(nc, slc_score, slc_pv, s_scr, nrow)

    def win_score(w):
        c = win_c0 + w
        if w < n_win:
            s = _dot(qrows, win_ref[0, 0, 0:CH, w * CH:(w + 1) * CH].astype(bf16))
        else:
            s = _dot_nt(qrows, wknew_scr[...].astype(bf16))
        wd = t_pos - (c * CH + lane)
        return jnp.where((wd >= 0) & (wd <= WINDOW), s + bias_ref[c], NEG_INF)

    def win_pv(w, p):
        if w < n_win:
            return _dot_nt(p, win_ref[0, 0, CH:2 * CH, w * CH:(w + 1) * CH].astype(bf16))
        return _dot(p, wvnew_scr[...].astype(bf16))

    o_win = _rows_softmax_pv(n_win + 1, win_score, win_pv, s_scr, nrow)

    o = (jax.nn.sigmoid(gate_scr[0]) * o_cmp + jax.nn.sigmoid(gate_scr[1]) * o_slc
         + jax.nn.sigmoid(gate_scr[2]) * o_win)
    for h in range(H_C):
        g = h // R_C
        o_ref[0, :, h * HEAD_DIM:(h + 1) * HEAD_DIM] = o[h * t_new:(h + 1) * t_new, g * HEAD_DIM:(g + 1) * HEAD_DIM]


def _nsa_sample(page_table, layer, qc, ikp, kvc4, kvw, win_buf, wbd, pe256, nsa_pool, rowbias, cmprow):
    n, t_new, _ = qc.shape
    n_pages = page_table.shape[1]
    nc = n_pages + 1
    w_buf = win_buf.shape[3]
    assert w_buf % CH == 0 and w_buf <= n_pages * PAGE_SIZE <= COMP_STRIDE * CH
    n_win = w_buf // CH
    nrow = H_C * t_new
    assert 8 % t_new == 0 and G_C == 2 and rowbias.shape == (nc, nrow, CH)
    cover, n_sel = _cover_matrix(n_pages * PAGE_SIZE + t_new)
    cover = cover.T
    grp = np.zeros((8, nrow), np.float32)
    for h in range(H_C):
        for j in range(t_new):
            grp[(h // R_C) * t_new + j, h * t_new + j] = 1.0
    grp = jnp.asarray(grp, dtype=bf16)
    seq = lambda w: pl.BlockSpec((1, t_new, w), lambda b, pt: (b, 0, 0))
    grid_spec = pltpu.PrefetchScalarGridSpec(
        num_scalar_prefetch=1, grid=(n,),
        in_specs=[seq(512), seq(CH), seq(512), seq(256),
                  pl.BlockSpec((1, 1, 256, w_buf), lambda b, pt: (b, layer, 0, 0)),
                  _const_spec(wbd), _const_spec(pe256),
                  pl.BlockSpec((CH, CH), lambda b, pt: (0, 0)),
                  pl.BlockSpec((8, nrow), lambda b, pt: (0, 0)),
                  pl.BlockSpec((nc, nrow, CH), lambda b, pt: (0, 0, 0)),
                  pl.BlockSpec((1, nrow, CH), lambda b, pt: (0, 0, 0))]
        + _page_specs(n_pages, layer, 512),
        out_specs=pl.BlockSpec((1, t_new, 512), lambda b, pt: (b, 0, 0)),
        scratch_shapes=[pltpu.VMEM((nrow, CH), f32), pltpu.VMEM((3, nrow, CH), f32),
                        pltpu.VMEM((CH, CH), f32), pltpu.VMEM((CH, CH), f32),
                        pltpu.VMEM((CH, CH), f32), pltpu.VMEM((CH, CH), f32),
                        pltpu.VMEM((n_pages * PAGE_SIZE, CH), f32), pltpu.VMEM((n_pages * PAGE_SIZE, CH), f32),
                        pltpu.VMEM((nc, nrow, CH), f32)])
    return pl.pallas_call(
        functools.partial(_nsa_sample_kernel, n_pages=n_pages, n_sel=n_sel, n_win=n_win),
        grid_spec=grid_spec, out_shape=jax.ShapeDtypeStruct((n, t_new, 512), f32),
        compiler_params=_cparams(1),
    )(page_table, qc, ikp, kvc4, kvw, win_buf, wbd, pe256, cover, grp, rowbias, cmprow, *([nsa_pool] * n_pages))


def _conv_kernel(glu_ref, hist_ref, cw_ref, cb_ref, lg_ref, lb_ref, wob_ref, ob_ref, rows_ref, ext_scr, *, tm, stride):
    hp = ext_scr.shape[0] - tm
    keep = (CONV_W - 1) * stride
    i = pl.program_id(1)

    @pl.when(i == 0)
    def _():
        ext_scr[0:hp, :] = hist_ref[0]

    glu = glu_ref[0]
    ext_scr[hp:hp + tm, :] = glu[:, 0:D_CONV] * jax.nn.sigmoid(glu[:, D_CONV:2 * D_CONV])
    y = jnp.zeros((tm, D_CONV), f32) + cb_ref[...]
    for w in range(CONV_W):
        y = y + ext_scr[hp - keep + w * stride:hp - keep + w * stride + tm, :] * cw_ref[w:w + 1, :]
    mu = jnp.mean(y, axis=-1, keepdims=True)
    var = jnp.mean(jnp.square(y - mu), axis=-1, keepdims=True)
    yn = (y - mu) * lax.rsqrt(var + EPS) * lg_ref[...] + lb_ref[...]
    act = (yn * jax.nn.sigmoid(yn)).astype(bf16)
    ob_ref[0] = _dot(act, wob_ref[...])
    tail = ext_scr[hp + tm - keep:hp + tm, :]
    rows_ref[0] = tail
    ext_scr[hp - keep:hp, :] = tail


def _conv_module(glu, hist, cw, cb, lg, lb, wob, tm, stride):
    nb, t, _ = glu.shape
    hp = hist.shape[1]
    keep = (CONV_W - 1) * stride
    d = wob.shape[1]
    return pl.pallas_call(
        functools.partial(_conv_kernel, tm=tm, stride=stride),
        grid=(nb, t // tm),
        in_specs=[pl.BlockSpec((1, tm, 2 * D_CONV), lambda n, i: (n, i, 0)),
                  pl.BlockSpec((1, hp, D_CONV), lambda n, i: (n, 0, 0)),
                  _const_spec(cw), _const_spec(cb), _const_spec(lg), _const_spec(lb), _const_spec(wob)],
        out_specs=[pl.BlockSpec((1, tm, d), lambda n, i: (n, i, 0)),
                   pl.BlockSpec((1, keep, D_CONV), lambda n, i: (n, 0, 0))],
        out_shape=[jax.ShapeDtypeStruct((nb, t, d), f32), jax.ShapeDtypeStruct((nb, keep, D_CONV), f32)],
        scratch_shapes=[pltpu.VMEM((hp + tm, D_CONV), f32)],
        compiler_params=_cparams(2),
    )(glu, hist, cw, cb, lg, lb, wob)


def _merge_kernel(x_ref, gt_ref, oa_ref, ob_ref, oc_ref, gm_ref, woa_ref, woc_ref, wo_ref, o_ref):
    d = x_ref.shape[2]
    ya = _dot(oa_ref[0].astype(bf16), woa_ref[...])
    yc = _dot(oc_ref[0].astype(bf16), woc_ref[...])
    gm = gm_ref[0]
    merged = (jax.nn.sigmoid(gm[:, 0:d]) * ya + jax.nn.sigmoid(gm[:, d:2 * d]) * ob_ref[0]
              + jax.nn.sigmoid(gm[:, 2 * d:3 * d]) * yc)
    o_ref[0] = x_ref[0] + gt_ref[0] * _dot(merged.astype(bf16), wo_ref[...])


def _merge(x, gt, oa, ob, oc, gm, woa, woc, wo, tm):
    nb, t, d = x.shape
    tok = lambda w: pl.BlockSpec((1, tm, w), lambda n, i: (n, i, 0))
    return pl.pallas_call(
        _merge_kernel, grid=(nb, t // tm),
        in_specs=[tok(d), _mod_spec(gt, tm), tok(512), tok(d), tok(512), tok(3 * d),
                  _const_spec(woa), _const_spec(woc), _const_spec(wo)],
        out_specs=tok(d), out_shape=jax.ShapeDtypeStruct((nb, t, d), f32),
        compiler_params=_cparams(2),
    )(x, gt, oa, ob, oc, gm, woa, woc, wo)


def _ffn_kernel(x_ref, sh_ref, sc_ref, gt_ref, g_ref, hist_ref, wa_ref, wb_ref, cw_ref, cb_ref, wd_ref,
                o_ref, rows_ref, h_scr, acc_scr, ext_scr, carry_scr, *, tm, stride):
    hp = ext_scr.shape[0] - tm
    i = pl.program_id(1)
    j = pl.program_id(2)

    @pl.when(j == 0)
    def _():
        x = x_ref[0]
        y = x * lax.rsqrt(jnp.mean(x * x, axis=-1, keepdims=True) + EPS) * g_ref[...]
        h_scr[...] = (y * (1.0 + sc_ref[0]) + sh_ref[0]).astype(bf16)
        acc_scr[...] = jnp.zeros(acc_scr.shape, f32)

    @pl.when(i == 0)
    def _():
        ext_scr[0:hp, :] = hist_ref[...]

    @pl.when(i > 0)
    def _():
        ext_scr[0:hp, :] = carry_scr[j]

    hb = h_scr[...]
    a = _dot(hb, wa_ref[...])
    b = _dot(hb, wb_ref[...])
    ext_scr[hp:hp + tm, :] = a
    conv = (ext_scr[hp - 2 * stride:hp - 2 * stride + tm, :] * cw_ref[0:1, :]
            + ext_scr[hp - stride:hp - stride + tm, :] * cw_ref[1:2, :]
            + a * cw_ref[2:3, :] + cb_ref[...])
    act = (jax.nn.gelu(conv, approximate=True) * b).astype(bf16)
    acc_scr[...] += _dot(act, wd_ref[...])
    tail = ext_scr[tm:tm + hp, :]
    carry_scr[j] = tail
    rows_ref[0, 0] = tail

    @pl.when(j == pl.num_programs(2) - 1)
    def _():
        o_ref[0] = x_ref[0] + gt_ref[0] * acc_scr[...]


def _ffn(x, sh, sc, gt, g, hist, wa, wb, cw, cb, wd, tm, tf, stride):
    nb, t, d = x.shape
    dff = wa.shape[1]
    hp = hist.shape[0]
    nj = dff // tf
    tok = pl.BlockSpec((1, tm, d), lambda n, i, j: (n, i, 0))

    def mod_spec(m):
        if m.shape[1] == 1:
            return pl.BlockSpec((1, 1, d), lambda n, i, j: (n, 0, 0))
        return pl.BlockSpec((1, tm, d), lambda n, i, j: (n, i, 0))

    return pl.pallas_call(
        functools.partial(_ffn_kernel, tm=tm, stride=stride),
        grid=(nb, t // tm, nj),
        in_specs=[tok, mod_spec(sh), mod_spec(sc), mod_spec(gt),
                  pl.BlockSpec((1, d), lambda n, i, j: (0, 0)),
                  pl.BlockSpec((hp, tf), lambda n, i, j: (0, j)),
                  pl.BlockSpec((d, tf), lambda n, i, j: (0, j)),
                  pl.BlockSpec((d, tf), lambda n, i, j: (0, j)),
                  pl.BlockSpec((FFN_CONV_W, tf), lambda n, i, j: (0, j)),
                  pl.BlockSpec((1, tf), lambda n, i, j: (0, j)),
                  pl.BlockSpec((tf, d), lambda n, i, j: (j, 0))],
        out_specs=[tok, pl.BlockSpec((1, 1, hp, tf), lambda n, i, j: (n, i, 0, j))],
        out_shape=[jax.ShapeDtypeStruct((nb, t, d), f32), jax.ShapeDtypeStruct((nb, t // tm, hp, dff), f32)],
        scratch_shapes=[pltpu.VMEM((tm, d), bf16), pltpu.VMEM((tm, d), f32), pltpu.VMEM((hp + tm, tf), f32),
                        pltpu.VMEM((nj, hp, tf), f32)],
        compiler_params=_cparams(3),
    )(x, sh, sc, gt, g, hist, wa, wb, cw, cb, wd)


def _final_kernel(x_ref, g_ref, o_ref):
    x = x_ref[0]
    o_ref[0] = x * lax.rsqrt(jnp.mean(x * x, axis=-1, keepdims=True) + EPS) * g_ref[...]


def _final_norm(x, g, tm):
    nb, t, d = x.shape
    tok = pl.BlockSpec((1, tm, d), lambda n, i: (n, i, 0))
    return pl.pallas_call(
        _final_kernel, grid=(nb, t // tm), in_specs=[tok, pl.BlockSpec((1, d), lambda n, i: (0, 0))],
        out_specs=tok, out_shape=jax.ShapeDtypeStruct((nb, t, d), f32), compiler_params=_cparams(2),
    )(x, g)


def _layer_weights(w_in, w_cmp, pe_cmp):
    d = w_in.shape[0]
    offs = np.cumsum([0, H_A * HEAD_DIM, G_A * HEAD_DIM, G_A * HEAD_DIM, H_IDX * D_IDX, H_IDX, D_IDX, 2 * D_CONV,
                      H_C * HEAD_DIM, 6 * G_C * HEAD_DIM, 3 * H_C, 3 * d])
    qa, ka, va, iq, iw, ik, glu, qc, kvc, gc, gm = [w_in[:, offs[k]:offs[k + 1]] for k in range(11)]
    ikp = jnp.concatenate([ik, iw, gc, jnp.zeros((d, CH - D_IDX - H_IDX - 3 * H_C), f32)], axis=1)
    ws = [qa, jnp.concatenate([ka, va], axis=1), iq, ikp, glu, qc, kvc[:, 0:512], kvc[:, 512:768], gm]
    ws = [w.astype(bf16) for w in ws]
    misc = jnp.concatenate([iw, jnp.zeros((d, 8 - H_IDX), f32), gc, jnp.zeros((d, CH - 8 - 3 * H_C), f32)], axis=1)
    wft = jnp.concatenate([va, kvc[:, 384:512], kvc[:, 640:768], misc], axis=1).T.astype(bf16)
    wbd = jnp.zeros((2, COMP_BLOCK, CH, CH), f32)
    for s in range(2):
        blk = w_cmp[s].reshape(COMP_BLOCK, HEAD_DIM, HEAD_DIM)
        for g in range(G_C):
            wbd = wbd.at[s, :, g * HEAD_DIM:(g + 1) * HEAD_DIM, g * HEAD_DIM:(g + 1) * HEAD_DIM].set(blk)
    pe256 = jnp.concatenate([pe_cmp] * G_C, axis=2)
    return ws, wft, wbd.astype(bf16), pe256


def kernel(x_prompt, x_sample, c_prompt, c_sample, cache_dsa_kv, cache_dsa_idx, cache_nsa_kv, state_nsa_win,
           state_conv, state_ffn, page_table, rel_bias, w_mod, b_mod, g_mix, g_ffn, w_in, w_cmp, pe_cmp, conv_w,
           conv_b, ln_g, ln_b, w_oa, w_ob, w_oc, w_o, w_up, ffn_conv_w, ffn_conv_b, w_down, g_final):
    n_p, t_p, d = x_prompt.shape
    n_s, t_s, _ = x_sample.shape
    depth = w_mod.shape[0]
    dff = w_down.shape[1]
    n_pool = cache_dsa_kv.shape[0]
    n_pages = page_table.shape[1]
    past = n_pages * PAGE_SIZE
    assert t_p % CH == 0 and t_s < COMP_STRIDE and (n_s * t_s) % CH == 0 and n_s % 8 == 0
    tm_p = 256
    rows_s = n_s * t_s

    mod = _modulation(jnp.concatenate([c_prompt, c_sample], axis=0), w_mod, b_mod)
    toep = _toeplitz_bias(rel_bias)
    nqb = past // CH + 1
    cmpb = _cmp_bias(rel_bias, max(nqb, t_p // CH))
    rowbias_a = _row_bias(rel_bias, 0, H_A, t_s, past, nqb)
    rowbias_c = _row_bias(rel_bias, H_A, H_C, t_s, past, nqb)
    cmprow = _row_bias(rel_bias, H_A, H_C, t_s, past, 1, COMP_STRIDE, COMP_BLOCK - 1)

    idx_pool = jnp.transpose(cache_dsa_idx, (0, 1, 3, 2))
    kv_pool = jnp.transpose(cache_dsa_kv, (0, 1, 3, 4, 5, 2)).reshape(n_pool, depth, 512, PAGE_SIZE)
    nsa_pool = jnp.transpose(cache_nsa_kv, (0, 1, 3, 4, 5, 2)).reshape(n_pool, depth, 512, PAGE_SIZE)
    win_t = jnp.transpose(state_nsa_win, (0, 1, 3, 4, 5, 2)).reshape(n_s, depth, 256, -1)

    xp = x_prompt
    xs = jnp.transpose(x_sample, (1, 0, 2)).reshape(1, rows_s, d)
    zeros_conv = jnp.zeros((n_p, 32, D_CONV), f32)
    zeros_ffn = jnp.zeros((8, dff), f32)
    tf = dff // 2

    def to_seq(a):
        return jnp.transpose(a.reshape(t_s, n_s, a.shape[-1]), (1, 0, 2))

    def to_rows(a):
        return jnp.transpose(a, (1, 0, 2)).reshape(1, t_s * n_s, a.shape[-1])

    outs_p = [[] for _ in range(6)]
    outs_s = [[] for _ in range(6)]
    for l in range(depth):
        ws, wft, wbd, pe256 = _layer_weights(w_in[l], w_cmp[l], pe_cmp[l])
        m = [mod[l][:, k * d:(k + 1) * d] for k in range(6)]
        mp = [a[:n_p].reshape(n_p, 1, d) for a in m]
        ms = [jnp.tile(a[n_p:], (t_s, 1)).reshape(1, rows_s, d) for a in m]
        g1 = g_mix[l].reshape(1, d)
        g2 = g_ffn[l].reshape(1, d)
        cw, cb = conv_w[l], conv_b[l].reshape(1, D_CONV)
        lg, lb = ln_g[l].reshape(1, D_CONV), ln_b[l].reshape(1, D_CONV)
        woa, wob, woc, wo = [w[l].astype(bf16) for w in (w_oa, w_ob, w_oc, w_o)]
        wa, wb = w_up[l][:, :dff].astype(bf16), w_up[l][:, dff:].astype(bf16)
        wd = w_down[l].astype(bf16)
        fcw, fcb = ffn_conv_w[l], ffn_conv_b[l].reshape(1, dff)

        qa, kva, iq, ikp, glu, qc, kvc4, kvw, gm, ft = _inproj(xp, mp[0], mp[1], g1, ws, wft, tm_p)
        o_a = _dsa_prompt(qa, iq, ikp, kva, ft, toep)
        kc, kct = _compress_prompt(kvc4, wbd, pe256)
        o_c = _nsa_prompt(qc, ft, kc, kct, cmpb, kvc4, kvw, toep)
        o_b, conv_rows = _conv_module(glu, zeros_conv, cw, cb, lg, lb, wob, tm_p, 1)
        xp = _merge(xp, mp[2], o_a, o_b, o_c, gm, woa, woc, wo, tm_p)
        xp, ffn_rows = _ffn(xp, mp[3], mp[4], mp[5], g2, zeros_ffn, wa, wb, fcw, fcb, wd, 512, tf, 1)
        outs_p[0].append(kva.reshape(n_p, t_p, 2, G_A, HEAD_DIM))
        outs_p[1].append(ikp[:, :, :D_IDX])
        outs_p[2].append(kvc4.reshape(n_p, t_p, 4, G_C, HEAD_DIM))
        outs_p[3].append(kvw[:, t_p - min(WINDOW, t_p):].reshape(n_p, min(WINDOW, t_p), 2, G_C, HEAD_DIM))
        outs_p[4].append(conv_rows)
        outs_p[5].append(ffn_rows[:, -1, 8 - (FFN_CONV_W - 1):])

        qa, kva, iq, ikp, glu, qc, kvc4, kvw, gm, ft = _inproj(xs, ms[0], ms[1], g1, ws, wft, rows_s)
        ft_s = jnp.transpose(jnp.transpose(ft[0], (1, 0, 2)).reshape(FT_ROWS, t_s, n_s), (2, 0, 1))
        qa_s, kva_s, iq_s, ikp_s, qc_s, kvc4_s, kvw_s = [to_seq(a) for a in (qa, kva, iq, ikp, qc, kvc4, kvw)]
        o_a = _dsa_sample(page_table, l, qa_s, iq_s, kva_s, ikp_s, idx_pool, kv_pool, rowbias_a)
        o_c = _nsa_sample(page_table, l, qc_s, ikp_s, kvc4_s, kvw_s, win_t, wbd, pe256, nsa_pool, rowbias_c, cmprow)
        hist_c = jnp.transpose(state_conv[:, l], (1, 0, 2)).reshape(1, (CONV_W - 1) * n_s, D_CONV)
        o_b, conv_rows = _conv_module(glu, hist_c, cw, cb, lg, lb, wob, rows_s, n_s)
        xs = _merge(xs, ms[2], to_rows(o_a), o_b, to_rows(o_c), gm, woa, woc, wo, rows_s)
        hist_f = jnp.transpose(state_ffn[:, l], (1, 0, 2)).reshape((FFN_CONV_W - 1) * n_s, dff)
        xs, ffn_rows = _ffn(xs, ms[3], ms[4], ms[5], g2, hist_f, wa, wb, fcw, fcb, wd, rows_s, tf, n_s)
        outs_s[0].append(kva_s.reshape(n_s, t_s, 2, G_A, HEAD_DIM))
        outs_s[1].append(ikp_s[:, :, :D_IDX])
        outs_s[2].append(kvc4_s.reshape(n_s, t_s, 4, G_C, HEAD_DIM))
        outs_s[3].append(jnp.transpose(kvw_s, (0, 2, 1)))
        outs_s[4].append(jnp.transpose(conv_rows[0].reshape(CONV_W - 1, n_s, D_CONV), (1, 0, 2)))
        outs_s[5].append(jnp.transpose(ffn_rows[0, 0].reshape(FFN_CONV_W - 1, n_s, dff), (1, 0, 2)))

    y_p = _final_norm(xp, g_final.reshape(1, d), 512)
    y_s = to_seq(_final_norm(xs, g_final.reshape(1, d), rows_s))
    stk = lambda group: [jnp.stack(rows, axis=1) for rows in group]
    keep_w = min(WINDOW, past + t_s)
    win_all = jnp.concatenate([win_t, jnp.stack(outs_s[3], axis=1)], axis=3)
    win_all = win_all[..., win_all.shape[3] - keep_w:].reshape(n_s, depth, 2, G_C, HEAD_DIM, keep_w)
    win_out = jnp.transpose(win_all, (0, 1, 5, 2, 3, 4))
    out_s = stk(outs_s[:3]) + [win_out] + stk(outs_s[4:])
    return tuple([y_p, y_s] + stk(outs_p) + out_s)
```

```python
import functools
import math

import numpy as np
import jax
import jax.numpy as jnp
from jax import lax
from jax.experimental import pallas as pl
from jax.experimental.pallas import tpu as pltpu

HEAD_DIM = 64
H_A, G_A = 8, 4
R_A = H_A // G_A
H_IDX, D_IDX = 4, 64
DSA_TOPK = 256
D_CONV, CONV_W = 512, 31
H_C, G_C = 8, 2
R_C = H_C // G_C
COMP_BLOCK, COMP_STRIDE = 32, 16
SEL_BLOCK, SEL_TOPK = 64, 8
WINDOW = 512
FFN_CONV_W = 3
NUM_BUCKETS, MAX_DISTANCE = 32, 128
FORCE_SCORE = 1.0e4
EPS = 1e-6
PAGE_SIZE = 128

CH = 128
FT_ROWS = 640
VMEM_LIMIT = 56 * 1024 * 1024

f32, bf16, i32 = jnp.float32, jnp.bfloat16, jnp.int32
NEG_INF = float("-inf")


def _cparams(n_axes):
    return pltpu.CompilerParams(dimension_semantics=("arbitrary",) * n_axes, vmem_limit_bytes=VMEM_LIMIT)


def _dot(a, b):
    return jnp.dot(a, b, preferred_element_type=f32)


def _dot_nt(a, b):
    return lax.dot_general(a, b, (((1,), (1,)), ((), ())), preferred_element_type=f32)


def _colsum(x):
    return x.reshape(CH // 8, 8, CH).sum(axis=0)


def _colmax(x):
    return x.reshape(CH // 8, 8, CH).max(axis=0)


def _mod_kernel(c_ref, w_ref, b_ref, o_ref):
    c = c_ref[...]
    a = (c * jax.nn.sigmoid(c)).astype(bf16)
    o_ref[0] = _dot(a, w_ref[0].astype(bf16)) + b_ref[0]


def _modulation(c_all, w_mod, b_mod):
    depth, d, d6 = w_mod.shape
    n = c_all.shape[0]
    tn = 1024
    return pl.pallas_call(
        _mod_kernel,
        grid=(depth, d6 // tn),
        in_specs=[pl.BlockSpec((n, d), lambda l, j: (0, 0)),
                  pl.BlockSpec((1, d, tn), lambda l, j: (l, 0, j)),
                  pl.BlockSpec((1, 1, tn), lambda l, j: (l, 0, j))],
        out_specs=pl.BlockSpec((1, n, tn), lambda l, j: (l, 0, j)),
        out_shape=jax.ShapeDtypeStruct((depth, n, d6), f32),
        compiler_params=_cparams(2),
    )(c_all, w_mod, b_mod.reshape(depth, 1, d6))


def _t5_bucket(n):
    max_exact = NUM_BUCKETS // 2
    nf = jnp.maximum(n, 1).astype(f32)
    large = max_exact + (jnp.log(nf / max_exact) / math.log(MAX_DISTANCE / max_exact)
                         * (NUM_BUCKETS - max_exact)).astype(i32)
    return jnp.where(n < max_exact, n, jnp.minimum(large, NUM_BUCKETS - 1))


def _bias_from_dist(dist, tab_ref, h):
    b = _t5_bucket(jnp.maximum(dist, 0))
    out = jnp.zeros(dist.shape, f32)
    for k in range(NUM_BUCKETS):
        out = jnp.where(b == k, tab_ref[k, h], out)
    return out


def _toep_kernel(tab_ref, o_ref):
    h = pl.program_id(0)
    row = lax.broadcasted_iota(i32, (CH, CH), 0)
    lane = lax.broadcasted_iota(i32, (CH, CH), 1)
    o_ref[0, 0] = _bias_from_dist(jnp.full((CH, CH), 2 * CH, i32), tab_ref, h)
    o_ref[0, 1] = _bias_from_dist(CH + lane - row, tab_ref, h)
    o_ref[0, 2] = _bias_from_dist(lane - row, tab_ref, h)


def _toeplitz_bias(rel_bias):
    nh = rel_bias.shape[1]
    return pl.pallas_call(
        _toep_kernel,
        grid=(nh,),
        in_specs=[pl.BlockSpec(memory_space=pltpu.SMEM)],
        out_specs=pl.BlockSpec((1, 3, CH, CH), lambda h: (h, 0, 0, 0)),
        out_shape=jax.ShapeDtypeStruct((nh, 3, CH, CH), f32),
        compiler_params=_cparams(1),
    )(rel_bias)


def _cmpbias_kernel(tab_ref, o_ref):
    h = pl.program_id(0)
    qb = pl.program_id(1)
    row = lax.broadcasted_iota(i32, (CH, CH), 0)
    lane = lax.broadcasted_iota(i32, (CH, CH), 1)
    dist = qb * CH + lane - (row * COMP_STRIDE + COMP_BLOCK - 1)
    o_ref[0, 0] = _bias_from_dist(dist, tab_ref, H_A + h)


def _cmp_bias(rel_bias, nqb):
    return pl.pallas_call(
        _cmpbias_kernel,
        grid=(H_C, nqb),
        in_specs=[pl.BlockSpec(memory_space=pltpu.SMEM)],
        out_specs=pl.BlockSpec((1, 1, CH, CH), lambda h, q: (h, q, 0, 0)),
        out_shape=jax.ShapeDtypeStruct((H_C, nqb, CH, CH), f32),
        compiler_params=_cparams(2),
    )(rel_bias)


def _inproj_kernel(x_ref, sh_ref, sc_ref, g_ref, *refs, n_w):
    w_refs = refs[:n_w]
    wft_ref = refs[n_w]
    out_refs = refs[n_w + 1:2 * n_w + 1]
    ft_ref = refs[2 * n_w + 1]
    x = x_ref[0]
    y = x * lax.rsqrt(jnp.mean(x * x, axis=-1, keepdims=True) + EPS) * g_ref[...]
    hb = (y * (1.0 + sc_ref[0]) + sh_ref[0]).astype(bf16)
    for w_ref, o_ref in zip(w_refs, out_refs):
        o_ref[0] = _dot(hb, w_ref[...])
    ft = _dot_nt(wft_ref[...], hb)
    for j in range(ft.shape[1] // CH):
        ft_ref[0, j] = ft[:, j * CH:(j + 1) * CH]


def _mod_spec(m, tm):
    if m.shape[1] == 1:
        return pl.BlockSpec((1, 1, m.shape[2]), lambda n, i: (n, 0, 0))
    return pl.BlockSpec((1, tm, m.shape[2]), lambda n, i: (n, i, 0))


def _const_spec(a):
    nd = a.ndim
    return pl.BlockSpec(a.shape, lambda *_: (0,) * nd, pipeline_mode=pl.Buffered(1))


def _inproj(x, sh, sc, g, ws, wft, tm):
    nb, t, d = x.shape
    n_w = len(ws)
    in_specs = [pl.BlockSpec((1, tm, d), lambda n, i: (n, i, 0)), _mod_spec(sh, tm), _mod_spec(sc, tm),
                _const_spec(g)] + [_const_spec(w) for w in ws] + [_const_spec(wft)]
    out_specs = [pl.BlockSpec((1, tm, w.shape[1]), lambda n, i: (n, i, 0)) for w in ws]
    out_specs.append(pl.BlockSpec((1, tm // CH, FT_ROWS, CH), lambda n, i: (n, i, 0, 0)))
    out_shape = [jax.ShapeDtypeStruct((nb, t, w.shape[1]), f32) for w in ws]
    out_shape.append(jax.ShapeDtypeStruct((nb, t // CH, FT_ROWS, CH), f32))
    return pl.pallas_call(
        functools.partial(_inproj_kernel, n_w=n_w),
        grid=(nb, t // tm), in_specs=in_specs, out_specs=out_specs, out_shape=out_shape,
        compiler_params=_cparams(2),
    )(x, sh, sc, g, *ws, wft)


def _sortable(x):
    x = jnp.where(x == 0.0, 0.0, x)
    b = lax.bitcast_convert_type(x, i32)
    return jnp.where(b < 0, b ^ 0x7FFFFFFF, b)


def _head_operand(tile, src_half, dst_half, mask_other):
    if src_half != dst_half:
        tile = pltpu.roll(tile, 64, 1)
    if mask_other:
        lane = lax.broadcasted_iota(i32, tile.shape, 1)
        keep = (lane < 64) if dst_half == 0 else (lane >= 64)
        tile = jnp.where(keep, tile, 0.0)
    return tile.astype(bf16)


def _shr_pow2(x, g):
    return x // g if isinstance(x, int) else jnp.right_shift(x, g.bit_length() - 1)


def _chunk_loop(lo, hi, body, carry, group):
    if isinstance(lo, int) and isinstance(hi, int):
        for c in range(lo, hi):
            carry = body(c, carry)
        return carry

    def gbody(gi, carry):
        for k in range(group):
            carry = body(gi * group + k, carry)
        return carry

    return lax.fori_loop(_shr_pow2(lo, group), _shr_pow2(hi + group - 1, group), gbody, carry)


def _super_loop(lo, hi, body, carry, group):
    if isinstance(lo, int) and isinstance(hi, int):
        c = lo
        while c < hi:
            n = min(group, hi - c)
            carry = body(c, n, carry)
            c += n
        return carry
    return lax.fori_loop(_shr_pow2(lo, group), _shr_pow2(hi + group - 1, group),
                         lambda gi, carry: body(gi * group, group, carry), carry)


ATT_RUN = 4


def _attend(lo, hi, q_stack, get_k, extra_fn, get_vt, s_scr, acc_scr, run=None):
    n_groups = len(q_stack)
    rep = q_stack[0].shape[0] // CH
    nh = n_groups * rep

    def p1(c0, n, macc):
        adds = [extra_fn(c0 + k) for k in range(n)]
        macc = list(macc)
        for g in range(n_groups):
            s_all = _dot_nt(get_k(c0, g, n).astype(bf16), q_stack[g])
            for k in range(n):
                for r in range(rep):
                    h = g * rep + r
                    s = s_all[k * CH:(k + 1) * CH, r * CH:(r + 1) * CH] + adds[k](h, g)
                    s_scr[h, c0 + k] = s
                    macc[h] = jnp.maximum(macc[h], _colmax(s))
        return tuple(macc)

    def walk(body, carry):
        if run is not None:
            return body(lo, run, carry)
        return _super_loop(lo, hi, body, carry, ATT_RUN)

    macc = walk(p1, tuple(jnp.full((8, CH), NEG_INF, f32) for _ in range(nh)))
    m = []
    for h in range(nh):
        mh = jnp.max(macc[h], axis=0, keepdims=True)
        m.append(jnp.where(mh > NEG_INF, mh, 0.0))
    for g in range(n_groups):
        acc_scr[g] = jnp.zeros(acc_scr.shape[1:], f32)

    def p2(c0, n, lacc):
        lacc = list(lacc)
        for g in range(n_groups):
            rows = []
            for k in range(n):
                ps = []
                for r in range(rep):
                    h = g * rep + r
                    p = jnp.exp(s_scr[h, c0 + k] - m[h])
                    lacc[h] = lacc[h] + _colsum(p)
                    ps.append(p.astype(bf16))
                rows.append(jnp.concatenate(ps, axis=1))
            p_run = rows[0] if n == 1 else jnp.concatenate(rows, axis=0)
            acc_scr[g] += _dot(get_vt(c0, g, n).astype(bf16), p_run)
        return tuple(lacc)

    lacc = walk(p2, tuple(jnp.zeros((8, CH), f32) for _ in range(nh)))
    outs = []
    for h in range(nh):
        g, r = divmod(h, rep)
        inv = 1.0 / jnp.maximum(jnp.sum(lacc[h], axis=0, keepdims=True), 1e-30)
        outs.append(acc_scr[g, :, r * CH:(r + 1) * CH] * inv)
    return outs


def _rows(c0, n):
    start = c0 * CH if isinstance(c0, int) else pl.multiple_of(c0 * CH, CH)
    return pl.ds(start, n * CH)


def _vt_run(ref, c0, n, lo, hi):
    tiles = [ref[c0 + k, lo:hi, :] for k in range(n)]
    return tiles[0] if n == 1 else jnp.concatenate(tiles, axis=1)


def _dsa_core(i_abs, nci, topk, pos_bits, n_valid, q_tile, iq_tile, iw_t, get_ik, get_k, get_vt, toep_ref,
              keys_scr, mask_scr, s_scr, acc_scr, ot_scr):
    row = lax.broadcasted_iota(i32, (CH, CH), 0)
    lane = lax.broadcasted_iota(i32, (CH, CH), 1)
    t_pos = i_abs * CH + lane

    iq_stack = jnp.concatenate([_head_operand(iq_tile(h // 2), h % 2, 0, True) for h in range(H_IDX)], axis=0)

    def idx_body(c, carry):
        s_all = _dot_nt(get_ik(c).astype(bf16), iq_stack)
        acc = jnp.zeros((CH, CH), f32)
        for h in range(H_IDX):
            acc = acc + iw_t[h:h + 1, :] * jnp.maximum(s_all[:, h * CH:(h + 1) * CH], 0.0)
        acc = jnp.where(c * CH + row <= t_pos, acc, NEG_INF)
        keys_scr[c] = _sortable(acc)
        return carry

    _chunk_loop(0, nci, idx_body, 0, 4)

    def count(pred_fn):
        def body(c, acc):
            return acc + _colsum(jnp.where(pred_fn(c, keys_scr[c]), 1.0, 0.0))
        acc = _chunk_loop(0, nci, body, jnp.zeros((8, CH), f32), 4)
        return jnp.sum(acc, axis=0, keepdims=True)

    kf = float(topk)
    int_min = jnp.full((1, CH), -2 ** 31, i32)
    c0 = count(lambda c, k: k >= 0)
    cand = jnp.where(c0 >= kf, jnp.zeros((1, CH), i32), int_min)

    def bit_body(j, cand):
        trial = cand | jnp.left_shift(jnp.int32(1), 30 - j)
        cnt = count(lambda c, k: k >= trial)
        return jnp.where(cnt >= kf, trial, cand)

    cand = lax.fori_loop(0, 31, bit_body, cand)
    cnt_ge = count(lambda c, k: k >= cand)
    tied = (cnt_ge > kf) & (lane[0:1, :] < n_valid)
    has_tie = jnp.max(jnp.where(tied, 1.0, 0.0)) > 0.5

    @pl.when(jnp.logical_not(has_tie))
    def _():
        def mask_body(c, carry):
            sel = (keys_scr[c] >= cand) & (c * CH + row <= t_pos)
            mask_scr[c] = jnp.where(sel, 0.0, NEG_INF)
            return carry
        _chunk_loop(0, nci, mask_body, 0, 4)

    @pl.when(has_tie)
    def _():
        need = kf - count(lambda c, k: k > cand)

        def pos_body(j, pcut):
            trial = pcut | jnp.left_shift(jnp.int32(1), pos_bits - 1 - j)
            cnt = count(lambda c, k: (k == cand) & (c * CH + row < trial))
            return jnp.where(cnt < need, trial, pcut)

        pcut = lax.fori_loop(0, pos_bits, pos_body, jnp.zeros((1, CH), i32))

        def mask_body(c, carry):
            k = keys_scr[c]
            s_pos = c * CH + row
            sel = ((k > cand) | ((k == cand) & (s_pos <= pcut))) & (s_pos <= t_pos)
            mask_scr[c] = jnp.where(sel, 0.0, NEG_INF)
            return carry
        _chunk_loop(0, nci, mask_body, 0, 4)

    scale = HEAD_DIM ** -0.5
    q_stack = []
    for g in range(G_A):
        heads = [_head_operand(q_tile((g * R_A + r) // 2) * scale, (g * R_A + r) % 2, g % 2, True) for r in range(R_A)]
        q_stack.append(jnp.concatenate(heads, axis=0))

    def extra_fn(c):
        bidx = jnp.clip(c - i_abs + 2, 0, 2)
        mk = mask_scr[c]
        return lambda h, g: toep_ref[h, bidx] + mk

    outs = _attend(0, nci, q_stack, lambda c0, g, n: get_k(c0, g // 2, n), extra_fn, get_vt, s_scr, acc_scr)
    for h in range(H_A):
        ot_scr[h * HEAD_DIM:(h + 1) * HEAD_DIM, :] = outs[h]
    return ot_scr[...].T


def _dsa_prompt_kernel(q_ref, iq_ref, misc_ref, ik_ref, k_ref, vt_ref, toep_ref, o_ref,
                       keys_scr, mask_scr, s_scr, acc_scr, ot_scr, *, topk, pos_bits):
    i = pl.program_id(1)

    def rows(c):
        return pl.ds(pl.multiple_of(c * CH, CH), CH)

    o_ref[0] = _dsa_core(
        i, i + 1, topk, pos_bits, CH,
        lambda j: q_ref[0, :, j * CH:(j + 1) * CH],
        lambda j: iq_ref[0, :, j * CH:(j + 1) * CH],
        misc_ref[0, 0, 0:8, :],
        lambda c: ik_ref[0, rows(c), :],
        lambda c0, j, n: k_ref[0, _rows(c0, n), j * CH:(j + 1) * CH],
        lambda c0, g, n: _vt_run(vt_ref.at[0], c0, n, g * HEAD_DIM, (g + 1) * HEAD_DIM),
        toep_ref, keys_scr, mask_scr, s_scr, acc_scr, ot_scr)


def _dsa_prompt(qa, iq, ikp, kva, ft, toep):
    n, t, _ = qa.shape
    nc = t // CH
    topk = min(DSA_TOPK, t // 4)
    assert nc % 4 == 0, "traced chunk loops walk aligned groups of up to 4 chunks"
    kern = functools.partial(_dsa_prompt_kernel, topk=topk, pos_bits=max(1, (t - 1).bit_length()))
    return pl.pallas_call(
        kern,
        grid=(n, nc),
        in_specs=[pl.BlockSpec((1, CH, 512), lambda b, i: (b, i, 0)),
                  pl.BlockSpec((1, CH, 256), lambda b, i: (b, i, 0)),
                  pl.BlockSpec((1, 1, CH, CH), lambda b, i: (b, i, 4, 0)),
                  pl.BlockSpec((1, t, CH), lambda b, i: (b, 0, 0)),
                  pl.BlockSpec((1, t, 256), lambda b, i: (b, 0, 0)),
                  pl.BlockSpec((1, nc, 256, CH), lambda b, i: (b, 0, 0, 0)),
                  pl.BlockSpec((H_A, 3, CH, CH), lambda b, i: (0, 0, 0, 0))],
        out_specs=pl.BlockSpec((1, CH, 512), lambda b, i: (b, i, 0)),
        out_shape=jax.ShapeDtypeStruct((n, t, 512), f32),
        scratch_shapes=[pltpu.VMEM((nc, CH, CH), i32), pltpu.VMEM((nc, CH, CH), f32),
                        pltpu.VMEM((H_A, nc, CH, CH), f32), pltpu.VMEM((G_A, HEAD_DIM, R_A * CH), f32),
                        pltpu.VMEM((512, CH), f32)],
        compiler_params=_cparams(2),
    )(qa, iq, ft, ikp, kva, ft, toep)


def _pad_rows(ref, x):
    ref[...] = jnp.zeros(ref.shape, f32)
    ref[0:x.shape[0], :] = x


def _rowbias_kernel(tab_ref, o_ref, *, head0, q0, n_heads, t_new, key_mul, key_add):
    c = pl.program_id(0)
    nrow = n_heads * t_new
    row = lax.broadcasted_iota(i32, (nrow, CH), 0)
    lane = lax.broadcasted_iota(i32, (nrow, CH), 1)
    hrow = row // t_new
    b = _t5_bucket(jnp.maximum(q0 + (row - hrow * t_new) - ((c * CH + lane) * key_mul + key_add), 0))
    out = jnp.zeros((nrow, CH), f32)
    for k in range(NUM_BUCKETS):
        for h in range(n_heads):
            out = jnp.where((b == k) & (hrow == h), tab_ref[k, head0 + h], out)
    o_ref[0] = out


def _row_bias(rel_bias, head0, n_heads, t_new, q0, nc, key_mul=1, key_add=0):
    nrow = n_heads * t_new
    return pl.pallas_call(
        functools.partial(_rowbias_kernel, head0=head0, q0=q0, n_heads=n_heads, t_new=t_new, key_mul=key_mul,
                          key_add=key_add),
        grid=(nc,),
        in_specs=[pl.BlockSpec(memory_space=pltpu.SMEM)],
        out_specs=pl.BlockSpec((1, nrow, CH), lambda c: (c, 0, 0)),
        out_shape=jax.ShapeDtypeStruct((nc, nrow, CH), f32),
        compiler_params=_cparams(1),
    )(rel_bias)


def _rowsum(x):
    return jnp.sum(x, axis=1, keepdims=True)


def _dsa_sample_kernel(pt_ref, q_ref, iq_ref, kvn_ref, ikn_ref, *refs, n_pages, topk, pos_bits):
    idx_pages = refs[:n_pages]
    kv_pages = refs[n_pages:2 * n_pages]
    (bias_ref, o_ref, qrow_scr, iqrow_scr, wrow_scr, iknew_scr, knew_scr, vnew_scr,
     keys_scr, mask_scr, s_scr) = refs[2 * n_pages:]
    t_new = q_ref.shape[1]
    nrow = H_A * t_new
    nc = n_pages + 1
    q0 = n_pages * PAGE_SIZE
    lane8 = lax.broadcasted_iota(i32, (8, CH), 1)
    row8 = lax.broadcasted_iota(i32, (8, CH), 0)

    iqrow_scr[...] = jnp.zeros(iqrow_scr.shape, f32)
    wrow_scr[...] = jnp.zeros(wrow_scr.shape, f32)
    ikn = ikn_ref[0]
    for hi in range(H_IDX):
        iqrow_scr[hi * 8:hi * 8 + t_new, 0:D_IDX] = iq_ref[0][:, hi * D_IDX:(hi + 1) * D_IDX]
        wrow_scr[hi * 8:hi * 8 + t_new, :] = jnp.broadcast_to(ikn[:, D_IDX + hi:D_IDX + hi + 1], (t_new, CH))
    qrow_scr[...] = jnp.zeros(qrow_scr.shape, f32)
    scale = HEAD_DIM ** -0.5
    for h in range(H_A):
        g = h // R_A
        qrow_scr[h * t_new:(h + 1) * t_new, g * HEAD_DIM:(g + 1) * HEAD_DIM] = (
            q_ref[0][:, h * HEAD_DIM:(h + 1) * HEAD_DIM] * scale)
    _pad_rows(iknew_scr, ikn)
    _pad_rows(knew_scr, kvn_ref[0][:, 0:256])
    _pad_rows(vnew_scr, kvn_ref[0][:, 256:512])
    iqrows = iqrow_scr[...].astype(bf16)
    qrows = qrow_scr[...].astype(bf16)
    wrow = wrow_scr[...]

    def idx_keys(s, valid):
        s = jnp.maximum(s, 0.0) * wrow
        acc = s[0:8] + s[8:16] + s[16:24] + s[24:32]
        if valid is not None:
            acc = jnp.where(valid, acc, NEG_INF)
        return _sortable(acc)

    zpad = jnp.zeros((CH - D_IDX, CH), f32)
    for p in range(n_pages):
        ikt = jnp.concatenate([idx_pages[p][0, 0], zpad], axis=0).astype(bf16)
        keys_scr[p] = idx_keys(_dot(iqrows, ikt), None)
    keys_scr[n_pages] = idx_keys(_dot_nt(iqrows, iknew_scr[...].astype(bf16)), lane8 <= row8)

    def count(pred_fn):
        acc = jnp.zeros((8, CH), f32)
        for c in range(nc):
            acc = acc + jnp.where(pred_fn(c, keys_scr[c]), 1.0, 0.0)
        return _rowsum(acc)

    def reach(base, trials):
        accs = [jnp.zeros((8, CH), f32) for _ in trials]
        for c in range(nc):
            k = keys_scr[c]
            for i, t in enumerate(trials):
                accs[i] = accs[i] + jnp.where(k >= t, 1.0, 0.0)
        for a, t in zip(accs, trials):
            base = jnp.where(_rowsum(a) >= kf, t, base)
        return base

    kf = float(topk)
    cand = reach(jnp.full((8, 1), -2 ** 31, i32), [jnp.zeros((8, 1), i32)])
    cand = reach(cand, [cand | (1 << 30)])

    def bits_body(j, cand):
        sh = 27 - 3 * j
        return reach(cand, [cand | jnp.left_shift(jnp.int32(v), sh) for v in range(1, 8)])

    cand = lax.fori_loop(0, 10, bits_body, cand)
    cnt_ge = count(lambda c, k: k >= cand)
    tied = (cnt_ge > kf) & (row8[:, 0:1] < t_new)
    has_tie = jnp.max(jnp.where(tied, 1.0, 0.0)) > 0.5

    def adm(c):
        return (c * CH + lane8 <= q0 + row8) if c == n_pages else None

    @pl.when(jnp.logical_not(has_tie))
    def _():
        for c in range(nc):
            sel = keys_scr[c] >= cand
            if adm(c) is not None:
                sel = sel & adm(c)
            mask_scr[c] = jnp.where(sel, 0.0, NEG_INF)

    @pl.when(has_tie)
    def _():
        need = kf - count(lambda c, k: k > cand)

        def pos_body(j, pcut):
            trial = pcut | jnp.left_shift(jnp.int32(1), pos_bits - 1 - j)
            cnt = count(lambda c, k: (k == cand) & (c * CH + lane8 < trial))
            return jnp.where(cnt < need, trial, pcut)

        pcut = lax.fori_loop(0, pos_bits, pos_body, jnp.zeros((8, 1), i32))
        for c in range(nc):
            k = keys_scr[c]
            sel = (k > cand) | ((k == cand) & (c * CH + lane8 <= pcut))
            if adm(c) is not None:
                sel = sel & adm(c)
            mask_scr[c] = jnp.where(sel, 0.0, NEG_INF)

    def rows_mask(c):
        m8 = mask_scr[c]
        m8 = jnp.where(row8 < t_new, m8, pltpu.roll(m8, t_new, 0))
        return jnp.concatenate([m8] * (nrow // 8), axis=0)

    macc = jnp.full((nrow, CH), NEG_INF, f32)
    for c in range(nc):
        if c < n_pages:
            s = _dot(qrows, kv_pages[c][0, 0][0:256, :].astype(bf16))
        else:
            s = _dot_nt(qrows, knew_scr[...].astype(bf16))
        s = s + bias_ref[c] + rows_mask(c)
        s_scr[c] = s
        macc = jnp.maximum(macc, s)
    m = jnp.max(macc, axis=1, keepdims=True)
    m = jnp.where(m > NEG_INF, m, 0.0)
    lacc = jnp.zeros((nrow, CH), f32)
    oacc = jnp.zeros((nrow, 256), f32)
    for c in range(nc):
        p = jnp.exp(s_scr[c] - m)
        lacc = lacc + p
        if c < n_pages:
            oacc = oacc + _dot_nt(p.astype(bf16), kv_pages[c][0, 0][256:512, :].astype(bf16))
        else:
            oacc = oacc + _dot(p.astype(bf16), vnew_scr[...].astype(bf16))
    o = oacc * (1.0 / jnp.maximum(_rowsum(lacc), 1e-30))
    for h in range(H_A):
        g = h // R_A
        o_ref[0, :, h * HEAD_DIM:(h + 1) * HEAD_DIM] = o[h * t_new:(h + 1) * t_new, g * HEAD_DIM:(g + 1) * HEAD_DIM]


def _page_specs(n_pages, layer, rows, row_block=0):
    return [pl.BlockSpec((1, 1, rows, PAGE_SIZE), lambda b, pt, p=p: (pt[b, p], layer, row_block, 0))
            for p in range(n_pages)]


def _dsa_sample(page_table, layer, qa, iq, kva, ikp, idx_pool, kv_pool, rowbias):
    n, t_new, _ = qa.shape
    n_pages = page_table.shape[1]
    nc = n_pages + 1
    lk = n_pages * PAGE_SIZE + t_new
    topk = min(DSA_TOPK, lk // 4)
    nrow = H_A * t_new
    assert nrow % 8 == 0 and t_new <= 4 and rowbias.shape == (nc, nrow, CH)
    kern = functools.partial(_dsa_sample_kernel, n_pages=n_pages, topk=topk, pos_bits=(nc * CH - 1).bit_length())
    seq = lambda w: pl.BlockSpec((1, t_new, w), lambda b, pt: (b, 0, 0))
    grid_spec = pltpu.PrefetchScalarGridSpec(
        num_scalar_prefetch=1, grid=(n,),
        in_specs=[seq(512), seq(256), seq(512), seq(CH)]
        + _page_specs(n_pages, layer, D_IDX) + _page_specs(n_pages, layer, 512)
        + [pl.BlockSpec((nc, nrow, CH), lambda b, pt: (0, 0, 0))],
        out_specs=pl.BlockSpec((1, t_new, 512), lambda b, pt: (b, 0, 0)),
        scratch_shapes=[pltpu.VMEM((nrow, 256), f32), pltpu.VMEM((H_IDX * 8, CH), f32), pltpu.VMEM((H_IDX * 8, CH), f32),
                        pltpu.VMEM((CH, CH), f32), pltpu.VMEM((CH, 256), f32), pltpu.VMEM((CH, 256), f32),
                        pltpu.VMEM((nc, 8, CH), i32), pltpu.VMEM((nc, 8, CH), f32), pltpu.VMEM((nc, nrow, CH), f32)])
    return pl.pallas_call(
        kern, grid_spec=grid_spec, out_shape=jax.ShapeDtypeStruct((n, t_new, 512), f32),
        compiler_params=_cparams(1),
    )(page_table, qa, iq, kva, ikp, *([idx_pool] * n_pages), *([kv_pool] * n_pages), rowbias)


def _compress_core(xk_ref, xv_ref, wbd_ref, pe_ref):
    nr = xk_ref.shape[0] // COMP_STRIDE
    halves = []
    for s, x_ref in enumerate((xk_ref, xv_ref)):
        a0 = jnp.zeros((nr, CH), f32)
        a1 = jnp.zeros((nr, CH), f32)
        for r in range(COMP_STRIDE):
            xr = x_ref[pl.ds(r, nr, stride=COMP_STRIDE), :]
            r2 = COMP_STRIDE + r
            a0 = a0 + _dot((xr + pe_ref[s, r:r + 1, :]).astype(bf16), wbd_ref[s, r])
            a1 = a1 + _dot((xr + pe_ref[s, r2:r2 + 1, :]).astype(bf16), wbd_ref[s, r2])
        kc = a0 + pltpu.roll(a1, nr - 1, 0)
        if nr < CH:
            kc = jnp.concatenate([kc, jnp.zeros((CH - nr, CH), f32)], axis=0)
        halves.append(kc)
    return halves


def _compress_prompt_kernel(xk_ref, xv_ref, wbd_ref, pe_ref, kc_ref, kct_ref):
    kck, kcv = _compress_core(xk_ref.at[0], xv_ref.at[0], wbd_ref, pe_ref)
    kc_ref[0, :, 0:CH] = kck
    kc_ref[0, :, CH:2 * CH] = kcv
    kct_ref[0] = kcv.T


def _compress_out(n):
    return ([pl.BlockSpec((1, CH, 256), lambda b, *_: (b, 0, 0)), pl.BlockSpec((1, CH, CH), lambda b, *_: (b, 0, 0))],
            [jax.ShapeDtypeStruct((n, CH, 256), f32), jax.ShapeDtypeStruct((n, CH, CH), f32)])


def _compress_prompt(kvc4, wbd, pe256):
    n, t, _ = kvc4.shape
    assert t % (8 * COMP_STRIDE) == 0 and t <= COMP_STRIDE * CH
    out_specs, out_shape = _compress_out(n)
    return pl.pallas_call(
        _compress_prompt_kernel, grid=(n,),
        in_specs=[pl.BlockSpec((1, t, CH), lambda b: (b, 0, 0)), pl.BlockSpec((1, t, CH), lambda b: (b, 0, 1)),
                  _const_spec(wbd), _const_spec(pe256)],
        out_specs=out_specs, out_shape=out_shape, compiler_params=_cparams(1),
    )(kvc4, kvc4, wbd, pe256)


def _split3(x):
    hi = x.astype(bf16)
    r1 = x - hi.astype(f32)
    mid = r1.astype(bf16)
    lo = (r1 - mid.astype(f32)).astype(bf16)
    return hi, mid, lo


def _nsa_core(i_abs, n_sel, win_c0, q_tile, gate_t, kc_ref, kct_ref, cmpb_ref, cover_ref, get_selk, get_selvt,
              get_wink, get_winvt, toep_ref, selm_scr, mask_scr, s_scr, acc_scr, ot_scr):
    row = lax.broadcasted_iota(i32, (CH, CH), 0)
    lane = lax.broadcasted_iota(i32, (CH, CH), 1)
    t_pos = i_abs * CH + lane
    scale = HEAD_DIM ** -0.5
    nci = i_abs + 1
    n_chunks_static = mask_scr.shape[1]
    ns_pad = -(-n_sel // 8) * 8

    q_stack = []
    for g in range(G_C):
        heads = [_head_operand(q_tile((g * R_C + r) // 2) * scale, (g * R_C + r) % 2, g, True) for r in range(R_C)]
        q_stack.append(jnp.concatenate(heads, axis=0))

    kc_b = kc_ref[0, :, 0:CH].astype(bf16)
    cmp_valid = t_pos >= row * COMP_STRIDE + COMP_BLOCK - 1
    o_cmp = []
    p_sum = []
    for g in range(G_C):
        s_all = _dot_nt(kc_b, q_stack[g])
        ps = []
        for r in range(R_C):
            h = g * R_C + r
            s = jnp.where(cmp_valid, s_all[:, r * CH:(r + 1) * CH] + cmpb_ref[h, 0], NEG_INF)
            m = jnp.max(s, axis=0, keepdims=True)
            e = jnp.exp(s - jnp.where(m > NEG_INF, m, 0.0))
            ps.append(e / jnp.maximum(jnp.sum(e, axis=0, keepdims=True), 1e-30))
        p_sum.append(ps[0] + ps[1] + ps[2] + ps[3])
        o_all = _dot(kct_ref[0, g * HEAD_DIM:(g + 1) * HEAD_DIM, :].astype(bf16),
                     jnp.concatenate([p.astype(bf16) for p in ps], axis=1))
        o_cmp += [o_all[:, r * CH:(r + 1) * CH] for r in range(R_C)]

    srow = lax.broadcasted_iota(i32, (ns_pad, CH), 0)
    cur = jnp.right_shift(i_abs * CH + lax.broadcasted_iota(i32, (ns_pad, CH), 1), SEL_BLOCK.bit_length() - 1)
    adm = srow <= cur
    forced = (srow == 0) | (srow == cur) | (srow == cur - 1)
    cov = cover_ref[...]
    for g in range(G_C):
        hi, mid, lo = _split3(p_sum[g])
        imp = (_dot(cov, hi) + _dot(cov, mid) + _dot(cov, lo))[0:ns_pad]
        score = jnp.where(adm, imp + jnp.where(forced, FORCE_SCORE, 0.0), NEG_INF)
        rank = jnp.zeros((ns_pad, CH), f32)
        for s2 in range(n_sel):
            other = score[s2:s2 + 1, :]
            ahead = (other > score) | ((other == score) & (s2 < srow))
            rank = rank + jnp.where(ahead, 1.0, 0.0)
        selm_scr[g, 0:ns_pad, :] = jnp.where(adm & (rank < float(SEL_TOPK)) & (srow < n_sel), 1.0, 0.0)
        for c in range(n_chunks_static):
            blk0 = selm_scr[g, 2 * c:2 * c + 1, :]
            blk1 = selm_scr[g, 2 * c + 1:2 * c + 2, :]
            on = jnp.where(row < SEL_BLOCK, blk0, blk1) > 0.5
            mask_scr[g, c] = jnp.where(on & (c * CH + row <= t_pos), 0.0, NEG_INF)

    def slc_extra(c):
        bidx = jnp.clip(c - i_abs + 2, 0, 2)
        return lambda h, g: toep_ref[h, bidx] + mask_scr[g, c]

    o_slc = _attend(0, nci, q_stack, lambda c0, g, n: get_selk(c0, n), slc_extra, get_selvt, s_scr, acc_scr)

    def win_extra(c):
        bidx = jnp.clip(c - i_abs + 2, 0, 2)
        wd = t_pos - (c * CH + row)
        wmask = jnp.where((wd >= 0) & (wd <= WINDOW), 0.0, NEG_INF)
        return lambda h, g: toep_ref[h, bidx] + wmask

    win_run = min(WINDOW // CH + 1, n_chunks_static)
    win_lo = jnp.clip(i_abs - WINDOW // CH, win_c0, n_chunks_static - win_run)
    o_win = _attend(win_lo, nci, q_stack, lambda c0, g, n: get_wink(c0, n), win_extra, get_winvt, s_scr, acc_scr,
                    run=win_run)

    for h in range(H_C):
        g0 = jax.nn.sigmoid(gate_t[8 + h:9 + h, :])
        g1 = jax.nn.sigmoid(gate_t[16 + h:17 + h, :])
        g2 = jax.nn.sigmoid(gate_t[24 + h:25 + h, :])
        ot_scr[h * HEAD_DIM:(h + 1) * HEAD_DIM, :] = g0 * o_cmp[h] + g1 * o_slc[h] + g2 * o_win[h]
    return ot_scr[...].T


def _nsa_prompt_kernel(q_ref, misc_ref, kc_ref, kct_ref, cmpb_ref, cover_ref, selk_ref, selvt_ref, wink_ref, winvt_ref,
                       toep_ref, o_ref, selm_scr, mask_scr, s_scr, acc_scr, ot_scr, *, n_sel):
    i = pl.program_id(1)

    def rows(c):
        return pl.ds(pl.multiple_of(c * CH, CH), CH)

    o_ref[0] = _nsa_core(
        i, n_sel, 0,
        lambda j: q_ref[0, :, j * CH:(j + 1) * CH],
        misc_ref[0, 0, 0:32, :],
        kc_ref, kct_ref, cmpb_ref, cover_ref,
        lambda c0, n: selk_ref[0, _rows(c0, n), :],
        lambda c0, g, n: _vt_run(selvt_ref.at[0], c0, n, g * HEAD_DIM, (g + 1) * HEAD_DIM),
        lambda c0, n: wink_ref[0, _rows(c0, n), :],
        lambda c0, g, n: _vt_run(winvt_ref.at[0], c0, n, g * HEAD_DIM, (g + 1) * HEAD_DIM),
        toep_ref, selm_scr, mask_scr, s_scr, acc_scr, ot_scr)


def _cover_matrix(n_keys):
    n_cmp = (n_keys - COMP_BLOCK) // COMP_STRIDE + 1
    n_sel = -(-n_keys // SEL_BLOCK)
    c0 = np.arange(n_cmp)[:, None] * COMP_STRIDE
    s0 = np.arange(n_sel)[None, :] * SEL_BLOCK
    cover = ((c0 < s0 + SEL_BLOCK) & (c0 + COMP_BLOCK > s0)).astype(np.float32)
    out = np.zeros((CH, CH), np.float32)
    out[:n_sel, :n_cmp] = cover.T
    return jnp.asarray(out, dtype=bf16), n_sel


def _nsa_prompt(qc, ft, kc, kct, cmpb, kvc4, kvw, toep):
    n, t, _ = qc.shape
    nc = t // CH
    cover, n_sel = _cover_matrix(t)
    return pl.pallas_call(
        functools.partial(_nsa_prompt_kernel, n_sel=n_sel),
        grid=(n, nc),
        in_specs=[pl.BlockSpec((1, CH, 512), lambda b, i: (b, i, 0)),
                  pl.BlockSpec((1, 1, CH, CH), lambda b, i: (b, i, 4, 0)),
                  pl.BlockSpec((1, CH, 256), lambda b, i: (b, 0, 0)),
                  pl.BlockSpec((1, CH, CH), lambda b, i: (b, 0, 0)),
                  pl.BlockSpec((H_C, 1, CH, CH), lambda b, i: (0, i, 0, 0)),
                  pl.BlockSpec((CH, CH), lambda b, i: (0, 0)),
                  pl.BlockSpec((1, t, CH), lambda b, i: (b, 0, 2)),
                  pl.BlockSpec((1, nc, CH, CH), lambda b, i: (b, 0, 2, 0)),
                  pl.BlockSpec((1, t, CH), lambda b, i: (b, 0, 0)),
                  pl.BlockSpec((1, nc, CH, CH), lambda b, i: (b, 0, 3, 0)),
                  pl.BlockSpec((H_C, 3, CH, CH), lambda b, i: (1, 0, 0, 0))],
        out_specs=pl.BlockSpec((1, CH, 512), lambda b, i: (b, i, 0)),
        out_shape=jax.ShapeDtypeStruct((n, t, 512), f32),
        scratch_shapes=[pltpu.VMEM((G_C, CH, CH), f32), pltpu.VMEM((G_C, nc, CH, CH), f32),
                        pltpu.VMEM((H_C, nc, CH, CH), f32), pltpu.VMEM((G_C, HEAD_DIM, R_C * CH), f32),
                        pltpu.VMEM((512, CH), f32)],
        compiler_params=_cparams(2),
    )(qc, ft, kc, kct, cmpb, cover, kvc4, ft, kvw, ft, toep)


def _rows_softmax_pv(n, score_fn, pv_fn, s_scr, nrow):
    macc = jnp.full((nrow, CH), NEG_INF, f32)
    for i in range(n):
        s = score_fn(i)
        s_scr[i] = s
        macc = jnp.maximum(macc, s)
    m = jnp.max(macc, axis=1, keepdims=True)
    m = jnp.where(m > NEG_INF, m, 0.0)
    lacc = jnp.zeros((nrow, CH), f32)
    oacc = jnp.zeros((nrow, CH), f32)
    for i in range(n):
        p = jnp.exp(s_scr[i] - m)
        lacc = lacc + p
        oacc = oacc + pv_fn(i, p.astype(bf16))
    return oacc * (1.0 / jnp.maximum(_rowsum(lacc), 1e-30))


def _nsa_sample_kernel(pt_ref, q_ref, ikn_ref, kvn_ref, kvwn_ref, win_ref, wbd_ref, pe_ref, cover_ref, grp_ref, bias_ref,
                       cmpb_ref, *refs, n_pages, n_sel, n_win):
    pages = refs[:n_pages]
    o_ref, qrow_scr, gate_scr, knew_scr, vnew_scr, wknew_scr, wvnew_scr, xk_scr, xv_scr, s_scr = refs[n_pages:]
    t_new = q_ref.shape[1]
    nrow = H_C * t_new
    nc = n_pages + 1
    q0 = n_pages * PAGE_SIZE
    win_c0 = n_pages - n_win
    scale = HEAD_DIM ** -0.5
    row = lax.broadcasted_iota(i32, (nrow, CH), 0)
    lane = lax.broadcasted_iota(i32, (nrow, CH), 1)
    t_pos = q0 + row - (row // t_new) * t_new
    row8 = lax.broadcasted_iota(i32, (8, CH), 0)
    lane8 = lax.broadcasted_iota(i32, (8, CH), 1)

    qrow_scr[...] = jnp.zeros(qrow_scr.shape, f32)
    ikn = ikn_ref[0]
    gate_col0 = D_IDX + H_IDX
    for h in range(H_C):
        g = h // R_C
        qrow_scr[h * t_new:(h + 1) * t_new, g * HEAD_DIM:(g + 1) * HEAD_DIM] = (
            q_ref[0][:, h * HEAD_DIM:(h + 1) * HEAD_DIM] * scale)
        for br in range(3):
            col = gate_col0 + br * H_C + h
            gate_scr[br, h * t_new:(h + 1) * t_new, :] = jnp.broadcast_to(ikn[:, col:col + 1], (t_new, CH))
    qrows = qrow_scr[...].astype(bf16)
    _pad_rows(knew_scr, kvn_ref[0][:, 256:384])
    _pad_rows(vnew_scr, kvn_ref[0][:, 384:512])
    _pad_rows(wknew_scr, kvwn_ref[0][:, 0:CH])
    _pad_rows(wvnew_scr, kvwn_ref[0][:, CH:2 * CH])

    for p in range(n_pages):
        page = pages[p][0, 0]
        xk_scr[p * PAGE_SIZE:(p + 1) * PAGE_SIZE, :] = page[0:CH, :].T
        xv_scr[p * PAGE_SIZE:(p + 1) * PAGE_SIZE, :] = page[CH:2 * CH, :].T
    kck, kcv = _compress_core(xk_scr, xv_scr, wbd_ref, pe_ref)
    s = _dot_nt(qrows, kck.astype(bf16)) + cmpb_ref[0]
    s = jnp.where(t_pos >= lane * COMP_STRIDE + COMP_BLOCK - 1, s, NEG_INF)
    m = jnp.max(s, axis=1, keepdims=True)
    e = jnp.exp(s - jnp.where(m > NEG_INF, m, 0.0))
    p_cmp = e / jnp.maximum(_rowsum(e), 1e-30)
    o_cmp = _dot(p_cmp.astype(bf16), kcv.astype(bf16))

    grp = grp_ref[...]
    cov = cover_ref[...]
    p_sum = sum(_dot(grp, part) for part in _split3(p_cmp))
    imp = sum(_dot(part, cov) for part in _split3(p_sum))
    cur = jnp.right_shift(q0 + row8 - (row8 // t_new) * t_new, SEL_BLOCK.bit_length() - 1)
    adm = lane8 <= cur
    forced = (lane8 == 0) | (lane8 == cur) | (lane8 == cur - 1)
    score = jnp.where(adm, imp + jnp.where(forced, FORCE_SCORE, 0.0), NEG_INF)
    rank = jnp.zeros((8, CH), f32)
    for s2 in range(n_sel):
        other = score[:, s2:s2 + 1]
        rank = rank + jnp.where((other > score) | ((other == score) & (s2 < lane8)), 1.0, 0.0)
    selm = jnp.where(adm & (rank < float(SEL_TOPK)) & (lane8 < n_sel), 1.0, 0.0)
    swapped = pltpu.roll(selm, t_new, 0)
    tiles = [jnp.where(row8 < t_new, selm, swapped), jnp.where(row8 < t_new, swapped, selm)]
    selm_rows = jnp.concatenate([tiles[(8 * i // t_new) // R_C] for i in range(nrow // 8)], axis=0)

    def slc_mask(c):
        on = jnp.where(lane < SEL_BLOCK, selm_rows[:, 2 * c:2 * c + 1], selm_rows[:, 2 * c + 1:2 * c + 2]) > 0.5
        if c == n_pages:
            on = on & (c * CH + lane <= t_pos)
        return jnp.where(on, 0.0, NEG_INF)

    def slc_score(c):
        if c < n_pages:
            s = _dot(qrows, pages[c][0, 0][2 * CH:3 * CH, :].astype(bf16))
        else:
            s = _dot_nt(qrows, knew_scr[...].astype(bf16))
        return s + bias_ref[c] + slc_mask(c)

    def slc_pv(c, p):
        if c < n_pages:
            return _dot_nt(p, pages[c][0, 0][3 * CH:4 * CH, :].astype(bf16))
        return _dot(p, vnew_scr[...].astype(bf16))

    o_slc = _rows_softmax_pv(nc, slc_score, slc_pv, s_scr, nrow)

    def win_score(w):
        c = win_c0 + w
        if w < n_win:
            s = _dot(qrows, win_ref[0, 0, 0:CH, w * CH:(w + 1) * CH].astype(bf16))
        else:
            s = _dot_nt(qrows, wknew_scr[...].astype(bf16))
        wd = t_pos - (c * CH + lane)
        return jnp.where((wd >= 0) & (wd <= WINDOW), s + bias_ref[c], NEG_INF)

    def win_pv(w, p):
        if w < n_win:
            return _dot_nt(p, win_ref[0, 0, CH:2 * CH, w * CH:(w + 1) * CH].astype(bf16))
        return _dot(p, wvnew_scr[...].astype(bf16))

    o_win = _rows_softmax_pv(n_win + 1, win_score, win_pv, s_scr, nrow)

    o = (jax.nn.sigmoid(gate_scr[0]) * o_cmp + jax.nn.sigmoid(gate_scr[1]) * o_slc
         + jax.nn.sigmoid(gate_scr[2]) * o_win)
    for h in range(H_C):
        g = h // R_C
        o_ref[0, :, h * HEAD_DIM:(h + 1) * HEAD_DIM] = o[h * t_new:(h + 1) * t_new, g * HEAD_DIM:(g + 1) * HEAD_DIM]


def _nsa_sample(page_table, layer, qc, ikp, kvc4, kvw, win_buf, wbd, pe256, nsa_pool, rowbias, cmprow):
    n, t_new, _ = qc.shape
    n_pages = page_table.shape[1]
    nc = n_pages + 1
    w_buf = win_buf.shape[3]
    assert w_buf % CH == 0 and w_buf <= n_pages * PAGE_SIZE <= COMP_STRIDE * CH
    n_win = w_buf // CH
    nrow = H_C * t_new
    assert 8 % t_new == 0 and G_C == 2 and rowbias.shape == (nc, nrow, CH)
    cover, n_sel = _cover_matrix(n_pages * PAGE_SIZE + t_new)
    cover = cover.T
    grp = np.zeros((8, nrow), np.float32)
    for h in range(H_C):
        for j in range(t_new):
            grp[(h // R_C) * t_new + j, h * t_new + j] = 1.0
    grp = jnp.asarray(grp, dtype=bf16)
    seq = lambda w: pl.BlockSpec((1, t_new, w), lambda b, pt: (b, 0, 0))
    grid_spec = pltpu.PrefetchScalarGridSpec(
        num_scalar_prefetch=1, grid=(n,),
        in_specs=[seq(512), seq(CH), seq(512), seq(256),
                  pl.BlockSpec((1, 1, 256, w_buf), lambda b, pt: (b, layer, 0, 0)),
                  _const_spec(wbd), _const_spec(pe256),
                  pl.BlockSpec((CH, CH), lambda b, pt: (0, 0)),
                  pl.BlockSpec((8, nrow), lambda b, pt: (0, 0)),
                  pl.BlockSpec((nc, nrow, CH), lambda b, pt: (0, 0, 0)),
                  pl.BlockSpec((1, nrow, CH), lambda b, pt: (0, 0, 0))]
        + _page_specs(n_pages, layer, 512),
        out_specs=pl.BlockSpec((1, t_new, 512), lambda b, pt: (b, 0, 0)),
        scratch_shapes=[pltpu.VMEM((nrow, CH), f32), pltpu.VMEM((3, nrow, CH), f32),
                        pltpu.VMEM((CH, CH), f32), pltpu.VMEM((CH, CH), f32),
                        pltpu.VMEM((CH, CH), f32), pltpu.VMEM((CH, CH), f32),
                        pltpu.VMEM((n_pages * PAGE_SIZE, CH), f32), pltpu.VMEM((n_pages * PAGE_SIZE, CH), f32),
                        pltpu.VMEM((nc, nrow, CH), f32)])
    return pl.pallas_call(
        functools.partial(_nsa_sample_kernel, n_pages=n_pages, n_sel=n_sel, n_win=n_win),
        grid_spec=grid_spec, out_shape=jax.ShapeDtypeStruct((n, t_new, 512), f32),
        compiler_params=_cparams(1),
    )(page_table, qc, ikp, kvc4, kvw, win_buf, wbd, pe256, cover, grp, rowbias, cmprow, *([nsa_pool] * n_pages))


def _conv_kernel(glu_ref, hist_ref, cw_ref, cb_ref, lg_ref, lb_ref, wob_ref, ob_ref, rows_ref, ext_scr, *, tm, stride):
    hp = ext_scr.shape[0] - tm
    keep = (CONV_W - 1) * stride
    i = pl.program_id(1)

    @pl.when(i == 0)
    def _():
        ext_scr[0:hp, :] = hist_ref[0]

    glu = glu_ref[0]
    ext_scr[hp:hp + tm, :] = glu[:, 0:D_CONV] * jax.nn.sigmoid(glu[:, D_CONV:2 * D_CONV])
    y = jnp.zeros((tm, D_CONV), f32) + cb_ref[...]
    for w in range(CONV_W):
        y = y + ext_scr[hp - keep + w * stride:hp - keep + w * stride + tm, :] * cw_ref[w:w + 1, :]
    mu = jnp.mean(y, axis=-1, keepdims=True)
    var = jnp.mean(jnp.square(y - mu), axis=-1, keepdims=True)
    yn = (y - mu) * lax.rsqrt(var + EPS) * lg_ref[...] + lb_ref[...]
    act = (yn * jax.nn.sigmoid(yn)).astype(bf16)
    ob_ref[0] = _dot(act, wob_ref[...])
    tail = ext_scr[hp + tm - keep:hp + tm, :]
    rows_ref[0] = tail
    ext_scr[hp - keep:hp, :] = tail


def _conv_module(glu, hist, cw, cb, lg, lb, wob, tm, stride):
    nb, t, _ = glu.shape
    hp = hist.shape[1]
    keep = (CONV_W - 1) * stride
    d = wob.shape[1]
    return pl.pallas_call(
        functools.partial(_conv_kernel, tm=tm, stride=stride),
        grid=(nb, t // tm),
        in_specs=[pl.BlockSpec((1, tm, 2 * D_CONV), lambda n, i: (n, i, 0)),
                  pl.BlockSpec((1, hp, D_CONV), lambda n, i: (n, 0, 0)),
                  _const_spec(cw), _const_spec(cb), _const_spec(lg), _const_spec(lb), _const_spec(wob)],
        out_specs=[pl.BlockSpec((1, tm, d), lambda n, i: (n, i, 0)),
                   pl.BlockSpec((1, keep, D_CONV), lambda n, i: (n, 0, 0))],
        out_shape=[jax.ShapeDtypeStruct((nb, t, d), f32), jax.ShapeDtypeStruct((nb, keep, D_CONV), f32)],
        scratch_shapes=[pltpu.VMEM((hp + tm, D_CONV), f32)],
        compiler_params=_cparams(2),
    )(glu, hist, cw, cb, lg, lb, wob)


def _merge_kernel(x_ref, gt_ref, oa_ref, ob_ref, oc_ref, gm_ref, woa_ref, woc_ref, wo_ref, o_ref):
    d = x_ref.shape[2]
    ya = _dot(oa_ref[0].astype(bf16), woa_ref[...])
    yc = _dot(oc_ref[0].astype(bf16), woc_ref[...])
    gm = gm_ref[0]
    merged = (jax.nn.sigmoid(gm[:, 0:d]) * ya + jax.nn.sigmoid(gm[:, d:2 * d]) * ob_ref[0]
              + jax.nn.sigmoid(gm[:, 2 * d:3 * d]) * yc)
    o_ref[0] = x_ref[0] + gt_ref[0] * _dot(merged.astype(bf16), wo_ref[...])


def _merge(x, gt, oa, ob, oc, gm, woa, woc, wo, tm):
    nb, t, d = x.shape
    tok = lambda w: pl.BlockSpec((1, tm, w), lambda n, i: (n, i, 0))
    return pl.pallas_call(
        _merge_kernel, grid=(nb, t // tm),
        in_specs=[tok(d), _mod_spec(gt, tm), tok(512), tok(d), tok(512), tok(3 * d),
                  _const_spec(woa), _const_spec(woc), _const_spec(wo)],
        out_specs=tok(d), out_shape=jax.ShapeDtypeStruct((nb, t, d), f32),
        compiler_params=_cparams(2),
    )(x, gt, oa, ob, oc, gm, woa, woc, wo)


def _ffn_kernel(x_ref, sh_ref, sc_ref, gt_ref, g_ref, hist_ref, wa_ref, wb_ref, cw_ref, cb_ref, wd_ref,
                o_ref, rows_ref, h_scr, acc_scr, ext_scr, carry_scr, *, tm, stride):
    hp = ext_scr.shape[0] - tm
    i = pl.program_id(1)
    j = pl.program_id(2)

    @pl.when(j == 0)
    def _():
        x = x_ref[0]
        y = x * lax.rsqrt(jnp.mean(x * x, axis=-1, keepdims=True) + EPS) * g_ref[...]
        h_scr[...] = (y * (1.0 + sc_ref[0]) + sh_ref[0]).astype(bf16)
        acc_scr[...] = jnp.zeros(acc_scr.shape, f32)

    @pl.when(i == 0)
    def _():
        ext_scr[0:hp, :] = hist_ref[...]

    @pl.when(i > 0)
    def _():
        ext_scr[0:hp, :] = carry_scr[j]

    hb = h_scr[...]
    a = _dot(hb, wa_ref[...])
    b = _dot(hb, wb_ref[...])
    ext_scr[hp:hp + tm, :] = a
    conv = (ext_scr[hp - 2 * stride:hp - 2 * stride + tm, :] * cw_ref[0:1, :]
            + ext_scr[hp - stride:hp - stride + tm, :] * cw_ref[1:2, :]
            + a * cw_ref[2:3, :] + cb_ref[...])
    act = (jax.nn.gelu(conv, approximate=True) * b).astype(bf16)
    acc_scr[...] += _dot(act, wd_ref[...])
    tail = ext_scr[tm:tm + hp, :]
    carry_scr[j] = tail
    rows_ref[0, 0] = tail

    @pl.when(j == pl.num_programs(2) - 1)
    def _():
        o_ref[0] = x_ref[0] + gt_ref[0] * acc_scr[...]


def _ffn(x, sh, sc, gt, g, hist, wa, wb, cw, cb, wd, tm, tf, stride):
    nb, t, d = x.shape
    dff = wa.shape[1]
    hp = hist.shape[0]
    nj = dff // tf
    tok = pl.BlockSpec((1, tm, d), lambda n, i, j: (n, i, 0))

    def mod_spec(m):
        if m.shape[1] == 1:
            return pl.BlockSpec((1, 1, d), lambda n, i, j: (n, 0, 0))
        return pl.BlockSpec((1, tm, d), lambda n, i, j: (n, i, 0))

    return pl.pallas_call(
        functools.partial(_ffn_kernel, tm=tm, stride=stride),
        grid=(nb, t // tm, nj),
        in_specs=[tok, mod_spec(sh), mod_spec(sc), mod_spec(gt),
                  pl.BlockSpec((1, d), lambda n, i, j: (0, 0)),
                  pl.BlockSpec((hp, tf), lambda n, i, j: (0, j)),
                  pl.BlockSpec((d, tf), lambda n, i, j: (0, j)),
                  pl.BlockSpec((d, tf), lambda n, i, j: (0, j)),
                  pl.BlockSpec((FFN_CONV_W, tf), lambda n, i, j: (0, j)),
                  pl.BlockSpec((1, tf), lambda n, i, j: (0, j)),
                  pl.BlockSpec((tf, d), lambda n, i, j: (j, 0))],
        out_specs=[tok, pl.BlockSpec((1, 1, hp, tf), lambda n, i, j: (n, i, 0, j))],
        out_shape=[jax.ShapeDtypeStruct((nb, t, d), f32), jax.ShapeDtypeStruct((nb, t // tm, hp, dff), f32)],
        scratch_shapes=[pltpu.VMEM((tm, d), bf16), pltpu.VMEM((tm, d), f32), pltpu.VMEM((hp + tm, tf), f32),
                        pltpu.VMEM((nj, hp, tf), f32)],
        compiler_params=_cparams(3),
    )(x, sh, sc, gt, g, hist, wa, wb, cw, cb, wd)


def _final_kernel(x_ref, g_ref, o_ref):
    x = x_ref[0]
    o_ref[0] = x * lax.rsqrt(jnp.mean(x * x, axis=-1, keepdims=True) + EPS) * g_ref[...]


def _final_norm(x, g, tm):
    nb, t, d = x.shape
    tok = pl.BlockSpec((1, tm, d), lambda n, i: (n, i, 0))
    return pl.pallas_call(
        _final_kernel, grid=(nb, t // tm), in_specs=[tok, pl.BlockSpec((1, d), lambda n, i: (0, 0))],
        out_specs=tok, out_shape=jax.ShapeDtypeStruct((nb, t, d), f32), compiler_params=_cparams(2),
    )(x, g)


def _layer_weights(w_in, w_cmp, pe_cmp):
    d = w_in.shape[0]
    offs = np.cumsum([0, H_A * HEAD_DIM, G_A * HEAD_DIM, G_A * HEAD_DIM, H_IDX * D_IDX, H_IDX, D_IDX, 2 * D_CONV,
                      H_C * HEAD_DIM, 6 * G_C * HEAD_DIM, 3 * H_C, 3 * d])
    qa, ka, va, iq, iw, ik, glu, qc, kvc, gc, gm = [w_in[:, offs[k]:offs[k + 1]] for k in range(11)]
    ikp = jnp.concatenate([ik, iw, gc, jnp.zeros((d, CH - D_IDX - H_IDX - 3 * H_C), f32)], axis=1)
    ws = [qa, jnp.concatenate([ka, va], axis=1), iq, ikp, glu, qc, kvc[:, 0:512], kvc[:, 512:768], gm]
    ws = [w.astype(bf16) for w in ws]
    misc = jnp.concatenate([iw, jnp.zeros((d, 8 - H_IDX), f32), gc, jnp.zeros((d, CH - 8 - 3 * H_C), f32)], axis=1)
    wft = jnp.concatenate([va, kvc[:, 384:512], kvc[:, 640:768], misc], axis=1).T.astype(bf16)
    wbd = jnp.zeros((2, COMP_BLOCK, CH, CH), f32)
    for s in range(2):
        blk = w_cmp[s].reshape(COMP_BLOCK, HEAD_DIM, HEAD_DIM)
        for g in range(G_C):
            wbd = wbd.at[s, :, g * HEAD_DIM:(g + 1) * HEAD_DIM, g * HEAD_DIM:(g + 1) * HEAD_DIM].set(blk)
    pe256 = jnp.concatenate([pe_cmp] * G_C, axis=2)
    return ws, wft, wbd.astype(bf16), pe256


def kernel(x_prompt, x_sample, c_prompt, c_sample, cache_dsa_kv, cache_dsa_idx, cache_nsa_kv, state_nsa_win,
           state_conv, state_ffn, page_table, rel_bias, w_mod, b_mod, g_mix, g_ffn, w_in, w_cmp, pe_cmp, conv_w,
           conv_b, ln_g, ln_b, w_oa, w_ob, w_oc, w_o, w_up, ffn_conv_w, ffn_conv_b, w_down, g_final):
    n_p, t_p, d = x_prompt.shape
    n_s, t_s, _ = x_sample.shape
    depth = w_mod.shape[0]
    dff = w_down.shape[1]
    n_pool = cache_dsa_kv.shape[0]
    n_pages = page_table.shape[1]
    past = n_pages * PAGE_SIZE
    assert t_p % CH == 0 and t_s < COMP_STRIDE and (n_s * t_s) % CH == 0 and n_s % 8 == 0
    tm_p = 256
    rows_s = n_s * t_s

    mod = _modulation(jnp.concatenate([c_prompt, c_sample], axis=0), w_mod, b_mod)
    toep = _toeplitz_bias(rel_bias)
    nqb = past // CH + 1
    cmpb = _cmp_bias(rel_bias, max(nqb, t_p // CH))
    rowbias_a = _row_bias(rel_bias, 0, H_A, t_s, past, nqb)
    rowbias_c = _row_bias(rel_bias, H_A, H_C, t_s, past, nqb)
    cmprow = _row_bias(rel_bias, H_A, H_C, t_s, past, 1, COMP_STRIDE, COMP_BLOCK - 1)

    idx_pool = jnp.transpose(cache_dsa_idx, (0, 1, 3, 2))
    kv_pool = jnp.transpose(cache_dsa_kv, (0, 1, 3, 4, 5, 2)).reshape(n_pool, depth, 512, PAGE_SIZE)
    nsa_pool = jnp.transpose(cache_nsa_kv, (0, 1, 3, 4, 5, 2)).reshape(n_pool, depth, 512, PAGE_SIZE)
    win_t = jnp.transpose(state_nsa_win, (0, 1, 3, 4, 5, 2)).reshape(n_s, depth, 256, -1)

    xp = x_prompt
    xs = jnp.transpose(x_sample, (1, 0, 2)).reshape(1, rows_s, d)
    zeros_conv = jnp.zeros((n_p, 32, D_CONV), f32)
    zeros_ffn = jnp.zeros((8, dff), f32)
    tf = dff // 2

    def to_seq(a):
        return jnp.transpose(a.reshape(t_s, n_s, a.shape[-1]), (1, 0, 2))

    def to_rows(a):
        return jnp.transpose(a, (1, 0, 2)).reshape(1, t_s * n_s, a.shape[-1])

    outs_p = [[] for _ in range(6)]
    outs_s = [[] for _ in range(6)]
    for l in range(depth):
        ws, wft, wbd, pe256 = _layer_weights(w_in[l], w_cmp[l], pe_cmp[l])
        m = [mod[l][:, k * d:(k + 1) * d] for k in range(6)]
        mp = [a[:n_p].reshape(n_p, 1, d) for a in m]
        ms = [jnp.tile(a[n_p:], (t_s, 1)).reshape(1, rows_s, d) for a in m]
        g1 = g_mix[l].reshape(1, d)
        g2 = g_ffn[l].reshape(1, d)
        cw, cb = conv_w[l], conv_b[l].reshape(1, D_CONV)
        lg, lb = ln_g[l].reshape(1, D_CONV), ln_b[l].reshape(1, D_CONV)
        woa, wob, woc, wo = [w[l].astype(bf16) for w in (w_oa, w_ob, w_oc, w_o)]
        wa, wb = w_up[l][:, :dff].astype(bf16), w_up[l][:, dff:].astype(bf16)
        wd = w_down[l].astype(bf16)
        fcw, fcb = ffn_conv_w[l], ffn_conv_b[l].reshape(1, dff)

        qa, kva, iq, ikp, glu, qc, kvc4, kvw, gm, ft = _inproj(xp, mp[0], mp[1], g1, ws, wft, tm_p)
        o_a = _dsa_prompt(qa, iq, ikp, kva, ft, toep)
        kc, kct = _compress_prompt(kvc4, wbd, pe256)
        o_c = _nsa_prompt(qc, ft, kc, kct, cmpb, kvc4, kvw, toep)
        o_b, conv_rows = _conv_module(glu, zeros_conv, cw, cb, lg, lb, wob, 2 * tm_p, 1)
        xp = _merge(xp, mp[2], o_a, o_b, o_c, gm, woa, woc, wo, 2 * tm_p)
        xp, ffn_rows = _ffn(xp, mp[3], mp[4], mp[5], g2, zeros_ffn, wa, wb, fcw, fcb, wd, 512, tf, 1)
        outs_p[0].append(kva.reshape(n_p, t_p, 2, G_A, HEAD_DIM))
        outs_p[1].append(ikp[:, :, :D_IDX])
        outs_p[2].append(kvc4.reshape(n_p, t_p, 4, G_C, HEAD_DIM))
        outs_p[3].append(kvw[:, t_p - min(WINDOW, t_p):].reshape(n_p, min(WINDOW, t_p), 2, G_C, HEAD_DIM))
        outs_p[4].append(conv_rows)
        outs_p[5].append(ffn_rows[:, -1, 8 - (FFN_CONV_W - 1):])

        qa, kva, iq, ikp, glu, qc, kvc4, kvw, gm, ft = _inproj(xs, ms[0], ms[1], g1, ws, wft, rows_s)
        ft_s = jnp.transpose(jnp.transpose(ft[0], (1, 0, 2)).reshape(FT_ROWS, t_s, n_s), (2, 0, 1))
        qa_s, kva_s, iq_s, ikp_s, qc_s, kvc4_s, kvw_s = [to_seq(a) for a in (qa, kva, iq, ikp, qc, kvc4, kvw)]
        o_a = _dsa_sample(page_table, l, qa_s, iq_s, kva_s, ikp_s, idx_pool, kv_pool, rowbias_a)
        o_c = _nsa_sample(page_table, l, qc_s, ikp_s, kvc4_s, kvw_s, win_t, wbd, pe256, nsa_pool, rowbias_c, cmprow)
        hist_c = jnp.transpose(state_conv[:, l], (1, 0, 2)).reshape(1, (CONV_W - 1) * n_s, D_CONV)
        o_b, conv_rows = _conv_module(glu, hist_c, cw, cb, lg, lb, wob, rows_s, n_s)
        xs = _merge(xs, ms[2], to_rows(o_a), o_b, to_rows(o_c), gm, woa, woc, wo, rows_s)
        hist_f = jnp.transpose(state_ffn[:, l], (1, 0, 2)).reshape((FFN_CONV_W - 1) * n_s, dff)
        xs, ffn_rows = _ffn(xs, ms[3], ms[4], ms[5], g2, hist_f, wa, wb, fcw, fcb, wd, rows_s, tf, n_s)
        outs_s[0].append(kva_s.reshape(n_s, t_s, 2, G_A, HEAD_DIM))
        outs_s[1].append(ikp_s[:, :, :D_IDX])
        outs_s[2].append(kvc4_s.reshape(n_s, t_s, 4, G_C, HEAD_DIM))
        outs_s[3].append(jnp.transpose(kvw_s, (0, 2, 1)))
        outs_s[4].append(jnp.transpose(conv_rows[0].reshape(CONV_W - 1, n_s, D_CONV), (1, 0, 2)))
        outs_s[5].append(jnp.transpose(ffn_rows[0, 0].reshape(FFN_CONV_W - 1, n_s, dff), (1, 0, 2)))

    y_p = _final_norm(xp, g_final.reshape(1, d), 512)
    y_s = to_seq(_final_norm(xs, g_final.reshape(1, d), rows_s))
    stk = lambda group: [jnp.stack(rows, axis=1) for rows in group]
    keep_w = min(WINDOW, past + t_s)
    win_all = jnp.concatenate([win_t, jnp.stack(outs_s[3], axis=1)], axis=3)
    win_all = win_all[..., win_all.shape[3] - keep_w:].reshape(n_s, depth, 2, G_C, HEAD_DIM, keep_w)
    win_out = jnp.transpose(win_all, (0, 1, 5, 2, 3, 4))
    out_s = stk(outs_s[:3]) + [win_out] + stk(outs_s[4:])
    return tuple([y_p, y_s] + stk(outs_p) + out_s)
```
